```python
import jax, jax.numpy as jnp
from jax import lax
import numpy as np

D_MODEL = 2048
BATCH = 2
SEQ = 4096
DEPTH = 2
DEC_BATCH = 8
DEC_SEQ = 16
PAST_LEN = 2048

CHUNK = 64
Q_BLOCK = 128
NORM_EPS = 1e-6
D_FF = 5632
N_ADA = 9
N_BRANCH = 3
BRANCH_WIDTH = 1024
ML_HEADS = 4
ML_DQK = 128
ML_DV = 256
ML_GATE_CAP = 15.0
MLA_HEADS = 8
MLA_NOPE = 128
MLA_ROPE = 64
MLA_V = 128
MLA_LATENT = 512
ROPE_BASE = 10000.0
RW_HEADS = 16
RW_HEAD = 64
RW_WIDTH = RW_HEADS * RW_HEAD
RW_DECAY_LORA = 64
RW_A_LORA = 64
RW_G_LORA = 128
RW_LN_EPS = 64e-5

ML_SIZES = (ML_HEADS * ML_DQK, ML_HEADS * ML_DQK, ML_HEADS * ML_DV, ML_HEADS * ML_DV, ML_HEADS, ML_HEADS)
MLA_SIZES = (MLA_HEADS * (MLA_NOPE + MLA_ROPE), MLA_LATENT, MLA_ROPE)
RW_SIZES = (RW_WIDTH, RW_WIDTH, RW_WIDTH, RW_DECAY_LORA, RW_A_LORA, RW_G_LORA)
ML_COLS = sum(ML_SIZES)
MLA_COLS = sum(MLA_SIZES)
RW_COLS = sum(RW_SIZES)
GATE_COLS = N_BRANCH * D_MODEL
IN_COLS = ML_COLS + MLA_COLS + RW_COLS + GATE_COLS

kernel_name = 'hybrid_streaming_mlstm_mla_rwkv7'


def _split(x, sizes):
    return jnp.split(x, [int(i) for i in np.cumsum(sizes)[:-1]], axis=-1)


def rms_norm(x, g, eps=NORM_EPS):
    xf = x.astype(jnp.float32)
    y = xf * lax.rsqrt(jnp.mean(xf * xf, axis=-1, keepdims=True) + eps)
    return (y * g.astype(jnp.float32)).astype(x.dtype)


def modulate(h, shift, scale):
    return h * (1.0 + scale) + shift


def softcap(x, cap=ML_GATE_CAP):
    return cap * jnp.tanh(x / cap)


def rope(x, pos):
    half = x.shape[-1] // 2
    inv = ROPE_BASE ** (-jnp.arange(half, dtype=jnp.float32) / half)
    ang = pos.astype(jnp.float32)[:, None] * inv[None, :]
    shape = (ang.shape[0],) + (1,) * (x.ndim - 3) + (half,)
    cos = jnp.cos(ang).reshape(shape)
    sin = jnp.sin(ang).reshape(shape)
    xf = x.astype(jnp.float32)
    x1, x2 = xf[..., :half], xf[..., half:]
    return jnp.concatenate([x1 * cos - x2 * sin, x2 * cos + x1 * sin], axis=-1).astype(x.dtype)


def swiglu(h, w_in, w_out):
    g, u = jnp.split(h @ w_in, 2, axis=-1)
    return (jax.nn.silu(g) * u) @ w_out


def mlstm_chunk(carry, inp):
    C, n, m = carry
    q, k, v, ig, lf = inp
    L = q.shape[1]
    b = jnp.cumsum(lf, axis=1)
    causal = jnp.tril(jnp.ones((L, L), dtype=bool))
    logD = b[:, :, None, :] - b[:, None, :, :] + ig[:, None, :, :]
    logD = jnp.where(causal[None, :, :, None], logD, -jnp.inf)
    g = b + m[:, None, :]
    m_t = jnp.maximum(g, jnp.max(logD, axis=2))
    Dm = jnp.exp(logD - m_t[:, :, None, :])
    inter = jnp.exp(g - m_t)
    qk = jnp.einsum('bthd,bshd->btsh', q, k) * Dm
    num = jnp.einsum('btsh,bshv->bthv', qk, v) + inter[..., None] * jnp.einsum('bhvd,bthd->bthv', C, q)
    den = jnp.sum(qk, axis=2) + inter * jnp.einsum('bhd,bthd->bth', n, q)
    h = num / jnp.maximum(jnp.abs(den), jnp.exp(-m_t))[..., None]
    m_new = m_t[:, -1]
    carry_w = jnp.exp(b[:, -1] + m - m_new)
    w_s = jnp.exp(b[:, -1:, :] - b + ig - m_new[:, None, :])
    C_new = carry_w[..., None, None] * C + jnp.einsum('bsh,bshv,bshd->bhvd', w_s, v, k)
    n_new = carry_w[..., None] * n + jnp.einsum('bsh,bshd->bhd', w_s, k)
    return (C_new, n_new, m_new), h


def mlstm_branch(zm, i_bias, f_bias, head_g, C0, n0, m0):
    B, T, _ = zm.shape
    q, k, v, o, ig, fg = _split(zm, ML_SIZES)
    q = q.reshape(B, T, ML_HEADS, ML_DQK).astype(jnp.float32)
    k = k.reshape(B, T, ML_HEADS, ML_DQK).astype(jnp.float32) * (ML_DQK ** -0.5)
    v = v.reshape(B, T, ML_HEADS, ML_DV).astype(jnp.float32)
    ig = softcap(ig.astype(jnp.float32) + i_bias)
    lf = jax.nn.log_sigmoid(softcap(fg.astype(jnp.float32) + f_bias))
    L = CHUNK if T % CHUNK == 0 else T
    nc = T // L
    to_blocks = lambda t: t.reshape((B, nc, L) + t.shape[2:]).swapaxes(0, 1)
    (C, n, m), h = lax.scan(mlstm_chunk, (C0, n0, m0), (to_blocks(q), to_blocks(k), to_blocks(v), to_blocks(ig), to_blocks(lf)))
    h = h.swapaxes(0, 1).reshape(B, T, ML_HEADS, ML_DV)
    h = rms_norm(h, head_g.reshape(ML_HEADS, ML_DV)).reshape(B, T, ML_HEADS * ML_DV)
    out = jax.nn.sigmoid(o.astype(jnp.float32)) * h
    return out.astype(zm.dtype), C, n, m


def block_chunk_attention(q_nope, q_rope, k_nope, k_rope, v, q_pos, k_pos):
    B, T, H, _ = q_nope.shape
    nb = T // Q_BLOCK if (T > Q_BLOCK and T % Q_BLOCK == 0) else 1
    bq = T // nb
    scale = (MLA_NOPE + MLA_ROPE) ** -0.5
    k_chunk = k_pos // CHUNK

    def one_block(blk):
        qn, qr, qp = blk
        s = (jnp.einsum('bqhd,bkhd->bhqk', qn, k_nope) + jnp.einsum('bqhd,bkd->bhqk', qr, k_rope)).astype(jnp.float32) * scale
        visible = k_chunk[None, :] <= (qp // CHUNK)[:, None]
        s = jnp.where(visible[None, None], s, -jnp.inf)
        pr = jax.nn.softmax(s, axis=-1).astype(v.dtype)
        return jnp.einsum('bhqk,bkhd->bqhd', pr, v)

    blocks = (q_nope.reshape(B, nb, bq, H, MLA_NOPE).swapaxes(0, 1),
              q_rope.reshape(B, nb, bq, H, MLA_ROPE).swapaxes(0, 1),
              q_pos.reshape(nb, bq))
    out = lax.map(one_block, blocks)
    return out.swapaxes(0, 1).reshape(B, T, H * MLA_V)


def mla_branch(za, pos, lat_past, krope_past, kv_norm_g, w_kv_b):
    B, T, _ = za.shape
    q, lat, kr = _split(za, MLA_SIZES)
    q = q.reshape(B, T, MLA_HEADS, MLA_NOPE + MLA_ROPE)
    q_nope, q_rope = q[..., :MLA_NOPE], rope(q[..., MLA_NOPE:], pos)
    lat = rms_norm(lat, kv_norm_g)
    kr = rope(kr, pos)
    lat_all = jnp.concatenate([lat_past.astype(lat.dtype), lat], axis=1)
    kr_all = jnp.concatenate([krope_past.astype(kr.dtype), kr], axis=1)
    Sk = lat_all.shape[1]
    kv = (lat_all @ w_kv_b).reshape(B, Sk, MLA_HEADS, MLA_NOPE + MLA_V)
    k_nope, v = kv[..., :MLA_NOPE], kv[..., MLA_NOPE:]
    out = block_chunk_attention(q_nope, q_rope, k_nope, kr_all, v, pos, jnp.arange(Sk))
    return out, lat, kr


def rwkv_step(S, inp):
    r, w, k, v, a, b = inp
    sa = jnp.einsum('bhvk,bhk->bhv', S, a)
    S = S * w[:, :, None, :] + sa[..., :, None] * b[..., None, :] + v[..., :, None] * k[..., None, :]
    y = jnp.einsum('bhvk,bhk->bhv', S, r)
    return S, y


def rwkv_branch(zr, shift0, S0, mu, w0, w2, a0, a2, g2, k_k, k_a, r_k, lnx_g, lnx_b):
    B, T, _ = zr.shape
    zf = zr.astype(jnp.float32)
    prev = jnp.concatenate([shift0[:, None, :].astype(jnp.float32), zf[:, :-1]], axis=1)
    xs = zf + (prev - zf) * mu
    r, k, v, wd, ad, gd = _split(xs, RW_SIZES)
    w_raw = -jax.nn.softplus(-(w0 + jnp.tanh(wd) @ w2)) - 0.5
    decay = jnp.exp(-jnp.exp(w_raw))
    a = jax.nn.sigmoid(a0 + ad @ a2)
    g = jax.nn.sigmoid(gd) @ g2
    kk = (k * k_k).reshape(B, T, RW_HEADS, RW_HEAD)
    kk = kk / jnp.maximum(jnp.sqrt(jnp.sum(kk * kk, axis=-1, keepdims=True)), 1e-12)
    k = k * (1.0 + (a - 1.0) * k_a)
    hs = lambda t: t.reshape(B, T, RW_HEADS, RW_HEAD)
    r_h, k_h, v_h, a_h = hs(r), hs(k), hs(v), hs(a)
    tm = lambda t: t.swapaxes(0, 1)
    S, y = lax.scan(rwkv_step, S0, (tm(r_h), tm(hs(decay)), tm(k_h), tm(v_h), tm(-kk), tm(kk * a_h)))
    y = y.swapaxes(0, 1)
    mean = jnp.mean(y, axis=-1, keepdims=True)
    var = jnp.mean(jnp.square(y - mean), axis=-1, keepdims=True)
    y = ((y - mean) * lax.rsqrt(var + RW_LN_EPS)).reshape(B, T, RW_WIDTH) * lnx_g + lnx_b
    y = y + (jnp.sum(r_h * k_h * r_k, axis=-1, keepdims=True) * v_h).reshape(B, T, RW_WIDTH)
    return (y * g).astype(zr.dtype), zf[:, -1], S


def layer(x, c, l, p, lat_past, krope_past, C0, n0, m0, S0, shift0):
    B, T, _ = x.shape
    mod = jax.nn.silu(c) @ p['ada_w'][l] + p['ada_b'][l]
    sh1, sc1, g1, sh2, sc2, g2, sh3, sc3, g3 = jnp.split(mod[:, None, :], N_ADA, axis=-1)
    h = modulate(rms_norm(x, p['norm_g'][l, 0]), sh1, sc1)
    x = x + 0.5 * g1 * swiglu(h, p['ffn_w_in'][l, 0], p['ffn_w_out'][l, 0])
    h = modulate(rms_norm(x, p['norm_g'][l, 1]), sh2, sc2)
    pos = lat_past.shape[1] + jnp.arange(T)
    z = h @ p['w_in'][l]
    zm, za, zr, zg = _split(z, (ML_COLS, MLA_COLS, RW_COLS, GATE_COLS))
    om, C, n, m = mlstm_branch(zm, p['mlstm_i_bias'][l], p['mlstm_f_bias'][l], p['mlstm_head_g'][l], C0, n0, m0)
    oa, lat, kr = mla_branch(za, pos, lat_past, krope_past, p['mla_kv_norm_g'][l], p['mla_w_kv_b'][l])
    orw, shift, S = rwkv_branch(zr, shift0, S0, p['rwkv_mu'][l], p['rwkv_w0'][l], p['rwkv_w2'][l], p['rwkv_a0'][l],
                                p['rwkv_a2'][l], p['rwkv_g2'][l], p['rwkv_k_k'][l], p['rwkv_k_a'][l], p['rwkv_r_k'][l],
                                p['rwkv_lnx_g'][l], p['rwkv_lnx_b'][l])
    o = jnp.stack([om, oa.astype(om.dtype), orw.astype(om.dtype)], axis=2)
    y_b = jnp.einsum('btnw,nwd->btnd', o, p['w_branch'][l])
    gates = jax.nn.sigmoid(zg + p['b_merge'][l]).reshape(B, T, N_BRANCH, D_MODEL)
    mixed = jnp.sum(gates * y_b, axis=2) @ p['w_out'][l]
    x = x + g2 * mixed
    h = modulate(rms_norm(x, p['norm_g'][l, 2]), sh3, sc3)
    x = x + 0.5 * g3 * swiglu(h, p['ffn_w_in'][l, 1], p['ffn_w_out'][l, 1])
    return x, (lat, kr, C, n, m, S, shift)


def trunk(x, c, p, lat_cache, kr_cache, C_st, n_st, m_st, S_st, sh_st):
    new = []
    for l in range(DEPTH):
        x, st = layer(x, c, l, p, lat_cache[l], kr_cache[l], C_st[l].astype(jnp.float32), n_st[l].astype(jnp.float32),
                      m_st[l].astype(jnp.float32), S_st[l].astype(jnp.float32), sh_st[l])
        new.append(st)
    y = rms_norm(x, p['final_norm_g'])
    stacked = tuple(jnp.stack(s, axis=0) for s in zip(*new))
    return y, stacked


def setup_inputs(seed: int = 0) -> dict:
    key = jax.random.key(seed)
    ks = iter(jax.random.split(key, 48))
    nrm = lambda shape, s: jax.random.normal(next(ks), shape, jnp.float32) * s
    D = D_MODEL
    return {
        'x_prompt': nrm((BATCH, SEQ, D), 1.0),
        'x_sample': nrm((DEC_BATCH, DEC_SEQ, D), 1.0),
        'c_prompt': nrm((BATCH, D), 1.0),
        'c_sample': nrm((DEC_BATCH, D), 1.0),
        'cache_mla_latent': nrm((DEPTH, DEC_BATCH, PAST_LEN, MLA_LATENT), 1.0),
        'cache_mla_krope': nrm((DEPTH, DEC_BATCH, PAST_LEN, MLA_ROPE), 1.0),
        'state_mlstm_C': nrm((DEPTH, DEC_BATCH, ML_HEADS, ML_DV, ML_DQK), 0.5),
        'state_mlstm_n': nrm((DEPTH, DEC_BATCH, ML_HEADS, ML_DQK), 0.5),
        'state_mlstm_m': nrm((DEPTH, DEC_BATCH, ML_HEADS), 1.0),
        'state_rwkv_S': nrm((DEPTH, DEC_BATCH, RW_HEADS, RW_HEAD, RW_HEAD), 0.5),
        'state_rwkv_shift': nrm((DEPTH, DEC_BATCH, RW_COLS), 1.0),
        'ada_w': nrm((DEPTH, D, N_ADA * D), 0.5 * D ** -0.5),
        'ada_b': nrm((DEPTH, N_ADA * D), 0.02),
        'norm_g': 1.0 + nrm((DEPTH, 3, D), 0.02),
        'ffn_w_in': nrm((DEPTH, 2, D, 2 * D_FF), D ** -0.5),
        'ffn_w_out': nrm((DEPTH, 2, D_FF, D), D_FF ** -0.5),
        'w_in': nrm((DEPTH, D, IN_COLS), D ** -0.5),
        'mlstm_i_bias': nrm((DEPTH, ML_HEADS), 0.1),
        'mlstm_f_bias': 3.0 + nrm((DEPTH, ML_HEADS), 0.5),
        'mlstm_head_g': 1.0 + nrm((DEPTH, ML_HEADS * ML_DV), 0.02),
        'mla_kv_norm_g': 1.0 + nrm((DEPTH, MLA_LATENT), 0.02),
        'mla_w_kv_b': nrm((DEPTH, MLA_LATENT, MLA_HEADS * (MLA_NOPE + MLA_V)), MLA_LATENT ** -0.5),
        'rwkv_mu': jax.random.uniform(next(ks), (DEPTH, RW_COLS), jnp.float32),
        'rwkv_w0': jax.random.uniform(next(ks), (DEPTH, RW_WIDTH), jnp.float32, -5.0, -0.5),
        'rwkv_w2': nrm((DEPTH, RW_DECAY_LORA, RW_WIDTH), 0.1),
        'rwkv_a0': nrm((DEPTH, RW_WIDTH), 0.1),
        'rwkv_a2': nrm((DEPTH, RW_A_LORA, RW_WIDTH), 0.1),
        'rwkv_g2': nrm((DEPTH, RW_G_LORA, RW_WIDTH), RW_G_LORA ** -0.5),
        'rwkv_k_k': 0.85 + nrm((DEPTH, RW_WIDTH), 0.05),
        'rwkv_k_a': 1.0 + nrm((DEPTH, RW_WIDTH), 0.05),
        'rwkv_r_k': nrm((DEPTH, RW_HEADS, RW_HEAD), 0.1),
        'rwkv_lnx_g': 1.0 + nrm((DEPTH, RW_WIDTH), 0.02),
        'rwkv_lnx_b': nrm((DEPTH, RW_WIDTH), 0.02),
        'w_branch': nrm((DEPTH, N_BRANCH, BRANCH_WIDTH, D), BRANCH_WIDTH ** -0.5),
        'b_merge': nrm((DEPTH, GATE_COLS), 0.1),
        'w_out': nrm((DEPTH, D, D), D ** -0.5),
        'final_norm_g': 1.0 + nrm((D,), 0.02),
    }


def reference(x_prompt, x_sample, c_prompt, c_sample, cache_mla_latent, cache_mla_krope, state_mlstm_C, state_mlstm_n,
              state_mlstm_m, state_rwkv_S, state_rwkv_shift, ada_w, ada_b, norm_g, ffn_w_in, ffn_w_out, w_in,
              mlstm_i_bias, mlstm_f_bias, mlstm_head_g, mla_kv_norm_g, mla_w_kv_b, rwkv_mu, rwkv_w0, rwkv_w2, rwkv_a0,
              rwkv_a2, rwkv_g2, rwkv_k_k, rwkv_k_a, rwkv_r_k, rwkv_lnx_g, rwkv_lnx_b, w_branch, b_merge, w_out,
              final_norm_g):
    p = dict(ada_w=ada_w, ada_b=ada_b, norm_g=norm_g, ffn_w_in=ffn_w_in, ffn_w_out=ffn_w_out, w_in=w_in,
             mlstm_i_bias=mlstm_i_bias, mlstm_f_bias=mlstm_f_bias, mlstm_head_g=mlstm_head_g,
             mla_kv_norm_g=mla_kv_norm_g, mla_w_kv_b=mla_w_kv_b, rwkv_mu=rwkv_mu, rwkv_w0=rwkv_w0, rwkv_w2=rwkv_w2,
             rwkv_a0=rwkv_a0, rwkv_a2=rwkv_a2, rwkv_g2=rwkv_g2, rwkv_k_k=rwkv_k_k, rwkv_k_a=rwkv_k_a,
             rwkv_r_k=rwkv_r_k, rwkv_lnx_g=rwkv_lnx_g, rwkv_lnx_b=rwkv_lnx_b, w_branch=w_branch, b_merge=b_merge,
             w_out=w_out, final_norm_g=final_norm_g)
    B = x_prompt.shape[0]
    f32 = jnp.float32
    y_p, (p_lat, p_kr, p_C, p_n, p_m, p_S, p_sh) = trunk(
        x_prompt, c_prompt, p,
        jnp.zeros((DEPTH, B, 0, MLA_LATENT), x_prompt.dtype), jnp.zeros((DEPTH, B, 0, MLA_ROPE), x_prompt.dtype),
        jnp.zeros((DEPTH, B, ML_HEADS, ML_DV, ML_DQK), f32), jnp.zeros((DEPTH, B, ML_HEADS, ML_DQK), f32),
        jnp.zeros((DEPTH, B, ML_HEADS), f32), jnp.zeros((DEPTH, B, RW_HEADS, RW_HEAD, RW_HEAD), f32),
        jnp.zeros((DEPTH, B, RW_COLS), f32))
    y_s, (s_lat, s_kr, s_C, s_n, s_m, s_S, s_sh) = trunk(
        x_sample, c_sample, p, cache_mla_latent, cache_mla_krope, state_mlstm_C, state_mlstm_n, state_mlstm_m,
        state_rwkv_S, state_rwkv_shift)
    return (y_p, y_s, p_lat, p_kr, p_C, p_n, p_m, p_S, p_sh, s_lat, s_kr, s_C, s_n, s_m, s_S, s_sh)
```

```python
import functools

import jax
import jax.numpy as jnp
import numpy as np
from jax import lax
from jax.experimental import pallas as pl
from jax.experimental.pallas import tpu as pltpu

F32 = jnp.float32
BF16 = jnp.bfloat16
HIGHEST = lax.Precision.HIGHEST

D_MODEL = 2048
CHUNK = 64
NORM_EPS = 1e-6
D_FF = 5632
N_ADA = 9
ML_HEADS, ML_DQK, ML_DV, ML_GATE_CAP = 4, 128, 256, 15.0
MLA_HEADS, MLA_NOPE, MLA_ROPE, MLA_V, MLA_LATENT = 8, 128, 64, 128, 512
ROPE_BASE = 10000.0
RW_HEADS, RW_HEAD = 16, 64
RW_WIDTH = RW_HEADS * RW_HEAD
RW_LN_EPS = 64e-5
RW_LORA = 256
RW_COLS = 3 * RW_WIDTH + RW_LORA
BRANCH_WIDTH = 1024

LANES = 128
VMEM_LIMIT = 56 * 1024 * 1024

OFF_ML_Q, OFF_ML_K, OFF_ML_V, OFF_ML_O = 0, 512, 1024, 2048
OFF_QN, OFF_QR = 3072, 4096
OFF_RW_R, OFF_RW_K, OFF_RW_V = 5120, 6144, 7168
OFF_GATE = 8192
OFF_LAT = 14336
OFF_RW_LORA = 14848
OFF_KR = 15104
OFF_IF = 15232
NZ = 15360


def _cparams(sem):
    return pltpu.CompilerParams(dimension_semantics=sem, vmem_limit_bytes=VMEM_LIMIT)


def _sigmoid(x):
    return 1.0 / (1.0 + jnp.exp(-x))


def _silu(x):
    return x * _sigmoid(x)


def _dot(a, b):
    return jnp.dot(a, b, preferred_element_type=F32)


def _dot_nt(a, b, precision=None):
    return lax.dot_general(a, b, (((1,), (1,)), ((), ())), precision=precision, preferred_element_type=F32)


def _dot_tn(a, b, precision=None):
    return lax.dot_general(a, b, (((0,), (0,)), ((), ())), precision=precision, preferred_element_type=F32)


class Rows:
    def __init__(self, B, T, tm_max):
        self.B, self.T, self.M = B, T, B * T
        if T >= 256:
            self.tm = min(tm_max, T)
            assert T % self.tm == 0
            self.tpg = T // self.tm
            self.R = 1
        else:
            self.tm = self.M
            self.tpg = 1
            self.R = self.M
        self.nm = self.M // self.tm

    def expand(self, v):
        if self.R == 1:
            return v[:, None, :]
        return jnp.repeat(v, self.T, axis=0)[None]

    def vec_spec(self, tn, col=lambda j: j):
        return pl.BlockSpec((None, self.R, tn), lambda i, j: (i // self.tpg, 0, col(j)))


def _ada_kernel(c_ref, w_ref, b_ref, o_ref):
    c = c_ref[...]
    o_ref[...] = _dot(_silu(c).astype(BF16), w_ref[...].astype(BF16)) + b_ref[...]


def ada_mod(c, w, b):
    R, D = c.shape
    N = w.shape[1]
    tn = 1024
    return pl.pallas_call(
        _ada_kernel,
        grid=(N // tn,),
        in_specs=[pl.BlockSpec((R, D), lambda j: (0, 0)),
                  pl.BlockSpec((D, tn), lambda j: (0, j)),
                  pl.BlockSpec((1, tn), lambda j: (0, j))],
        out_specs=pl.BlockSpec((R, tn), lambda j: (0, j)),
        out_shape=jax.ShapeDtypeStruct((R, N), F32),
        compiler_params=_cparams(("arbitrary",)),
        name="ada_mod",
    )(c, w, b.reshape(1, N))


def _norm_mod_kernel(x_ref, g_ref, sh_ref, sc_ref, o_ref):
    x = x_ref[...]
    y = x * lax.rsqrt(jnp.mean(x * x, axis=-1, keepdims=True) + NORM_EPS) * g_ref[...]
    o_ref[...] = (y * (1.0 + sc_ref[...]) + sh_ref[...]).astype(o_ref.dtype)


def norm_mod(rows, x, g, shift, scale):
    D = x.shape[1]
    tm = min(rows.tm, 256)
    sub = rows.tm // tm
    vec = pl.BlockSpec((None, rows.R if rows.R == 1 else tm, D),
                       (lambda i: (i // (rows.tpg * sub), 0, 0)) if rows.R == 1 else (lambda i: (0, i, 0)))
    return pl.pallas_call(
        _norm_mod_kernel,
        grid=(rows.M // tm,),
        in_specs=[pl.BlockSpec((tm, D), lambda i: (i, 0)),
                  pl.BlockSpec((1, D), lambda i: (0, 0)), vec, vec],
        out_specs=pl.BlockSpec((tm, D), lambda i: (i, 0)),
        out_shape=jax.ShapeDtypeStruct((rows.M, D), BF16),
        compiler_params=_cparams(("parallel",)),
        name="norm_mod",
    )(x, g.reshape(1, D), shift, scale)


def _final_norm_kernel(x_ref, g_ref, o_ref):
    x = x_ref[...]
    o_ref[...] = x * lax.rsqrt(jnp.mean(x * x, axis=-1, keepdims=True) + NORM_EPS) * g_ref[...]


def final_norm(rows, x, g):
    D = x.shape[1]
    tm = min(rows.tm, 256)
    return pl.pallas_call(
        _final_norm_kernel,
        grid=(rows.M // tm,),
        in_specs=[pl.BlockSpec((tm, D), lambda i: (i, 0)), pl.BlockSpec((1, D), lambda i: (0, 0))],
        out_specs=pl.BlockSpec((tm, D), lambda i: (i, 0)),
        out_shape=jax.ShapeDtypeStruct((rows.M, D), F32),
        compiler_params=_cparams(("parallel",)),
        name="final_norm",
    )(x, g.reshape(1, D))


def _mm_kernel(a_ref, w_ref, o_ref):
    o_ref[...] = _dot(a_ref[...].astype(BF16), w_ref[...]).astype(o_ref.dtype)


def matmul(a, w, tm, tn, out_dtype=F32, name="matmul"):
    M, K = a.shape
    N = w.shape[1]
    return pl.pallas_call(
        _mm_kernel,
        grid=(M // tm, N // tn),
        in_specs=[pl.BlockSpec((tm, K), lambda i, j: (i, 0)), pl.BlockSpec((K, tn), lambda i, j: (0, j))],
        out_specs=pl.BlockSpec((tm, tn), lambda i, j: (i, j)),
        out_shape=jax.ShapeDtypeStruct((M, N), out_dtype),
        compiler_params=_cparams(("parallel", "arbitrary")),
        name=name,
    )(a, w)


def _swiglu_kernel(h_ref, wg_ref, wu_ref, o_ref):
    h = h_ref[...]
    g = _dot(h, wg_ref[...])
    u = _dot(h, wu_ref[...])
    o_ref[...] = (_silu(g) * u).astype(o_ref.dtype)


def mm_swiglu(rows, h, w_in):
    K = h.shape[1]
    tn = 512
    nj = D_FF // tn
    return pl.pallas_call(
        _swiglu_kernel,
        grid=(rows.nm, nj),
        in_specs=[pl.BlockSpec((rows.tm, K), lambda i, j: (i, 0)),
                  pl.BlockSpec((K, tn), lambda i, j: (0, j)),
                  pl.BlockSpec((K, tn), lambda i, j: (0, j + nj))],
        out_specs=pl.BlockSpec((rows.tm, tn), lambda i, j: (i, j)),
        out_shape=jax.ShapeDtypeStruct((rows.M, D_FF), BF16),
        compiler_params=_cparams(("parallel", "arbitrary")),
        name="mm_swiglu",
    )(h, w_in, w_in)


def _resid_kernel(a_ref, w_ref, x_ref, g_ref, o_ref, *, coef):
    o_ref[...] = x_ref[...] + (coef * g_ref[...]) * _dot(a_ref[...], w_ref[...])


def mm_resid(rows, a, w, x, gate, coef):
    K = a.shape[1]
    N = w.shape[1]
    tn = 512
    return pl.pallas_call(
        functools.partial(_resid_kernel, coef=coef),
        grid=(rows.nm, N // tn),
        in_specs=[pl.BlockSpec((rows.tm, K), lambda i, j: (i, 0)),
                  pl.BlockSpec((K, tn), lambda i, j: (0, j)),
                  pl.BlockSpec((rows.tm, tn), lambda i, j: (i, j)),
                  rows.vec_spec(tn)],
        out_specs=pl.BlockSpec((rows.tm, tn), lambda i, j: (i, j)),
        out_shape=jax.ShapeDtypeStruct((rows.M, N), F32),
        compiler_params=_cparams(("parallel", "arbitrary")),
        name="mm_resid",
    )(a, w, x, gate)


def _branch_kernel(om_ref, oa_ref, or_ref, w0_ref, w1_ref, w2_ref, z0_ref, z1_ref, z2_ref, b0_ref, b1_ref, b2_ref,
                   o_ref):
    acc = _sigmoid(z0_ref[...] + b0_ref[...]) * _dot(om_ref[...], w0_ref[...])
    acc += _sigmoid(z1_ref[...] + b1_ref[...]) * _dot(oa_ref[...], w1_ref[...])
    acc += _sigmoid(z2_ref[...] + b2_ref[...]) * _dot(or_ref[...], w2_ref[...])
    o_ref[...] = acc.astype(o_ref.dtype)


def mm_branch(rows, om, oa, orw, w_branch, z_all, b_merge):
    tn = 512
    D = D_MODEL
    a_spec = pl.BlockSpec((rows.tm, BRANCH_WIDTH), lambda i, j: (i, 0))
    w_specs = [pl.BlockSpec((None, BRANCH_WIDTH, tn), lambda i, j, n=n: (n, 0, j)) for n in range(3)]
    z_specs = [pl.BlockSpec((rows.tm, tn), lambda i, j, n=n: (i, (OFF_GATE + n * D) // tn + j)) for n in range(3)]
    b_specs = [pl.BlockSpec((1, tn), lambda i, j, n=n: (0, n * D // tn + j)) for n in range(3)]
    return pl.pallas_call(
        _branch_kernel,
        grid=(rows.nm, D // tn),
        in_specs=[a_spec, a_spec, a_spec] + w_specs + z_specs + b_specs,
        out_specs=pl.BlockSpec((rows.tm, tn), lambda i, j: (i, j)),
        out_shape=jax.ShapeDtypeStruct((rows.M, D), BF16),
        compiler_params=_cparams(("parallel", "arbitrary")),
        name="mm_branch",
    )(om, oa, orw, w_branch, w_branch, w_branch, z_all, z_all, z_all, b_merge, b_merge, b_merge)


def _mlstm_kernel(q_ref, k_ref, v_ref, o_ref, if_ref, bias_ref, hg_ref, C0_ref, n0_ref, m0_ref,
                  om_ref, C_ref, n_ref, m_ref, *, L):
    c = pl.program_id(1)

    @pl.when(c == 0)
    def _():
        C_ref[...] = C0_ref[...]
        n_ref[...] = n0_ref[...]
        m_ref[...] = m0_ref[...]

    H = ML_HEADS
    sc = ML_GATE_CAP * jnp.tanh((if_ref[...] + bias_ref[...]) * (1.0 / ML_GATE_CAP))
    lf = jnp.minimum(sc, 0.0) - jnp.log(1.0 + jnp.exp(-jnp.abs(sc)))
    row = lax.broadcasted_iota(jnp.int32, (L, L), 0)
    col = lax.broadcasted_iota(jnp.int32, (L, L), 1)
    causal = row >= col
    bcum = jnp.dot(causal.astype(F32), lf, precision=HIGHEST, preferred_element_type=F32)
    sel = (lax.broadcasted_iota(jnp.int32, (8, LANES), 0) == lax.broadcasted_iota(jnp.int32, (8, LANES), 1)).astype(F32)
    scT = _dot_nt(sel, sc, HIGHEST)
    bT = _dot_nt(sel, bcum, HIGHEST)
    for h in range(H):
        b_col = bcum[:, H + h:H + h + 1]
        b_row = bT[H + h:H + h + 1, :]
        ig_row = scT[h:h + 1, :]
        ig_col = sc[:, h:h + 1]
        m_prev = m_ref[0, h, :, 0:1]
        logD = jnp.where(causal, b_col - b_row + ig_row, -jnp.inf)
        g = b_col + m_prev
        m_t = jnp.maximum(g, jnp.max(logD, axis=1, keepdims=True))
        Dm = jnp.exp(logD - m_t)
        inter = jnp.exp(g - m_t)
        qf = q_ref[:, h * ML_DQK:(h + 1) * ML_DQK]
        q = qf.astype(BF16)
        k = k_ref[:, h * ML_DQK:(h + 1) * ML_DQK] * (ML_DQK ** -0.5)
        kb = k.astype(BF16)
        v = v_ref[:, h * ML_DV:(h + 1) * ML_DV]
        C = C_ref[0, h]
        n = n_ref[0, h]
        qk = _dot_nt(q, kb) * Dm
        num = _dot(qk.astype(BF16), v.astype(BF16)) + inter * _dot_nt(q, C.astype(BF16))
        qn = jnp.sum(qf * n, axis=1, keepdims=True)
        den = jnp.sum(qk, axis=1, keepdims=True) + inter * qn
        hh = num / jnp.maximum(jnp.abs(den), jnp.exp(-m_t))
        y = hh * lax.rsqrt(jnp.mean(hh * hh, axis=-1, keepdims=True) + NORM_EPS) * hg_ref[:, h * ML_DV:(h + 1) * ML_DV]
        og = _sigmoid(o_ref[:, h * ML_DV:(h + 1) * ML_DV])
        om_ref[:, h * ML_DV:(h + 1) * ML_DV] = (og * y).astype(om_ref.dtype)
        m_new = m_t[L - 1:L, :]
        b_last = b_col[L - 1:L, :]
        carry_w = jnp.exp(b_last + m_prev - m_new)
        w_s = jnp.exp(b_last - b_col + ig_col - m_new)
        C_ref[0, h] = carry_w * C + _dot_tn((w_s * v).astype(BF16), kb)
        n_ref[0, h] = carry_w * n + jnp.sum(w_s * kb.astype(F32), axis=0, keepdims=True)
        m_ref[0, h] = jnp.broadcast_to(m_new, (1, LANES))


def mlstm(B, T, z_all, bias_row, head_g, C0, n0, m0):
    L = CHUNK if T % CHUNK == 0 else T
    nc = T // L
    H = ML_HEADS
    zrow = lambda w, off: pl.BlockSpec((L, w), lambda b, c: (b * nc + c, off // w))
    st = lambda *shape: pl.BlockSpec((1,) + shape, lambda b, c: (b,) + (0,) * len(shape))
    om, C, n, m = pl.pallas_call(
        functools.partial(_mlstm_kernel, L=L),
        grid=(B, nc),
        in_specs=[zrow(H * ML_DQK, OFF_ML_Q), zrow(H * ML_DQK, OFF_ML_K), zrow(H * ML_DV, OFF_ML_V),
                  zrow(H * ML_DV, OFF_ML_O), zrow(LANES, OFF_IF),
                  pl.BlockSpec((1, LANES), lambda b, c: (0, 0)),
                  pl.BlockSpec((1, H * ML_DV), lambda b, c: (0, 0)),
                  st(H, ML_DV, ML_DQK), st(H, 1, ML_DQK), st(H, 1, LANES)],
        out_specs=[pl.BlockSpec((L, H * ML_DV), lambda b, c: (b * nc + c, 0)),
                   st(H, ML_DV, ML_DQK), st(H, 1, ML_DQK), st(H, 1, LANES)],
        out_shape=[jax.ShapeDtypeStruct((B * T, H * ML_DV), BF16),
                   jax.ShapeDtypeStruct((B, H, ML_DV, ML_DQK), F32),
                   jax.ShapeDtypeStruct((B, H, 1, ML_DQK), F32),
                   jax.ShapeDtypeStruct((B, H, 1, LANES), F32)],
        compiler_params=_cparams(("parallel", "arbitrary")),
        name="mlstm",
    )(z_all, z_all, z_all, z_all, z_all, bias_row, head_g.reshape(1, -1), C0, n0.reshape(B, H, 1, ML_DQK),
      jnp.broadcast_to(m0[:, :, None, None], (B, H, 1, LANES)))
    return om, C, n.reshape(B, H, ML_DQK), m[:, :, 0, 0]


def _rope_lanes(x, cos, sin):
    swapped = pltpu.roll(x, 32, 1) + pltpu.roll(x, 96, 1)
    return x * cos + swapped * sin


def _mla_pre_kernel(qr_ref, lat_ref, kr_ref, cos_ref, sin_ref, g_ref, qro_ref, lato_ref, kro_ref):
    cos = cos_ref[...]
    sin = sin_ref[...]
    for h in range(MLA_HEADS):
        sl = slice(h * LANES, (h + 1) * LANES)
        qro_ref[:, sl] = _rope_lanes(qr_ref[:, sl], cos, sin).astype(qro_ref.dtype)
    kro_ref[...] = _rope_lanes(kr_ref[...], cos, sin)
    lat = lat_ref[...]
    lato_ref[...] = lat * lax.rsqrt(jnp.mean(lat * lat, axis=-1, keepdims=True) + NORM_EPS) * g_ref[...]


def mla_pre(rows, z_all, cos_tab, sin_tab, kv_norm_g):
    tm = min(rows.tm, 256)
    M = rows.M
    zspec = lambda w, off: pl.BlockSpec((tm, w), lambda i: (i, off // w))
    W = MLA_HEADS * LANES
    return pl.pallas_call(
        _mla_pre_kernel,
        grid=(M // tm,),
        in_specs=[zspec(W, OFF_QR), zspec(MLA_LATENT, OFF_LAT), zspec(LANES, OFF_KR),
                  pl.BlockSpec((tm, LANES), lambda i: (i, 0)), pl.BlockSpec((tm, LANES), lambda i: (i, 0)),
                  pl.BlockSpec((1, MLA_LATENT), lambda i: (0, 0))],
        out_specs=[pl.BlockSpec((tm, W), lambda i: (i, 0)), pl.BlockSpec((tm, MLA_LATENT), lambda i: (i, 0)),
                   pl.BlockSpec((tm, LANES), lambda i: (i, 0))],
        out_shape=[jax.ShapeDtypeStruct((M, W), BF16), jax.ShapeDtypeStruct((M, MLA_LATENT), F32),
                   jax.ShapeDtypeStruct((M, LANES), F32)],
        compiler_params=_cparams(("parallel",)),
        name="mla_pre",
    )(z_all, z_all, z_all, cos_tab, sin_tab, kv_norm_g.reshape(1, -1))


def _attn_kernel(qn_ref, qr_ref, kn_ref, kr_ref, v_ref, o_ref, m_sc, l_sc, acc_sc, *, tq, tk, past, nk):
    qi = pl.program_id(2)
    kj = pl.program_id(3)

    @pl.when(kj == 0)
    def _():
        m_sc[...] = jnp.full_like(m_sc, -jnp.inf)
        l_sc[...] = jnp.zeros_like(l_sc)
        acc_sc[...] = jnp.zeros_like(acc_sc)

    last_q_chunk = (past + qi * tq + tq - 1) // CHUNK

    @pl.when((kj * tk) // CHUNK <= last_q_chunk)
    def _():
        q = jnp.concatenate([qn_ref[...].astype(BF16), qr_ref[...]], axis=1)
        k = jnp.concatenate([kn_ref[...].astype(BF16), kr_ref[...].astype(BF16)], axis=1)
        s = _dot_nt(q, k) * ((MLA_NOPE + MLA_ROPE) ** -0.5)
        q_chunk = (past + qi * tq + lax.broadcasted_iota(jnp.int32, (tq, tk), 0)) // CHUNK
        k_chunk = (kj * tk + lax.broadcasted_iota(jnp.int32, (tq, tk), 1)) // CHUNK
        s = jnp.where(k_chunk <= q_chunk, s, -jnp.inf)
        m_prev = m_sc[...]
        m_new = jnp.maximum(m_prev, jnp.max(s, axis=1, keepdims=True))
        alpha = jnp.exp(m_prev - m_new)
        p = jnp.exp(s - m_new)
        l_sc[...] = alpha * l_sc[...] + jnp.sum(p, axis=1, keepdims=True)
        acc_sc[...] = alpha * acc_sc[...] + _dot(p.astype(BF16), v_ref[...].astype(BF16))
        m_sc[...] = m_new

    @pl.when(kj == nk - 1)
    def _():
        o_ref[...] = (acc_sc[...] / l_sc[...]).astype(o_ref.dtype)


def mla_attention(B, T, Sk, past, z_all, qr, kv, kr_all):
    H = MLA_HEADS
    tq = min(T, 512)
    tk = 512 if Sk % 512 == 0 else Sk
    nq, nk = T // tq, Sk // tk
    z3 = z_all.reshape(B, T, NZ)
    qr3 = qr.reshape(B, T, H * LANES)
    kv3 = kv.reshape(B, Sk, H * 2 * LANES)
    kr3 = kr_all.reshape(B, Sk, LANES)

    def kidx(qi, kj):
        last = ((past + qi * tq + tq - 1) // CHUNK * CHUNK + CHUNK - 1) // tk
        return jnp.minimum(kj, jnp.minimum(last, nk - 1))

    out = pl.pallas_call(
        functools.partial(_attn_kernel, tq=tq, tk=tk, past=past, nk=nk),
        grid=(B, H, nq, nk),
        in_specs=[pl.BlockSpec((None, tq, LANES), lambda b, h, qi, kj: (b, qi, OFF_QN // LANES + h)),
                  pl.BlockSpec((None, tq, LANES), lambda b, h, qi, kj: (b, qi, h)),
                  pl.BlockSpec((None, tk, LANES), lambda b, h, qi, kj: (b, kidx(qi, kj), 2 * h)),
                  pl.BlockSpec((None, tk, LANES), lambda b, h, qi, kj: (b, kidx(qi, kj), 0)),
                  pl.BlockSpec((None, tk, LANES), lambda b, h, qi, kj: (b, kidx(qi, kj), 2 * h + 1))],
        out_specs=pl.BlockSpec((None, tq, LANES), lambda b, h, qi, kj: (b, qi, h)),
        out_shape=jax.ShapeDtypeStruct((B, T, H * MLA_V), BF16),
        scratch_shapes=[pltpu.VMEM((tq, 1), F32), pltpu.VMEM((tq, 1), F32), pltpu.VMEM((tq, MLA_V), F32)],
        compiler_params=_cparams(("parallel", "parallel", "parallel", "arbitrary")),
        name="mla_attention",
    )(z3, qr3, kv3, kr3, kv3)
    return out.reshape(B * T, H * MLA_V)


def _pair_sum_matrix():
    r = lax.broadcasted_iota(jnp.int32, (LANES, LANES), 0) // RW_HEAD
    c = lax.broadcasted_iota(jnp.int32, (LANES, LANES), 1) // RW_HEAD
    return (r == c).astype(F32)


def _head_sum(x, bd):
    parts = [jnp.dot(x[:, p * LANES:(p + 1) * LANES], bd, precision=HIGHEST, preferred_element_type=F32)
             for p in range(RW_WIDTH // LANES)]
    return jnp.concatenate(parts, axis=1)


def _rwkv_pre_kernel(r_ref, k_ref, v_ref, lo_ref, pr_ref, pk_ref, pv_ref, plo_ref, mu_ref, mulo_ref,
                     w0_ref, w2_ref, a0_ref, a2_ref, g2_ref, kk_ref, ka_ref, rk_ref,
                     ro_ref, wo_ref, ko_ref, vo_ref, ao_ref, bo_ref, go_ref, bonus_ref):
    W = RW_WIDTH
    mix = lambda z, p, mu: z + (p - z) * mu
    r = mix(r_ref[...], pr_ref[...], mu_ref[:, 0:W])
    k = mix(k_ref[...], pk_ref[...], mu_ref[:, W:2 * W])
    v = mix(v_ref[...], pv_ref[...], mu_ref[:, 2 * W:3 * W])
    lo = mix(lo_ref[...], plo_ref[...], mulo_ref[...])
    wd, ad, gd = lo[:, 0:64], lo[:, 64:128], lo[:, 128:256]
    hdot = lambda x, w: jnp.dot(x, w, precision=HIGHEST, preferred_element_type=F32)
    wpre = w0_ref[...] + hdot(jnp.tanh(wd), w2_ref[...])
    w_raw = jnp.minimum(wpre, 0.0) - jnp.log(1.0 + jnp.exp(-jnp.abs(wpre))) - 0.5
    decay = jnp.exp(-jnp.exp(w_raw))
    a = _sigmoid(a0_ref[...] + hdot(ad, a2_ref[...]))
    g = hdot(_sigmoid(gd), g2_ref[...])
    bd = _pair_sum_matrix()
    kk = k * kk_ref[...]
    kk = kk / jnp.maximum(jnp.sqrt(_head_sum(kk * kk, bd)), 1e-12)
    k = k * (1.0 + (a - 1.0) * ka_ref[...])
    ro_ref[...] = r
    wo_ref[...] = decay
    ko_ref[...] = k
    vo_ref[...] = v
    ao_ref[...] = -kk
    bo_ref[...] = kk * a
    go_ref[...] = g
    bonus_ref[...] = _head_sum(r * k * rk_ref[...], bd) * v


def rwkv_pre(rows, z_all, zprev, mu, w0, w2, a0, a2, g2, k_k, k_a, r_k):
    M = rows.M
    tm = min(rows.tm, 256)
    W = RW_WIDTH
    zs = lambda w, off: pl.BlockSpec((tm, w), lambda i: (i, off // w))
    full = lambda a: pl.BlockSpec(a.shape, lambda i: (0,) * a.ndim)
    row = lambda a: a.reshape(1, -1)
    params = [row(mu[:3 * W]), row(mu[3 * W:]), row(w0), w2, row(a0), a2, g2, row(k_k), row(k_a), row(r_k)]
    outs = pl.pallas_call(
        _rwkv_pre_kernel,
        grid=(M // tm,),
        in_specs=[zs(W, OFF_RW_R), zs(W, OFF_RW_K), zs(W, OFF_RW_V), zs(RW_LORA, OFF_RW_LORA),
                  zs(W, 0), zs(W, W), zs(W, 2 * W), zs(RW_LORA, 3 * W)] + [full(p) for p in params],
        out_specs=[pl.BlockSpec((tm, W), lambda i: (i, 0))] * 8,
        out_shape=[jax.ShapeDtypeStruct((M, W), F32)] * 8,
        compiler_params=_cparams(("parallel",)),
        name="rwkv_pre",
    )(z_all, z_all, z_all, z_all, zprev, zprev, zprev, zprev, *params)
    return outs


def _rwkv_scan_kernel(r_ref, w_ref, k_ref, v_ref, a_ref, b_ref, S0_ref, y_ref, S_ref, *, Tb, NP):
    tb = pl.program_id(1)

    @pl.when(tb == 0)
    def _():
        S_ref[...] = S0_ref[...]

    bd = _pair_sum_matrix()
    diag = (lax.broadcasted_iota(jnp.int32, (RW_HEAD, LANES), 1) % RW_HEAD
            == lax.broadcasted_iota(jnp.int32, (RW_HEAD, LANES), 0)).astype(F32)
    hsum = lambda x: jnp.dot(x, bd, precision=HIGHEST, preferred_element_type=F32)

    def step(t, carry):
        rt, wt, kt = r_ref[pl.ds(t, 1), :], w_ref[pl.ds(t, 1), :], k_ref[pl.ds(t, 1), :]
        vt, at, bt = v_ref[pl.ds(t, 1), :], a_ref[pl.ds(t, 1), :], b_ref[pl.ds(t, 1), :]
        ys = []
        for p in range(NP):
            sl = slice(p * LANES, (p + 1) * LANES)
            S = S_ref[0, p]
            sa = hsum(S * at[:, sl])
            vcol = hsum(diag * vt[:, sl])
            S = S * wt[:, sl] + sa * bt[:, sl] + vcol * kt[:, sl]
            S_ref[0, p] = S
            yb = hsum(S * rt[:, sl])
            ys.append(jnp.sum(yb * diag, axis=0, keepdims=True))
        y_ref[pl.ds(t, 1), :] = jnp.concatenate(ys, axis=1)
        return carry

    lax.fori_loop(0, Tb, step, 0)


def rwkv_scan(B, T, r, w, k, v, a, b, S0):
    NP = RW_HEADS // 2
    Tb = min(T, 128)
    nT = T // Tb
    Sp = S0.reshape(B, NP, 2, RW_HEAD, RW_HEAD).transpose(0, 1, 3, 2, 4).reshape(B, NP, RW_HEAD, LANES)
    xs = pl.BlockSpec((Tb, RW_WIDTH), lambda bi, tb: (bi * nT + tb, 0))
    ss = pl.BlockSpec((1, NP, RW_HEAD, LANES), lambda bi, tb: (bi, 0, 0, 0))
    y, S = pl.pallas_call(
        functools.partial(_rwkv_scan_kernel, Tb=Tb, NP=NP),
        grid=(B, nT),
        in_specs=[xs] * 6 + [ss],
        out_specs=[xs, ss],
        out_shape=[jax.ShapeDtypeStruct((B * T, RW_WIDTH), F32),
                   jax.ShapeDtypeStruct((B, NP, RW_HEAD, LANES), F32)],
        compiler_params=_cparams(("parallel", "arbitrary")),
        name="rwkv_scan",
    )(r, w, k, v, a, b, Sp)
    S = S.reshape(B, NP, RW_HEAD, 2, RW_HEAD).transpose(0, 1, 3, 2, 4).reshape(B, RW_HEADS, RW_HEAD, RW_HEAD)
    return y, S


def _rwkv_post_kernel(y_ref, g_ref, bonus_ref, lg_ref, lb_ref, o_ref):
    bd = _pair_sum_matrix()
    y = y_ref[...]
    mean = _head_sum(y, bd) * (1.0 / RW_HEAD)
    d = y - mean
    var = _head_sum(d * d, bd) * (1.0 / RW_HEAD)
    yn = d * lax.rsqrt(var + RW_LN_EPS) * lg_ref[...] + lb_ref[...]
    o_ref[...] = ((yn + bonus_ref[...]) * g_ref[...]).astype(o_ref.dtype)


def rwkv_post(rows, y, g, bonus, lnx_g, lnx_b):
    M = rows.M
    tm = min(rows.tm, 256)
    W = RW_WIDTH
    xs = pl.BlockSpec((tm, W), lambda i: (i, 0))
    ps = pl.BlockSpec((1, W), lambda i: (0, 0))
    return pl.pallas_call(
        _rwkv_post_kernel,
        grid=(M // tm,),
        in_specs=[xs, xs, xs, ps, ps],
        out_specs=xs,
        out_shape=jax.ShapeDtypeStruct((M, W), BF16),
        compiler_params=_cparams(("parallel",)),
        name="rwkv_post",
    )(y, g, bonus, lnx_g.reshape(1, W), lnx_b.reshape(1, W))


def _prep_w_in(w):
    D = w.shape[0]
    o = 0
    take = lambda n: (w[:, o:o + n], o + n)
    ml_q, o = take(512)
    ml_k, o = take(512)
    ml_v, o = take(1024)
    ml_o, o = take(1024)
    ml_i, o = take(4)
    ml_f, o = take(4)
    q, o = take(MLA_HEADS * (MLA_NOPE + MLA_ROPE))
    lat, o = take(MLA_LATENT)
    kr, o = take(MLA_ROPE)
    rw_r, o = take(1024)
    rw_k, o = take(1024)
    rw_v, o = take(1024)
    rw_lo, o = take(RW_LORA)
    gate, o = take(3 * D_MODEL)
    assert o == w.shape[1]
    q = q.reshape(D, MLA_HEADS, MLA_NOPE + MLA_ROPE)
    qn = q[:, :, :MLA_NOPE].reshape(D, MLA_HEADS * MLA_NOPE)
    qr = jnp.pad(q[:, :, MLA_NOPE:], ((0, 0), (0, 0), (0, LANES - MLA_ROPE))).reshape(D, MLA_HEADS * LANES)
    kr = jnp.pad(kr, ((0, 0), (0, LANES - MLA_ROPE)))
    gif = jnp.pad(jnp.concatenate([ml_i, ml_f], axis=1), ((0, 0), (0, LANES - 2 * ML_HEADS)))
    out = jnp.concatenate([ml_q, ml_k, ml_v, ml_o, qn, qr, rw_r, rw_k, rw_v, gate, lat, rw_lo, kr, gif], axis=1)
    assert out.shape[1] == NZ
    return out.astype(BF16)


def _rope_tables(T, past, reps):
    half = MLA_ROPE // 2
    inv = ROPE_BASE ** (-jnp.arange(half, dtype=F32) / half)
    ang = (past + jnp.arange(T)).astype(F32)[:, None] * inv[None, :]
    cos, sin = jnp.cos(ang), jnp.sin(ang)
    zeros = jnp.zeros((T, LANES - MLA_ROPE), F32)
    cos_t = jnp.concatenate([cos, cos, zeros], axis=1)
    sin_t = jnp.concatenate([-sin, sin, zeros], axis=1)
    return jnp.tile(cos_t, (reps, 1)), jnp.tile(sin_t, (reps, 1))


def _layer(rows, x, mod, l, W, P, lat_past, kr_past, C0, n0, m0, S0, shift0, tabs):
    B, T, M = rows.B, rows.T, rows.M
    D = D_MODEL
    m9 = [rows.expand(mod[:, i * D:(i + 1) * D]) for i in range(N_ADA)]
    sh1, sc1, g1, sh2, sc2, g2, sh3, sc3, g3 = m9
    h = norm_mod(rows, x, P['norm_g'][l, 0], sh1, sc1)
    x = mm_resid(rows, mm_swiglu(rows, h, W['ffn_in'][l][0]), W['ffn_out'][l][0], x, g1, 0.5)
    h = norm_mod(rows, x, P['norm_g'][l, 1], sh2, sc2)
    z_all = matmul(h, W['w_in'][l], rows.tm, 512, name="mm_in")
    bias_row = jnp.pad(jnp.concatenate([P['mlstm_i_bias'][l], P['mlstm_f_bias'][l]]), (0, LANES - 2 * ML_HEADS))[None]
    om, C, n, m = mlstm(B, T, z_all, bias_row, P['mlstm_head_g'][l], C0, n0, m0)
    past = lat_past.shape[1]
    qr, lat, kr = mla_pre(rows, z_all, tabs[0], tabs[1], P['mla_kv_norm_g'][l])
    Sk = past + T
    if past:
        lat_all = jnp.concatenate([lat_past, lat.reshape(B, T, -1)], axis=1).reshape(B * Sk, -1)
        kr_all = jnp.concatenate([jnp.pad(kr_past, ((0, 0), (0, 0), (0, LANES - MLA_ROPE))), kr.reshape(B, T, -1)],
                                 axis=1).reshape(B * Sk, LANES)
    else:
        lat_all, kr_all = lat, kr
    kv_tm = 512 if (B * Sk) % 512 == 0 else 128
    kv = matmul(lat_all, W['kv_b'][l], kv_tm, 512, name="mm_kv")
    oa = mla_attention(B, T, Sk, past, z_all, qr, kv, kr_all)
    z3 = z_all.reshape(B, T, NZ)
    zr = jnp.concatenate([z3[:, :, OFF_RW_R:OFF_RW_R + 3 * RW_WIDTH], z3[:, :, OFF_RW_LORA:OFF_RW_LORA + RW_LORA]],
                         axis=2)
    zprev = jnp.concatenate([shift0[:, None, :], zr[:, :-1]], axis=1).reshape(M, RW_COLS)
    r_, w_, k_, v_, a_, b_, g_, bonus = rwkv_pre(rows, z_all, zprev, P['rwkv_mu'][l], P['rwkv_w0'][l], P['rwkv_w2'][l],
                                                 P['rwkv_a0'][l], P['rwkv_a2'][l], P['rwkv_g2'][l], P['rwkv_k_k'][l],
                                                 P['rwkv_k_a'][l], P['rwkv_r_k'][l].reshape(-1))
    y, S = rwkv_scan(B, T, r_, w_, k_, v_, a_, b_, S0)
    orw = rwkv_post(rows, y, g_, bonus, P['rwkv_lnx_g'][l], P['rwkv_lnx_b'][l])
    shift = zr[:, -1]
    mixed = mm_branch(rows, om, oa, orw, W['branch'][l], z_all, P['b_merge'][l][None])
    x = mm_resid(rows, mixed, W['w_out'][l], x, g2, 1.0)
    h = norm_mod(rows, x, P['norm_g'][l, 2], sh3, sc3)
    x = mm_resid(rows, mm_swiglu(rows, h, W['ffn_in'][l][1]), W['ffn_out'][l][1], x, g3, 0.5)
    new = (lat.reshape(B, T, MLA_LATENT), kr.reshape(B, T, LANES)[:, :, :MLA_ROPE], C, n, m, S, shift)
    return x, new


def _trunk(rows, x, mods, W, P, lat_cache, kr_cache, C_st, n_st, m_st, S_st, sh_st):
    depth = len(mods)
    past = lat_cache.shape[2]
    tabs = _rope_tables(rows.T, past, rows.B)
    x = x.reshape(rows.M, D_MODEL)
    new = []
    for l in range(depth):
        x, st = _layer(rows, x, mods[l], l, W, P, lat_cache[l], kr_cache[l], C_st[l], n_st[l], m_st[l], S_st[l],
                       sh_st[l], tabs)
        new.append(st)
    y = final_norm(rows, x, P['final_norm_g']).reshape(rows.B, rows.T, D_MODEL)
    return y, tuple(jnp.stack(s, axis=0) for s in zip(*new))


def kernel(x_prompt, x_sample, c_prompt, c_sample, cache_mla_latent, cache_mla_krope, state_mlstm_C, state_mlstm_n,
           state_mlstm_m, state_rwkv_S, state_rwkv_shift, ada_w, ada_b, norm_g, ffn_w_in, ffn_w_out, w_in,
           mlstm_i_bias, mlstm_f_bias, mlstm_head_g, mla_kv_norm_g, mla_w_kv_b, rwkv_mu, rwkv_w0, rwkv_w2, rwkv_a0,
           rwkv_a2, rwkv_g2, rwkv_k_k, rwkv_k_a, rwkv_r_k, rwkv_lnx_g, rwkv_lnx_b, w_branch, b_merge, w_out,
           final_norm_g):
    P = dict(norm_g=norm_g, mlstm_i_bias=mlstm_i_bias, mlstm_f_bias=mlstm_f_bias, mlstm_head_g=mlstm_head_g,
             mla_kv_norm_g=mla_kv_norm_g, rwkv_mu=rwkv_mu, rwkv_w0=rwkv_w0, rwkv_w2=rwkv_w2, rwkv_a0=rwkv_a0,
             rwkv_a2=rwkv_a2, rwkv_g2=rwkv_g2, rwkv_k_k=rwkv_k_k, rwkv_k_a=rwkv_k_a, rwkv_r_k=rwkv_r_k,
             rwkv_lnx_g=rwkv_lnx_g, rwkv_lnx_b=rwkv_lnx_b, b_merge=b_merge, final_norm_g=final_norm_g)
    depth = w_in.shape[0]
    W = dict(ffn_in=[[ffn_w_in[l, i].astype(BF16) for i in range(2)] for l in range(depth)],
             ffn_out=[[ffn_w_out[l, i].astype(BF16) for i in range(2)] for l in range(depth)],
             w_in=[_prep_w_in(w_in[l]) for l in range(depth)],
             kv_b=[mla_w_kv_b[l].astype(BF16) for l in range(depth)],
             branch=[w_branch[l].astype(BF16) for l in range(depth)],
             w_out=[w_out[l].astype(BF16) for l in range(depth)])
    Bp, Tp, _ = x_prompt.shape
    Bs, Ts, _ = x_sample.shape
    n_c = Bp + Bs
    c_all = jnp.pad(jnp.concatenate([c_prompt, c_sample], axis=0), ((0, (-n_c) % 8), (0, 0)))
    mods = [ada_mod(c_all, ada_w[l], ada_b[l]) for l in range(depth)]
    rows_p = Rows(Bp, Tp, 512)
    rows_s = Rows(Bs, Ts, 512)
    zeros = lambda *s: jnp.zeros((depth, Bp) + s, F32)
    y_p, st_p = _trunk(rows_p, x_prompt, [m[:Bp] for m in mods], W, P,
                       zeros(0, MLA_LATENT), zeros(0, MLA_ROPE), zeros(ML_HEADS, ML_DV, ML_DQK), zeros(ML_HEADS, ML_DQK),
                       zeros(ML_HEADS), zeros(RW_HEADS, RW_HEAD, RW_HEAD), zeros(RW_COLS))
    y_s, st_s = _trunk(rows_s, x_sample, [m[Bp:n_c] for m in mods], W, P,
                       cache_mla_latent, cache_mla_krope, state_mlstm_C, state_mlstm_n, state_mlstm_m, state_rwkv_S,
                       state_rwkv_shift)
    return (y_p, y_s) + st_p + st_s
```

```python
import functools

import jax
import jax.numpy as jnp
import numpy as np
from jax import lax
from jax.experimental import pallas as pl
from jax.experimental.pallas import tpu as pltpu

F32 = jnp.float32
BF16 = jnp.bfloat16
HIGHEST = lax.Precision.HIGHEST

D_MODEL = 2048
CHUNK = 64
NORM_EPS = 1e-6
D_FF = 5632
N_ADA = 9
ML_HEADS, ML_DQK, ML_DV, ML_GATE_CAP = 4, 128, 256, 15.0
MLA_HEADS, MLA_NOPE, MLA_ROPE, MLA_V, MLA_LATENT = 8, 128, 64, 128, 512
MLA_SCALE = (MLA_NOPE + MLA_ROPE) ** -0.5
ROPE_BASE = 10000.0
RW_HEADS, RW_HEAD = 16, 64
RW_WIDTH = RW_HEADS * RW_HEAD
RW_LN_EPS = 64e-5
RW_LORA = 256
RW_COLS = 3 * RW_WIDTH + RW_LORA
BRANCH_WIDTH = 1024

LANES = 128
VMEM_LIMIT = 56 * 1024 * 1024
TM_MAX = 1024
TN = 512

OFF_ML_Q, OFF_ML_K, OFF_ML_V, OFF_ML_O = 0, 512, 1024, 2048
OFF_QN, OFF_QR = 3072, 4096
OFF_RW_R, OFF_RW_K, OFF_RW_V = 5120, 6144, 7168
OFF_GATE = 8192
OFF_LAT = 14336
OFF_RW_LORA = 14848
OFF_KR = 15104
OFF_IF = 15232
NZ = 15360


def _cparams(sem):
    return pltpu.CompilerParams(dimension_semantics=sem, vmem_limit_bytes=VMEM_LIMIT)


def _sigmoid(x):
    return 1.0 / (1.0 + jnp.exp(-x))


def _silu(x):
    return x * _sigmoid(x)


def _dot(a, b):
    return jnp.dot(a, b, preferred_element_type=F32)


def _dot_nt(a, b, precision=None):
    return lax.dot_general(a, b, (((1,), (1,)), ((), ())), precision=precision, preferred_element_type=F32)


def _dot_tn(a, b, precision=None):
    return lax.dot_general(a, b, (((0,), (0,)), ((), ())), precision=precision, preferred_element_type=F32)


class Rows:
    def __init__(self, B, T, tm_max):
        self.B, self.T, self.M = B, T, B * T
        if T >= 256:
            self.tm = min(tm_max, T)
            assert T % self.tm == 0
            self.tpg = T // self.tm
            self.R = 1
        else:
            self.tm = self.M
            self.tpg = 1
            self.R = self.M
        self.nm = self.M // self.tm

    def expand(self, v):
        if self.R == 1:
            return v[:, None, :]
        return jnp.repeat(v, self.T, axis=0)[None]

    def vec_spec(self, tn, col=lambda j: j):
        return pl.BlockSpec((None, self.R, tn), lambda i, j: (i // self.tpg, 0, col(j)))


def _ada_kernel(c_ref, w_ref, b_ref, o_ref):
    c = c_ref[...]
    o_ref[...] = _dot(_silu(c).astype(BF16), w_ref[...].astype(BF16)) + b_ref[...]


def ada_mod(c, w, b):
    R, D = c.shape
    N = w.shape[1]
    tn = 1024
    return pl.pallas_call(
        _ada_kernel,
        grid=(N // tn,),
        in_specs=[pl.BlockSpec((R, D), lambda j: (0, 0)),
                  pl.BlockSpec((D, tn), lambda j: (0, j)),
                  pl.BlockSpec((1, tn), lambda j: (0, j))],
        out_specs=pl.BlockSpec((R, tn), lambda j: (0, j)),
        out_shape=jax.ShapeDtypeStruct((R, N), F32),
        compiler_params=_cparams(("arbitrary",)),
        name="ada_mod",
    )(c, w, b.reshape(1, N))


def _norm_mod_kernel(x_ref, g_ref, sh_ref, sc_ref, o_ref):
    x = x_ref[...]
    y = x * lax.rsqrt(jnp.mean(x * x, axis=-1, keepdims=True) + NORM_EPS) * g_ref[...]
    o_ref[...] = (y * (1.0 + sc_ref[...]) + sh_ref[...]).astype(o_ref.dtype)


def norm_mod(rows, x, g, shift, scale):
    D = x.shape[1]
    tm = min(rows.tm, 256)
    sub = rows.tm // tm
    vec = pl.BlockSpec((None, rows.R if rows.R == 1 else tm, D),
                       (lambda i: (i // (rows.tpg * sub), 0, 0)) if rows.R == 1 else (lambda i: (0, i, 0)))
    return pl.pallas_call(
        _norm_mod_kernel,
        grid=(rows.M // tm,),
        in_specs=[pl.BlockSpec((tm, D), lambda i: (i, 0)),
                  pl.BlockSpec((1, D), lambda i: (0, 0)), vec, vec],
        out_specs=pl.BlockSpec((tm, D), lambda i: (i, 0)),
        out_shape=jax.ShapeDtypeStruct((rows.M, D), BF16),
        compiler_params=_cparams(("parallel",)),
        name="norm_mod",
    )(x, g.reshape(1, D), shift, scale)


def _final_norm_kernel(x_ref, g_ref, o_ref):
    x = x_ref[...]
    o_ref[...] = x * lax.rsqrt(jnp.mean(x * x, axis=-1, keepdims=True) + NORM_EPS) * g_ref[...]


def final_norm(rows, x, g):
    D = x.shape[1]
    tm = min(rows.tm, 256)
    return pl.pallas_call(
        _final_norm_kernel,
        grid=(rows.M // tm,),
        in_specs=[pl.BlockSpec((tm, D), lambda i: (i, 0)), pl.BlockSpec((1, D), lambda i: (0, 0))],
        out_specs=pl.BlockSpec((tm, D), lambda i: (i, 0)),
        out_shape=jax.ShapeDtypeStruct((rows.M, D), F32),
        compiler_params=_cparams(("parallel",)),
        name="final_norm",
    )(x, g.reshape(1, D))


def _mm_kernel(a_ref, w_ref, o_ref):
    o_ref[...] = _dot(a_ref[...].astype(BF16), w_ref[...]).astype(o_ref.dtype)


def matmul(a, w, tm, out_dtype=F32, name="matmul"):
    M, K = a.shape
    nj, _, tn = w.shape
    N = nj * tn
    return pl.pallas_call(
        _mm_kernel,
        grid=(M // tm, nj),
        in_specs=[pl.BlockSpec((tm, K), lambda i, j: (i, 0)), pl.BlockSpec((None, K, tn), lambda i, j: (j, 0, 0))],
        out_specs=pl.BlockSpec((tm, tn), lambda i, j: (i, j)),
        out_shape=jax.ShapeDtypeStruct((M, N), out_dtype),
        compiler_params=_cparams(("parallel", "arbitrary")),
        name=name,
    )(a, w)


def _swiglu_kernel(h_ref, wg_ref, wu_ref, o_ref):
    h = h_ref[...]
    g = _dot(h, wg_ref[...])
    u = _dot(h, wu_ref[...])
    o_ref[...] = (_silu(g) * u).astype(o_ref.dtype)


def mm_swiglu(rows, h, w_in):
    K = h.shape[1]
    tn = w_in.shape[2]
    nj = D_FF // tn
    return pl.pallas_call(
        _swiglu_kernel,
        grid=(rows.nm, nj),
        in_specs=[pl.BlockSpec((rows.tm, K), lambda i, j: (i, 0)),
                  pl.BlockSpec((None, K, tn), lambda i, j: (j, 0, 0)),
                  pl.BlockSpec((None, K, tn), lambda i, j: (j + nj, 0, 0))],
        out_specs=pl.BlockSpec((rows.tm, tn), lambda i, j: (i, j)),
        out_shape=jax.ShapeDtypeStruct((rows.M, D_FF), BF16),
        compiler_params=_cparams(("parallel", "arbitrary")),
        name="mm_swiglu",
    )(h, w_in, w_in)


def _resid_kernel(a_ref, w_ref, x_ref, g_ref, o_ref, *, coef):
    o_ref[...] = x_ref[...] + (coef * g_ref[...]) * _dot(a_ref[...], w_ref[...])


def mm_resid(rows, a, w, x, gate, coef):
    K = a.shape[1]
    nj, _, tn = w.shape
    N = nj * tn
    return pl.pallas_call(
        functools.partial(_resid_kernel, coef=coef),
        grid=(rows.nm, nj),
        in_specs=[pl.BlockSpec((rows.tm, K), lambda i, j: (i, 0)),
                  pl.BlockSpec((None, K, tn), lambda i, j: (j, 0, 0)),
                  pl.BlockSpec((rows.tm, tn), lambda i, j: (i, j)),
                  rows.vec_spec(tn)],
        out_specs=pl.BlockSpec((rows.tm, tn), lambda i, j: (i, j)),
        out_shape=jax.ShapeDtypeStruct((rows.M, N), F32),
        compiler_params=_cparams(("parallel", "arbitrary")),
        name="mm_resid",
    )(a, w, x, gate)


def _branch_kernel(om_ref, oa_ref, or_ref, w0_ref, w1_ref, w2_ref, z0_ref, z1_ref, z2_ref, b0_ref, b1_ref, b2_ref,
                   o_ref):
    acc = _sigmoid(z0_ref[...] + b0_ref[...]) * _dot(om_ref[...], w0_ref[...])
    acc += _sigmoid(z1_ref[...] + b1_ref[...]) * _dot(oa_ref[...], w1_ref[...])
    acc += _sigmoid(z2_ref[...] + b2_ref[...]) * _dot(or_ref[...], w2_ref[...])
    o_ref[...] = acc.astype(o_ref.dtype)


def mm_branch(rows, om, oa, orw, w_branch, z_all, b_merge):
    tn = w_branch.shape[2]
    D = D_MODEL
    nj = D // tn
    a_spec = pl.BlockSpec((rows.tm, BRANCH_WIDTH), lambda i, j: (i, 0))
    w_specs = [pl.BlockSpec((None, BRANCH_WIDTH, tn), lambda i, j, n=n: (n * nj + j, 0, 0)) for n in range(3)]
    z_specs = [pl.BlockSpec((rows.tm, tn), lambda i, j, n=n: (i, (OFF_GATE + n * D) // tn + j)) for n in range(3)]
    b_specs = [pl.BlockSpec((1, tn), lambda i, j, n=n: (0, n * D // tn + j)) for n in range(3)]
    return pl.pallas_call(
        _branch_kernel,
        grid=(rows.nm, nj),
        in_specs=[a_spec, a_spec, a_spec] + w_specs + z_specs + b_specs,
        out_specs=pl.BlockSpec((rows.tm, tn), lambda i, j: (i, j)),
        out_shape=jax.ShapeDtypeStruct((rows.M, D), BF16),
        compiler_params=_cparams(("parallel", "arbitrary")),
        name="mm_branch",
    )(om, oa, orw, w_branch, w_branch, w_branch, z_all, z_all, z_all, b_merge, b_merge, b_merge)


def _mlstm_kernel(q_ref, k_ref, v_ref, o_ref, if_ref, bias_ref, hg_ref, C0_ref, n0_ref, m0_ref,
                  om_ref, C_ref, n_ref, m_ref, *, L):
    c = pl.program_id(1)

    @pl.when(c == 0)
    def _():
        C_ref[...] = C0_ref[...]
        n_ref[...] = n0_ref[...]
        m_ref[...] = m0_ref[...]

    H = ML_HEADS
    sc = ML_GATE_CAP * jnp.tanh((if_ref[...] + bias_ref[...]) * (1.0 / ML_GATE_CAP))
    lf = jnp.minimum(sc, 0.0) - jnp.log(1.0 + jnp.exp(-jnp.abs(sc)))
    row = lax.broadcasted_iota(jnp.int32, (L, L), 0)
    col = lax.broadcasted_iota(jnp.int32, (L, L), 1)
    causal = row >= col
    bcum = jnp.dot(causal.astype(F32), lf, precision=HIGHEST, preferred_element_type=F32)
    sel = (lax.broadcasted_iota(jnp.int32, (8, LANES), 0) == lax.broadcasted_iota(jnp.int32, (8, LANES), 1)).astype(F32)
    scT = _dot_nt(sel, sc, HIGHEST)
    bT = _dot_nt(sel, bcum, HIGHEST)
    for h in range(H):
        b_col = bcum[:, H + h:H + h + 1]
        b_row = bT[H + h:H + h + 1, :]
        ig_row = scT[h:h + 1, :]
        ig_col = sc[:, h:h + 1]
        m_prev = m_ref[0, h, :, 0:1]
        logD = jnp.where(causal, b_col - b_row + ig_row, -jnp.inf)
        g = b_col + m_prev
        m_t = jnp.maximum(g, jnp.max(logD, axis=1, keepdims=True))
        Dm = jnp.exp(logD - m_t)
        inter = jnp.exp(g - m_t)
        qf = q_ref[:, h * ML_DQK:(h + 1) * ML_DQK]
        q = qf.astype(BF16)
        k = k_ref[:, h * ML_DQK:(h + 1) * ML_DQK] * (ML_DQK ** -0.5)
        kb = k.astype(BF16)
        v = v_ref[:, h * ML_DV:(h + 1) * ML_DV]
        C = C_ref[0, h]
        n = n_ref[0, h]
        qk = _dot_nt(q, kb) * Dm
        num = _dot(qk.astype(BF16), v.astype(BF16)) + inter * _dot_nt(q, C.astype(BF16))
        qn = jnp.sum(qf * n, axis=1, keepdims=True)
        den = jnp.sum(qk, axis=1, keepdims=True) + inter * qn
        hh = num / jnp.maximum(jnp.abs(den), jnp.exp(-m_t))
        y = hh * lax.rsqrt(jnp.mean(hh * hh, axis=-1, keepdims=True) + NORM_EPS) * hg_ref[:, h * ML_DV:(h + 1) * ML_DV]
        og = _sigmoid(o_ref[:, h * ML_DV:(h + 1) * ML_DV])
        om_ref[:, h * ML_DV:(h + 1) * ML_DV] = (og * y).astype(om_ref.dtype)
        m_new = m_t[L - 1:L, :]
        b_last = b_col[L - 1:L, :]
        carry_w = jnp.exp(b_last + m_prev - m_new)
        w_s = jnp.exp(b_last - b_col + ig_col - m_new)
        C_ref[0, h] = carry_w * C + _dot_tn((w_s * v).astype(BF16), kb)
        n_ref[0, h] = carry_w * n + jnp.sum(w_s * kb.astype(F32), axis=0, keepdims=True)
        m_ref[0, h] = jnp.broadcast_to(m_new, (1, LANES))


def mlstm(B, T, z_all, bias_row, head_g, C0, n0, m0):
    L = CHUNK if T % CHUNK == 0 else T
    nc = T // L
    H = ML_HEADS
    zrow = lambda w, off: pl.BlockSpec((L, w), lambda b, c: (b * nc + c, off // w))
    st = lambda *shape: pl.BlockSpec((1,) + shape, lambda b, c: (b,) + (0,) * len(shape))
    om, C, n, m = pl.pallas_call(
        functools.partial(_mlstm_kernel, L=L),
        grid=(B, nc),
        in_specs=[zrow(H * ML_DQK, OFF_ML_Q), zrow(H * ML_DQK, OFF_ML_K), zrow(H * ML_DV, OFF_ML_V),
                  zrow(H * ML_DV, OFF_ML_O), zrow(LANES, OFF_IF),
                  pl.BlockSpec((1, LANES), lambda b, c: (0, 0)),
                  pl.BlockSpec((1, H * ML_DV), lambda b, c: (0, 0)),
                  st(H, ML_DV, ML_DQK), st(H, 1, ML_DQK), st(H, 1, LANES)],
        out_specs=[pl.BlockSpec((L, H * ML_DV), lambda b, c: (b * nc + c, 0)),
                   st(H, ML_DV, ML_DQK), st(H, 1, ML_DQK), st(H, 1, LANES)],
        out_shape=[jax.ShapeDtypeStruct((B * T, H * ML_DV), BF16),
                   jax.ShapeDtypeStruct((B, H, ML_DV, ML_DQK), F32),
                   jax.ShapeDtypeStruct((B, H, 1, ML_DQK), F32),
                   jax.ShapeDtypeStruct((B, H, 1, LANES), F32)],
        compiler_params=_cparams(("parallel", "arbitrary")),
        name="mlstm",
    )(z_all, z_all, z_all, z_all, z_all, bias_row, head_g.reshape(1, -1), C0, n0.reshape(B, H, 1, ML_DQK),
      jnp.broadcast_to(m0[:, :, None, None], (B, H, 1, LANES)))
    return om, C, n.reshape(B, H, ML_DQK), m[:, :, 0, 0]


def _rope_lanes(x, cos, sin):
    swapped = pltpu.roll(x, 32, 1) + pltpu.roll(x, 96, 1)
    return x * cos + swapped * sin


def _mla_pre_kernel(qr_ref, lat_ref, kr_ref, cos_ref, sin_ref, g_ref, qro_ref, lato_ref, kro_ref):
    cos = cos_ref[...]
    sin = sin_ref[...]
    for h in range(MLA_HEADS):
        sl = slice(h * LANES, (h + 1) * LANES)
        qro_ref[:, sl] = (_rope_lanes(qr_ref[:, sl], cos, sin) * MLA_SCALE).astype(qro_ref.dtype)
    kro_ref[...] = _rope_lanes(kr_ref[...], cos, sin)
    lat = lat_ref[...]
    lato_ref[...] = lat * lax.rsqrt(jnp.mean(lat * lat, axis=-1, keepdims=True) + NORM_EPS) * g_ref[...]


def mla_pre(rows, z_all, cos_tab, sin_tab, kv_norm_g):
    tm = min(rows.tm, 256)
    M = rows.M
    zspec = lambda w, off: pl.BlockSpec((tm, w), lambda i: (i, off // w))
    W = MLA_HEADS * LANES
    return pl.pallas_call(
        _mla_pre_kernel,
        grid=(M // tm,),
        in_specs=[zspec(W, OFF_QR), zspec(MLA_LATENT, OFF_LAT), zspec(LANES, OFF_KR),
                  pl.BlockSpec((tm, LANES), lambda i: (i, 0)), pl.BlockSpec((tm, LANES), lambda i: (i, 0)),
                  pl.BlockSpec((1, MLA_LATENT), lambda i: (0, 0))],
        out_specs=[pl.BlockSpec((tm, W), lambda i: (i, 0)), pl.BlockSpec((tm, MLA_LATENT), lambda i: (i, 0)),
                   pl.BlockSpec((tm, LANES), lambda i: (i, 0))],
        out_shape=[jax.ShapeDtypeStruct((M, W), BF16), jax.ShapeDtypeStruct((M, MLA_LATENT), F32),
                   jax.ShapeDtypeStruct((M, LANES), F32)],
        compiler_params=_cparams(("parallel",)),
        name="mla_pre",
    )(z_all, z_all, z_all, cos_tab, sin_tab, kv_norm_g.reshape(1, -1))


def _attn_kernel(qn_ref, qr_ref, kn_ref, kr_ref, v_ref, o_ref, m_sc, l_sc, acc_sc, *, tq, tk, past, nk):
    qi = pl.program_id(2)
    q = jnp.concatenate([(qn_ref[...] * MLA_SCALE).astype(BF16), qr_ref[...]], axis=1)
    m_sc[...] = jnp.full_like(m_sc, -jnp.inf)
    l_sc[...] = jnp.zeros_like(l_sc)
    acc_sc[...] = jnp.zeros_like(acc_sc)
    q0 = past + qi * tq
    n_full = jnp.minimum(((q0 // CHUNK + 1) * CHUNK) // tk, nk)
    n_vis = jnp.minimum((((q0 + tq - 1) // CHUNK + 1) * CHUNK + tk - 1) // tk, nk)

    def tile(kj, masked):
        k0 = pl.multiple_of(kj * tk, tk)
        k = jnp.concatenate([kn_ref[pl.ds(k0, tk), :], kr_ref[pl.ds(k0, tk), :]], axis=1)
        s = _dot_nt(q, k)
        if masked:
            q_chunk = (q0 + lax.broadcasted_iota(jnp.int32, (tq, 1), 0)) // CHUNK
            k_chunk = (k0 + lax.broadcasted_iota(jnp.int32, (1, tk), 1)) // CHUNK
            s = jnp.where(k_chunk <= q_chunk, s, -jnp.inf)
        m_prev = m_sc[...]
        m_new = jnp.maximum(m_prev, jnp.max(s, axis=1, keepdims=True))
        alpha = jnp.exp(m_prev - m_new)
        p = jnp.exp(s - m_new)
        l_sc[...] = alpha * l_sc[...] + jnp.sum(p, axis=1, keepdims=True)
        acc_sc[...] = alpha * acc_sc[...] + _dot(p.astype(BF16), v_ref[pl.ds(k0, tk), :])
        m_sc[...] = m_new

    def full_tile(kj, c):
        tile(kj, False)
        return c

    def edge_tile(kj, c):
        tile(kj, True)
        return c

    lax.fori_loop(0, n_full, full_tile, 0)
    lax.fori_loop(n_full, n_vis, edge_tile, 0)
    o_ref[...] = (acc_sc[...] / l_sc[...]).astype(o_ref.dtype)


def mla_attention(B, T, Sk, past, z_all, qr, kv, kr_all):
    H = MLA_HEADS
    tq = min(T, 512)
    tk = 512 if Sk % 512 == 0 else Sk
    nq, nk = T // tq, Sk // tk
    z3 = z_all.reshape(B, T, NZ)
    qr3 = qr.reshape(B, T, H * LANES)
    kv3 = kv.reshape(B, Sk, H * 2 * LANES)
    kr3 = kr_all.reshape(B, Sk, LANES)
    out = pl.pallas_call(
        functools.partial(_attn_kernel, tq=tq, tk=tk, past=past, nk=nk),
        grid=(B, H, nq),
        in_specs=[pl.BlockSpec((None, tq, LANES), lambda b, h, qi: (b, qi, OFF_QN // LANES + h)),
                  pl.BlockSpec((None, tq, LANES), lambda b, h, qi: (b, qi, h)),
                  pl.BlockSpec((None, Sk, LANES), lambda b, h, qi: (b, 0, 2 * h)),
                  pl.BlockSpec((None, Sk, LANES), lambda b, h, qi: (b, 0, 0)),
                  pl.BlockSpec((None, Sk, LANES), lambda b, h, qi: (b, 0, 2 * h + 1))],
        out_specs=pl.BlockSpec((None, tq, LANES), lambda b, h, qi: (b, qi, h)),
        out_shape=jax.ShapeDtypeStruct((B, T, H * MLA_V), BF16),
        scratch_shapes=[pltpu.VMEM((tq, 1), F32), pltpu.VMEM((tq, 1), F32), pltpu.VMEM((tq, MLA_V), F32)],
        compiler_params=_cparams(("parallel", "parallel", "arbitrary")),
        name="mla_attention",
    )(z3, qr3, kv3, kr3, kv3)
    return out.reshape(B * T, H * MLA_V)


def _pair_sum_matrix():
    r = lax.broadcasted_iota(jnp.int32, (LANES, LANES), 0) // RW_HEAD
    c = lax.broadcasted_iota(jnp.int32, (LANES, LANES), 1) // RW_HEAD
    return (r == c).astype(F32)


def _head_sum(x, bd):
    parts = [jnp.dot(x[:, p * LANES:(p + 1) * LANES], bd, precision=HIGHEST, preferred_element_type=F32)
             for p in range(RW_WIDTH // LANES)]
    return jnp.concatenate(parts, axis=1)


def _rwkv_pre_kernel(r_ref, k_ref, v_ref, lo_ref, pr_ref, pk_ref, pv_ref, plo_ref, mu_ref, mulo_ref,
                     w0_ref, w2_ref, a0_ref, a2_ref, g2_ref, kk_ref, ka_ref, rk_ref,
                     ro_ref, wo_ref, ko_ref, vo_ref, ao_ref, bo_ref, go_ref, bonus_ref):
    W = RW_WIDTH
    mix = lambda z, p, mu: z + (p - z) * mu
    r = mix(r_ref[...], pr_ref[...], mu_ref[:, 0:W])
    k = mix(k_ref[...], pk_ref[...], mu_ref[:, W:2 * W])
    v = mix(v_ref[...], pv_ref[...], mu_ref[:, 2 * W:3 * W])
    lo = mix(lo_ref[...], plo_ref[...], mulo_ref[...])
    wd, ad, gd = lo[:, 0:64], lo[:, 64:128], lo[:, 128:256]
    hdot = lambda x, w: jnp.dot(x, w, precision=HIGHEST, preferred_element_type=F32)
    wpre = w0_ref[...] + hdot(jnp.tanh(wd), w2_ref[...])
    w_raw = jnp.minimum(wpre, 0.0) - jnp.log(1.0 + jnp.exp(-jnp.abs(wpre))) - 0.5
    log_decay = -jnp.exp(w_raw)
    a = _sigmoid(a0_ref[...] + hdot(ad, a2_ref[...]))
    g = hdot(_sigmoid(gd), g2_ref[...])
    bd = _pair_sum_matrix()
    kk = k * kk_ref[...]
    kk = kk / jnp.maximum(jnp.sqrt(_head_sum(kk * kk, bd)), 1e-12)
    k = k * (1.0 + (a - 1.0) * ka_ref[...])
    ro_ref[...] = r
    wo_ref[...] = log_decay
    ko_ref[...] = k
    vo_ref[...] = v
    ao_ref[...] = -kk
    bo_ref[...] = kk * a
    go_ref[...] = g
    bonus_ref[...] = _head_sum(r * k * rk_ref[...], bd) * v


def rwkv_pre(rows, z_all, zprev, mu, w0, w2, a0, a2, g2, k_k, k_a, r_k):
    M = rows.M
    tm = min(rows.tm, 256)
    W = RW_WIDTH
    zs = lambda w, off: pl.BlockSpec((tm, w), lambda i: (i, off // w))
    full = lambda a: pl.BlockSpec(a.shape, lambda i: (0,) * a.ndim)
    row = lambda a: a.reshape(1, -1)
    params = [row(mu[:3 * W]), row(mu[3 * W:]), row(w0), w2, row(a0), a2, g2, row(k_k), row(k_a), row(r_k)]
    outs = pl.pallas_call(
        _rwkv_pre_kernel,
        grid=(M // tm,),
        in_specs=[zs(W, OFF_RW_R), zs(W, OFF_RW_K), zs(W, OFF_RW_V), zs(RW_LORA, OFF_RW_LORA),
                  zs(W, 0), zs(W, W), zs(W, 2 * W), zs(RW_LORA, 3 * W)] + [full(p) for p in params],
        out_specs=[pl.BlockSpec((tm, W), lambda i: (i, 0))] * 8,
        out_shape=[jax.ShapeDtypeStruct((M, W), F32)] * 8,
        compiler_params=_cparams(("parallel",)),
        name="rwkv_pre",
    )(z_all, z_all, z_all, z_all, zprev, zprev, zprev, zprev, *params)
    return outs


_DIMS = {"nn": (((1,), (0,)), ((), ())), "nt": (((1,), (1,)), ((), ())), "tn": (((0,), (0,)), ((), ()))}


def _split_bf16(x):
    hi = x.astype(BF16)
    return hi, (x - hi.astype(F32)).astype(BF16)


def _mm(a, b, kind, passes):
    dims = _DIMS[kind]
    if passes == 6:
        return lax.dot_general(a, b, dims, precision=HIGHEST, preferred_element_type=F32)
    dg = lambda x, y: lax.dot_general(x, y, dims, preferred_element_type=F32)
    if passes == 1:
        return dg(a.astype(BF16), b.astype(BF16))
    ah, al = _split_bf16(a)
    bh, bl = _split_bf16(b)
    return dg(ah, bh) + (dg(al, bh) + dg(ah, bl))


def _rwkv_chunk_kernel(r_ref, lw_ref, k_ref, v_ref, a_ref, b_ref, S0_ref, y_ref, S_ref, *, L, P_G, P_T, P_X, P_S):
    c = pl.program_id(1)

    @pl.when(c == 0)
    def _():
        S_ref[...] = S0_ref[...]

    N = RW_HEAD
    row = lax.broadcasted_iota(jnp.int32, (L, L), 0)
    col = lax.broadcasted_iota(jnp.int32, (L, L), 1)
    incl = row >= col
    strict = row > col
    eye = (row == col).astype(F32)
    tri = incl.astype(F32)
    cum_all = jnp.dot(tri, lw_ref[...], precision=HIGHEST, preferred_element_type=F32)
    sls = [slice(h * N, (h + 1) * N) for h in range(RW_HEADS)]
    each = lambda f, *lists: [f(*xs) for xs in zip(*lists)]
    S0 = [S_ref[0, h] for h in range(RW_HEADS)]
    lw = [lw_ref[:, s] for s in sls]
    cum = [cum_all[:, s] for s in sls]
    V = [v_ref[:, s] for s in sls]
    g_in = each(jnp.exp, cum)
    g_inv = each(lambda c: jnp.exp(-c), cum)
    At = each(lambda s, c, w: a_ref[:, s] * jnp.exp(c - w), sls, cum, lw)
    Bt = each(lambda s, g: b_ref[:, s] * g, sls, g_inv)
    Kt = each(lambda s, g: k_ref[:, s] * g, sls, g_inv)
    Rt = each(lambda s, g: r_ref[:, s] * g, sls, g_in)
    right = each(lambda x, y: jnp.concatenate([x, y], axis=0), Bt, Kt)
    G = each(lambda x, y, rt: _mm(jnp.concatenate([x, y], axis=0), rt, "nt", P_G), At, Rt, right)
    Nab = each(lambda g: jnp.where(strict, g[:L, :L], 0.0), G)
    Nak = each(lambda g: jnp.where(strict, g[:L, L:], 0.0), G)
    Mrb = each(lambda g: jnp.where(incl, g[L:, :L], 0.0), G)
    Mrk = each(lambda g: jnp.where(incl, g[L:, L:], 0.0), G)
    T = each(lambda n: eye + jnp.where(row // 2 == col // 2, n, 0.0), Nab)
    s = 2
    while s < L:
        off = (row // (2 * s) == col // (2 * s)) & (row // s != col // s)
        T = each(lambda t, n: t + _mm(t, _mm(jnp.where(off, n, 0.0), t, "nn", P_T), "nn", P_T), T, Nab)
        s *= 2
    NV = each(lambda n, v: _mm(n, v, "nn", P_X), Nak, V)
    TX = each(lambda t, x, nv: _mm(t, jnp.concatenate([x, nv], axis=1), "nn", P_X), T, At, NV)
    MX = each(lambda m, tx: _mm(m, tx, "nn", P_X), Mrb, TX)
    MV = each(lambda m, v: _mm(m, v, "nn", P_X), Mrk, V)
    Y = each(lambda rt, mx, mv, s0: _mm(rt + mx[:, :N], s0, "nt", P_S) + (mx[:, N:] + mv), Rt, MX, MV, S0)
    U = each(lambda tx, s0: _mm(tx[:, :N], s0, "nt", P_S) + tx[:, N:], TX, S0)
    upd = each(lambda u, v, rt: _mm(jnp.concatenate([u, v], axis=0), rt, "tn", P_S), U, V, right)
    for h, s in enumerate(sls):
        y_ref[:, s] = Y[h]
        S_ref[0, h] = (S0[h] + upd[h]) * g_in[h][L - 1:L, :]


def rwkv_scan(B, T, r, lw, k, v, a, b, S0):
    L = CHUNK if T % CHUNK == 0 else T
    assert L & (L - 1) == 0
    nc = T // L
    xs = pl.BlockSpec((L, RW_WIDTH), lambda bi, c: (bi * nc + c, 0))
    ss = pl.BlockSpec((1, RW_HEADS, RW_HEAD, RW_HEAD), lambda bi, c: (bi, 0, 0, 0))
    return pl.pallas_call(
        functools.partial(_rwkv_chunk_kernel, L=L, P_G=3, P_T=3, P_X=3, P_S=3),
        grid=(B, nc),
        in_specs=[xs] * 6 + [ss],
        out_specs=[xs, ss],
        out_shape=[jax.ShapeDtypeStruct((B * T, RW_WIDTH), F32),
                   jax.ShapeDtypeStruct((B, RW_HEADS, RW_HEAD, RW_HEAD), F32)],
        compiler_params=_cparams(("parallel", "arbitrary")),
        name="rwkv_scan",
    )(r, lw, k, v, a, b, S0)


def _rwkv_post_kernel(y_ref, g_ref, bonus_ref, lg_ref, lb_ref, o_ref):
    bd = _pair_sum_matrix()
    y = y_ref[...]
    mean = _head_sum(y, bd) * (1.0 / RW_HEAD)
    d = y - mean
    var = _head_sum(d * d, bd) * (1.0 / RW_HEAD)
    yn = d * lax.rsqrt(var + RW_LN_EPS) * lg_ref[...] + lb_ref[...]
    o_ref[...] = ((yn + bonus_ref[...]) * g_ref[...]).astype(o_ref.dtype)


def rwkv_post(rows, y, g, bonus, lnx_g, lnx_b):
    M = rows.M
    tm = min(rows.tm, 256)
    W = RW_WIDTH
    xs = pl.BlockSpec((tm, W), lambda i: (i, 0))
    ps = pl.BlockSpec((1, W), lambda i: (0, 0))
    return pl.pallas_call(
        _rwkv_post_kernel,
        grid=(M // tm,),
        in_specs=[xs, xs, xs, ps, ps],
        out_specs=xs,
        out_shape=jax.ShapeDtypeStruct((M, W), BF16),
        compiler_params=_cparams(("parallel",)),
        name="rwkv_post",
    )(y, g, bonus, lnx_g.reshape(1, W), lnx_b.reshape(1, W))


def _prep_w_in(w):
    D = w.shape[0]
    o = 0
    take = lambda n: (w[:, o:o + n], o + n)
    ml_q, o = take(512)
    ml_k, o = take(512)
    ml_v, o = take(1024)
    ml_o, o = take(1024)
    ml_i, o = take(4)
    ml_f, o = take(4)
    q, o = take(MLA_HEADS * (MLA_NOPE + MLA_ROPE))
    lat, o = take(MLA_LATENT)
    kr, o = take(MLA_ROPE)
    rw_r, o = take(1024)
    rw_k, o = take(1024)
    rw_v, o = take(1024)
    rw_lo, o = take(RW_LORA)
    gate, o = take(3 * D_MODEL)
    assert o == w.shape[1]
    q = q.reshape(D, MLA_HEADS, MLA_NOPE + MLA_ROPE)
    qn = q[:, :, :MLA_NOPE].reshape(D, MLA_HEADS * MLA_NOPE)
    qr = jnp.pad(q[:, :, MLA_NOPE:], ((0, 0), (0, 0), (0, LANES - MLA_ROPE))).reshape(D, MLA_HEADS * LANES)
    kr = jnp.pad(kr, ((0, 0), (0, LANES - MLA_ROPE)))
    gif = jnp.pad(jnp.concatenate([ml_i, ml_f], axis=1), ((0, 0), (0, LANES - 2 * ML_HEADS)))
    out = jnp.concatenate([ml_q, ml_k, ml_v, ml_o, qn, qr, rw_r, rw_k, rw_v, gate, lat, rw_lo, kr, gif], axis=1)
    assert out.shape[1] == NZ
    return out.astype(BF16)


def _block_cols(w, tn):
    K, N = w.shape
    return w.reshape(K, N // tn, tn).transpose(1, 0, 2)


def _row_tile(M, cap):
    return max(t for t in range(8, cap + 1, 8) if M % t == 0)


def _rope_tables(T, past, reps):
    half = MLA_ROPE // 2
    inv = ROPE_BASE ** (-jnp.arange(half, dtype=F32) / half)
    ang = (past + jnp.arange(T)).astype(F32)[:, None] * inv[None, :]
    cos, sin = jnp.cos(ang), jnp.sin(ang)
    zeros = jnp.zeros((T, LANES - MLA_ROPE), F32)
    cos_t = jnp.concatenate([cos, cos, zeros], axis=1)
    sin_t = jnp.concatenate([-sin, sin, zeros], axis=1)
    return jnp.tile(cos_t, (reps, 1)), jnp.tile(sin_t, (reps, 1))


def _layer(rows, x, mod, l, W, P, lat_past, kr_past, C0, n0, m0, S0, shift0, tabs):
    B, T, M = rows.B, rows.T, rows.M
    D = D_MODEL
    m9 = [rows.expand(mod[:, i * D:(i + 1) * D]) for i in range(N_ADA)]
    sh1, sc1, g1, sh2, sc2, g2, sh3, sc3, g3 = m9
    h = norm_mod(rows, x, P['norm_g'][l, 0], sh1, sc1)
    x = mm_resid(rows, mm_swiglu(rows, h, W['ffn_in'][l][0]), W['ffn_out'][l][0], x, g1, 0.5)
    h = norm_mod(rows, x, P['norm_g'][l, 1], sh2, sc2)
    z_all = matmul(h, W['w_in'][l], rows.tm, name="mm_in")
    bias_row = jnp.pad(jnp.concatenate([P['mlstm_i_bias'][l], P['mlstm_f_bias'][l]]), (0, LANES - 2 * ML_HEADS))[None]
    om, C, n, m = mlstm(B, T, z_all, bias_row, P['mlstm_head_g'][l], C0, n0, m0)
    past = lat_past.shape[1]
    qr, lat, kr = mla_pre(rows, z_all, tabs[0], tabs[1], P['mla_kv_norm_g'][l])
    Sk = past + T
    if past:
        lat_all = jnp.concatenate([lat_past, lat.reshape(B, T, -1)], axis=1).reshape(B * Sk, -1)
        kr_all = jnp.concatenate([jnp.pad(kr_past, ((0, 0), (0, 0), (0, LANES - MLA_ROPE))), kr.reshape(B, T, -1)],
                                 axis=1).reshape(B * Sk, LANES)
    else:
        lat_all, kr_all = lat, kr
    kv = matmul(lat_all, W['kv_b'][l], _row_tile(B * Sk, 1536), out_dtype=BF16, name="mm_kv")
    oa = mla_attention(B, T, Sk, past, z_all, qr, kv, kr_all.astype(BF16))
    z3 = z_all.reshape(B, T, NZ)
    zr = jnp.concatenate([z3[:, :, OFF_RW_R:OFF_RW_R + 3 * RW_WIDTH], z3[:, :, OFF_RW_LORA:OFF_RW_LORA + RW_LORA]],
                         axis=2)
    zprev = jnp.concatenate([shift0[:, None, :], zr[:, :-1]], axis=1).reshape(M, RW_COLS)
    r_, w_, k_, v_, a_, b_, g_, bonus = rwkv_pre(rows, z_all, zprev, P['rwkv_mu'][l], P['rwkv_w0'][l], P['rwkv_w2'][l],
                                                 P['rwkv_a0'][l], P['rwkv_a2'][l], P['rwkv_g2'][l], P['rwkv_k_k'][l],
                                                 P['rwkv_k_a'][l], P['rwkv_r_k'][l].reshape(-1))
    y, S = rwkv_scan(B, T, r_, w_, k_, v_, a_, b_, S0)
    orw = rwkv_post(rows, y, g_, bonus, P['rwkv_lnx_g'][l], P['rwkv_lnx_b'][l])
    shift = zr[:, -1]
    mixed = mm_branch(rows, om, oa, orw, W['branch'][l], z_all, P['b_merge'][l][None])
    x = mm_resid(rows, mixed, W['w_out'][l], x, g2, 1.0)
    h = norm_mod(rows, x, P['norm_g'][l, 2], sh3, sc3)
    x = mm_resid(rows, mm_swiglu(rows, h, W['ffn_in'][l][1]), W['ffn_out'][l][1], x, g3, 0.5)
    new = (lat.reshape(B, T, MLA_LATENT), kr.reshape(B, T, LANES)[:, :, :MLA_ROPE], C, n, m, S, shift)
    return x, new


def _trunk(rows, x, mods, W, P, lat_cache, kr_cache, C_st, n_st, m_st, S_st, sh_st):
    depth = len(mods)
    past = lat_cache.shape[2]
    tabs = _rope_tables(rows.T, past, rows.B)
    x = x.reshape(rows.M, D_MODEL)
    new = []
    for l in range(depth):
        x, st = _layer(rows, x, mods[l], l, W, P, lat_cache[l], kr_cache[l], C_st[l], n_st[l], m_st[l], S_st[l],
                       sh_st[l], tabs)
        new.append(st)
    y = final_norm(rows, x, P['final_norm_g']).reshape(rows.B, rows.T, D_MODEL)
    return y, tuple(jnp.stack(s, axis=0) for s in zip(*new))


def kernel(x_prompt, x_sample, c_prompt, c_sample, cache_mla_latent, cache_mla_krope, state_mlstm_C, state_mlstm_n,
           state_mlstm_m, state_rwkv_S, state_rwkv_shift, ada_w, ada_b, norm_g, ffn_w_in, ffn_w_out, w_in,
           mlstm_i_bias, mlstm_f_bias, mlstm_head_g, mla_kv_norm_g, mla_w_kv_b, rwkv_mu, rwkv_w0, rwkv_w2, rwkv_a0,
           rwkv_a2, rwkv_g2, rwkv_k_k, rwkv_k_a, rwkv_r_k, rwkv_lnx_g, rwkv_lnx_b, w_branch, b_merge, w_out,
           final_norm_g):
    P = dict(norm_g=norm_g, mlstm_i_bias=mlstm_i_bias, mlstm_f_bias=mlstm_f_bias, mlstm_head_g=mlstm_head_g,
             mla_kv_norm_g=mla_kv_norm_g, rwkv_mu=rwkv_mu, rwkv_w0=rwkv_w0, rwkv_w2=rwkv_w2, rwkv_a0=rwkv_a0,
             rwkv_a2=rwkv_a2, rwkv_g2=rwkv_g2, rwkv_k_k=rwkv_k_k, rwkv_k_a=rwkv_k_a, rwkv_r_k=rwkv_r_k,
             rwkv_lnx_g=rwkv_lnx_g, rwkv_lnx_b=rwkv_lnx_b, b_merge=b_merge, final_norm_g=final_norm_g)
    depth = w_in.shape[0]
    blk = lambda w: _block_cols(w.astype(BF16), TN)
    W = dict(ffn_in=[[blk(ffn_w_in[l, i]) for i in range(2)] for l in range(depth)],
             ffn_out=[[blk(ffn_w_out[l, i]) for i in range(2)] for l in range(depth)],
             w_in=[_block_cols(_prep_w_in(w_in[l]), TN) for l in range(depth)],
             kv_b=[blk(mla_w_kv_b[l]) for l in range(depth)],
             branch=[jnp.concatenate([blk(w_branch[l, n]) for n in range(3)], axis=0) for l in range(depth)],
             w_out=[blk(w_out[l]) for l in range(depth)])
    Bp, Tp, _ = x_prompt.shape
    Bs, Ts, _ = x_sample.shape
    n_c = Bp + Bs
    c_all = jnp.pad(jnp.concatenate([c_prompt, c_sample], axis=0), ((0, (-n_c) % 8), (0, 0)))
    mods = [ada_mod(c_all, ada_w[l], ada_b[l]) for l in range(depth)]
    rows_p = Rows(Bp, Tp, TM_MAX)
    rows_s = Rows(Bs, Ts, TM_MAX)
    zeros = lambda *s: jnp.zeros((depth, Bp) + s, F32)
    y_p, st_p = _trunk(rows_p, x_prompt, [m[:Bp] for m in mods], W, P,
                       zeros(0, MLA_LATENT), zeros(0, MLA_ROPE), zeros(ML_HEADS, ML_DV, ML_DQK), zeros(ML_HEADS, ML_DQK),
                       zeros(ML_HEADS), zeros(RW_HEADS, RW_HEAD, RW_HEAD), zeros(RW_COLS))
    y_s, st_s = _trunk(rows_s, x_sample, [m[Bp:n_c] for m in mods], W, P,
                       cache_mla_latent, cache_mla_krope, state_mlstm_C, state_mlstm_n, state_mlstm_m, state_rwkv_S,
                       state_rwkv_shift)
    return (y_p, y_s) + st_p + st_s
```

```python
import functools

import jax
import jax.numpy as jnp
import numpy as np
from jax import lax
from jax.experimental import pallas as pl
from jax.experimental.pallas import tpu as pltpu

F32 = jnp.float32
BF16 = jnp.bfloat16
HIGHEST = lax.Precision.HIGHEST

D_MODEL = 2048
CHUNK = 64
NORM_EPS = 1e-6
D_FF = 5632
N_ADA = 9
ML_HEADS, ML_DQK, ML_DV, ML_GATE_CAP = 4, 128, 256, 15.0
ML_CHUNK = 256
MLA_HEADS, MLA_NOPE, MLA_ROPE, MLA_V, MLA_LATENT = 8, 128, 64, 128, 512
MLA_SCALE = (MLA_NOPE + MLA_ROPE) ** -0.5
ROPE_BASE = 10000.0
RW_HEADS, RW_HEAD = 16, 64
RW_WIDTH = RW_HEADS * RW_HEAD
RW_LN_EPS = 64e-5
RW_LORA = 256
RW_COLS = 3 * RW_WIDTH + RW_LORA
BRANCH_WIDTH = 1024

LANES = 128
VMEM_LIMIT = 56 * 1024 * 1024
TM_MAX = 1024
TN = 512

OFF_ML_Q, OFF_ML_K, OFF_ML_V, OFF_ML_O = 0, 512, 1024, 2048
OFF_QN, OFF_QR = 3072, 4096
OFF_RW_R, OFF_RW_K, OFF_RW_V = 5120, 6144, 7168
OFF_GATE = 8192
OFF_LAT = 14336
OFF_RW_LORA = 14848
OFF_KR = 15104
OFF_IF = 15232
NZ = 15360


def _cparams(sem):
    return pltpu.CompilerParams(dimension_semantics=sem, vmem_limit_bytes=VMEM_LIMIT)


def _sigmoid(x):
    return 1.0 / (1.0 + jnp.exp(-x))


def _silu(x):
    return x * _sigmoid(x)


def _dot(a, b):
    return jnp.dot(a, b, preferred_element_type=F32)


def _dot_nt(a, b, precision=None):
    return lax.dot_general(a, b, (((1,), (1,)), ((), ())), precision=precision, preferred_element_type=F32)


def _dot_tn(a, b, precision=None):
    return lax.dot_general(a, b, (((0,), (0,)), ((), ())), precision=precision, preferred_element_type=F32)


class Rows:
    def __init__(self, B, T, tm_max):
        self.B, self.T, self.M = B, T, B * T
        if T >= 256:
            self.tm = min(tm_max, T)
            assert T % self.tm == 0
            self.tpg = T // self.tm
            self.R = 1
        else:
            self.tm = self.M
            self.tpg = 1
            self.R = self.M
        self.nm = self.M // self.tm

    def expand(self, v):
        if self.R == 1:
            return v[:, None, :]
        return jnp.repeat(v, self.T, axis=0)[None]

    def vec_spec(self, tn, col=lambda j: j):
        return pl.BlockSpec((None, self.R, tn), lambda i, j: (i // self.tpg, 0, col(j)))


def _ada_kernel(c_ref, w_ref, b_ref, o_ref):
    c = c_ref[...]
    o_ref[...] = _dot(_silu(c).astype(BF16), w_ref[...].astype(BF16)) + b_ref[...]


def ada_mod(c, w, b):
    R, D = c.shape
    N = w.shape[1]
    tn = 1024
    return pl.pallas_call(
        _ada_kernel,
        grid=(N // tn,),
        in_specs=[pl.BlockSpec((R, D), lambda j: (0, 0)),
                  pl.BlockSpec((D, tn), lambda j: (0, j)),
                  pl.BlockSpec((1, tn), lambda j: (0, j))],
        out_specs=pl.BlockSpec((R, tn), lambda j: (0, j)),
        out_shape=jax.ShapeDtypeStruct((R, N), F32),
        compiler_params=_cparams(("arbitrary",)),
        name="ada_mod",
    )(c, w, b.reshape(1, N))


def _norm_mod_kernel(x_ref, g_ref, sh_ref, sc_ref, o_ref):
    x = x_ref[...]
    y = x * lax.rsqrt(jnp.mean(x * x, axis=-1, keepdims=True) + NORM_EPS) * g_ref[...]
    o_ref[...] = (y * (1.0 + sc_ref[...]) + sh_ref[...]).astype(o_ref.dtype)


def norm_mod(rows, x, g, shift, scale):
    D = x.shape[1]
    tm = min(rows.tm, 256)
    sub = rows.tm // tm
    vec = pl.BlockSpec((None, rows.R if rows.R == 1 else tm, D),
                       (lambda i: (i // (rows.tpg * sub), 0, 0)) if rows.R == 1 else (lambda i: (0, i, 0)))
    return pl.pallas_call(
        _norm_mod_kernel,
        grid=(rows.M // tm,),
        in_specs=[pl.BlockSpec((tm, D), lambda i: (i, 0)),
                  pl.BlockSpec((1, D), lambda i: (0, 0)), vec, vec],
        out_specs=pl.BlockSpec((tm, D), lambda i: (i, 0)),
        out_shape=jax.ShapeDtypeStruct((rows.M, D), BF16),
        compiler_params=_cparams(("parallel",)),
        name="norm_mod",
    )(x, g.reshape(1, D), shift, scale)


def _final_norm_kernel(x_ref, g_ref, o_ref):
    x = x_ref[...]
    o_ref[...] = x * lax.rsqrt(jnp.mean(x * x, axis=-1, keepdims=True) + NORM_EPS) * g_ref[...]


def final_norm(rows, x, g):
    D = x.shape[1]
    tm = min(rows.tm, 256)
    return pl.pallas_call(
        _final_norm_kernel,
        grid=(rows.M // tm,),
        in_specs=[pl.BlockSpec((tm, D), lambda i: (i, 0)), pl.BlockSpec((1, D), lambda i: (0, 0))],
        out_specs=pl.BlockSpec((tm, D), lambda i: (i, 0)),
        out_shape=jax.ShapeDtypeStruct((rows.M, D), F32),
        compiler_params=_cparams(("parallel",)),
        name="final_norm",
    )(x, g.reshape(1, D))


def _mm_kernel(a_ref, w_ref, o_ref):
    o_ref[...] = _dot(a_ref[...].astype(BF16), w_ref[...]).astype(o_ref.dtype)


def matmul(a, w, tm, out_dtype=F32, name="matmul"):
    M, K = a.shape
    nj, _, tn = w.shape
    N = nj * tn
    return pl.pallas_call(
        _mm_kernel,
        grid=(M // tm, nj),
        in_specs=[pl.BlockSpec((tm, K), lambda i, j: (i, 0)), pl.BlockSpec((None, K, tn), lambda i, j: (j, 0, 0))],
        out_specs=pl.BlockSpec((tm, tn), lambda i, j: (i, j)),
        out_shape=jax.ShapeDtypeStruct((M, N), out_dtype),
        compiler_params=_cparams(("parallel", "arbitrary")),
        name=name,
    )(a, w)


def _swiglu_kernel(h_ref, wg_ref, wu_ref, o_ref):
    h = h_ref[...]
    g = _dot(h, wg_ref[...])
    u = _dot(h, wu_ref[...])
    o_ref[...] = (_silu(g) * u).astype(o_ref.dtype)


def mm_swiglu(rows, h, w_in):
    K = h.shape[1]
    tn = w_in.shape[2]
    nj = D_FF // tn
    return pl.pallas_call(
        _swiglu_kernel,
        grid=(rows.nm, nj),
        in_specs=[pl.BlockSpec((rows.tm, K), lambda i, j: (i, 0)),
                  pl.BlockSpec((None, K, tn), lambda i, j: (j, 0, 0)),
                  pl.BlockSpec((None, K, tn), lambda i, j: (j + nj, 0, 0))],
        out_specs=pl.BlockSpec((rows.tm, tn), lambda i, j: (i, j)),
        out_shape=jax.ShapeDtypeStruct((rows.M, D_FF), BF16),
        compiler_params=_cparams(("parallel", "arbitrary")),
        name="mm_swiglu",
    )(h, w_in, w_in)


def _resid_kernel(a_ref, w_ref, x_ref, g_ref, o_ref, *, coef):
    o_ref[...] = x_ref[...] + (coef * g_ref[...]) * _dot(a_ref[...], w_ref[...])


def mm_resid(rows, a, w, x, gate, coef):
    K = a.shape[1]
    nj, _, tn = w.shape
    N = nj * tn
    return pl.pallas_call(
        functools.partial(_resid_kernel, coef=coef),
        grid=(rows.nm, nj),
        in_specs=[pl.BlockSpec((rows.tm, K), lambda i, j: (i, 0)),
                  pl.BlockSpec((None, K, tn), lambda i, j: (j, 0, 0)),
                  pl.BlockSpec((rows.tm, tn), lambda i, j: (i, j)),
                  rows.vec_spec(tn)],
        out_specs=pl.BlockSpec((rows.tm, tn), lambda i, j: (i, j)),
        out_shape=jax.ShapeDtypeStruct((rows.M, N), F32),
        compiler_params=_cparams(("parallel", "arbitrary")),
        name="mm_resid",
    )(a, w, x, gate)


def _branch_kernel(om_ref, oa_ref, or_ref, w0_ref, w1_ref, w2_ref, z0_ref, z1_ref, z2_ref, b0_ref, b1_ref, b2_ref,
                   o_ref):
    acc = _sigmoid(z0_ref[...] + b0_ref[...]) * _dot(om_ref[...], w0_ref[...])
    acc += _sigmoid(z1_ref[...] + b1_ref[...]) * _dot(oa_ref[...], w1_ref[...])
    acc += _sigmoid(z2_ref[...] + b2_ref[...]) * _dot(or_ref[...], w2_ref[...])
    o_ref[...] = acc.astype(o_ref.dtype)


def mm_branch(rows, om, oa, orw, w_branch, z_all, b_merge):
    tn = w_branch.shape[2]
    D = D_MODEL
    nj = D // tn
    a_spec = pl.BlockSpec((rows.tm, BRANCH_WIDTH), lambda i, j: (i, 0))
    w_specs = [pl.BlockSpec((None, BRANCH_WIDTH, tn), lambda i, j, n=n: (n * nj + j, 0, 0)) for n in range(3)]
    z_specs = [pl.BlockSpec((rows.tm, tn), lambda i, j, n=n: (i, (OFF_GATE + n * D) // tn + j)) for n in range(3)]
    b_specs = [pl.BlockSpec((1, tn), lambda i, j, n=n: (0, n * D // tn + j)) for n in range(3)]
    return pl.pallas_call(
        _branch_kernel,
        grid=(rows.nm, nj),
        in_specs=[a_spec, a_spec, a_spec] + w_specs + z_specs + b_specs,
        out_specs=pl.BlockSpec((rows.tm, tn), lambda i, j: (i, j)),
        out_shape=jax.ShapeDtypeStruct((rows.M, D), BF16),
        compiler_params=_cparams(("parallel", "arbitrary")),
        name="mm_branch",
    )(om, oa, orw, w_branch, w_branch, w_branch, z_all, z_all, z_all, b_merge, b_merge, b_merge)


def _mlstm_kernel(q_ref, k_ref, v_ref, o_ref, if_ref, bias_ref, hg_ref, C0_ref, n0_ref, m0_ref,
                  om_ref, C_ref, n_ref, m_ref, *, L):
    c = pl.program_id(1)

    @pl.when(c == 0)
    def _():
        C_ref[...] = C0_ref[...]
        n_ref[...] = n0_ref[...]
        m_ref[...] = m0_ref[...]

    H = ML_HEADS
    sc = ML_GATE_CAP * jnp.tanh((if_ref[...] + bias_ref[...]) * (1.0 / ML_GATE_CAP))
    lf = jnp.minimum(sc, 0.0) - jnp.log(1.0 + jnp.exp(-jnp.abs(sc)))
    row = lax.broadcasted_iota(jnp.int32, (L, L), 0)
    col = lax.broadcasted_iota(jnp.int32, (L, L), 1)
    causal = row >= col
    bcum = jnp.dot(causal.astype(F32), lf, precision=HIGHEST, preferred_element_type=F32)
    sel = (lax.broadcasted_iota(jnp.int32, (8, LANES), 0) == lax.broadcasted_iota(jnp.int32, (8, LANES), 1)).astype(F32)
    scT = _dot_nt(sel, sc, HIGHEST)
    bT = _dot_nt(sel, bcum, HIGHEST)
    hs = range(H)
    each = lambda f, *lists: [f(*xs) for xs in zip(*lists)]
    qsl = [slice(h * ML_DQK, (h + 1) * ML_DQK) for h in hs]
    vsl = [slice(h * ML_DV, (h + 1) * ML_DV) for h in hs]
    b_col = [bcum[:, H + h:H + h + 1] for h in hs]
    b_row = [bT[H + h:H + h + 1, :] for h in hs]
    ig_row = [scT[h:h + 1, :] for h in hs]
    ig_col = [sc[:, h:h + 1] for h in hs]
    m_prev = [m_ref[0, h, :, 0:1] for h in hs]
    C = [C_ref[0, h] for h in hs]
    n = [n_ref[0, h] for h in hs]
    qf = [q_ref[:, s] for s in qsl]
    q = each(lambda x: x.astype(BF16), qf)
    kb = [(k_ref[:, s] * (ML_DQK ** -0.5)).astype(BF16) for s in qsl]
    v = [v_ref[:, s] for s in vsl]
    logD = each(lambda bc, br, ir: jnp.where(causal, bc - br + ir, -jnp.inf), b_col, b_row, ig_row)
    g = each(lambda bc, mp: bc + mp, b_col, m_prev)
    m_t = each(lambda gg, ld: jnp.maximum(gg, jnp.max(ld, axis=1, keepdims=True)), g, logD)
    Dm = each(lambda ld, mt: jnp.exp(ld - mt), logD, m_t)
    inter = each(lambda gg, mt: jnp.exp(gg - mt), g, m_t)
    qk = each(lambda qq, kk, d: _dot_nt(qq, kk) * d, q, kb, Dm)
    qC = each(lambda qq, cc: _dot_nt(qq, cc.astype(BF16)), q, C)
    num = each(lambda s, vv, it, x: _dot(s.astype(BF16), vv.astype(BF16)) + it * x, qk, v, inter, qC)
    qn = each(lambda x, nn: jnp.sum(x * nn, axis=1, keepdims=True), qf, n)
    den = each(lambda s, it, x: jnp.sum(s, axis=1, keepdims=True) + it * x, qk, inter, qn)
    hh = each(lambda nu, de, mt: nu / jnp.maximum(jnp.abs(de), jnp.exp(-mt)), num, den, m_t)
    m_new = each(lambda mt: mt[L - 1:L, :], m_t)
    b_last = each(lambda bc: bc[L - 1:L, :], b_col)
    carry_w = each(lambda bl, mp, mn: jnp.exp(bl + mp - mn), b_last, m_prev, m_new)
    w_s = each(lambda bl, bc, ic, mn: jnp.exp(bl - bc + ic - mn), b_last, b_col, ig_col, m_new)
    dC = each(lambda w, vv, kk: _dot_tn((w * vv).astype(BF16), kk), w_s, v, kb)
    dn = each(lambda w, kk: jnp.sum(w * kk.astype(F32), axis=0, keepdims=True), w_s, kb)
    for h in hs:
        y = hh[h] * lax.rsqrt(jnp.mean(hh[h] * hh[h], axis=-1, keepdims=True) + NORM_EPS) * hg_ref[:, vsl[h]]
        om_ref[:, vsl[h]] = (_sigmoid(o_ref[:, vsl[h]]) * y).astype(om_ref.dtype)
        C_ref[0, h] = carry_w[h] * C[h] + dC[h]
        n_ref[0, h] = carry_w[h] * n[h] + dn[h]
        m_ref[0, h] = jnp.broadcast_to(m_new[h], (1, LANES))


def mlstm(B, T, z_all, bias_row, head_g, C0, n0, m0):
    L = next((c for c in (ML_CHUNK, CHUNK) if T % c == 0), T)
    nc = T // L
    H = ML_HEADS
    zrow = lambda w, off: pl.BlockSpec((L, w), lambda b, c: (b * nc + c, off // w))
    st = lambda *shape: pl.BlockSpec((1,) + shape, lambda b, c: (b,) + (0,) * len(shape))
    om, C, n, m = pl.pallas_call(
        functools.partial(_mlstm_kernel, L=L),
        grid=(B, nc),
        in_specs=[zrow(H * ML_DQK, OFF_ML_Q), zrow(H * ML_DQK, OFF_ML_K), zrow(H * ML_DV, OFF_ML_V),
                  zrow(H * ML_DV, OFF_ML_O), zrow(LANES, OFF_IF),
                  pl.BlockSpec((1, LANES), lambda b, c: (0, 0)),
                  pl.BlockSpec((1, H * ML_DV), lambda b, c: (0, 0)),
                  st(H, ML_DV, ML_DQK), st(H, 1, ML_DQK), st(H, 1, LANES)],
        out_specs=[pl.BlockSpec((L, H * ML_DV), lambda b, c: (b * nc + c, 0)),
                   st(H, ML_DV, ML_DQK), st(H, 1, ML_DQK), st(H, 1, LANES)],
        out_shape=[jax.ShapeDtypeStruct((B * T, H * ML_DV), BF16),
                   jax.ShapeDtypeStruct((B, H, ML_DV, ML_DQK), F32),
                   jax.ShapeDtypeStruct((B, H, 1, ML_DQK), F32),
                   jax.ShapeDtypeStruct((B, H, 1, LANES), F32)],
        compiler_params=_cparams(("parallel", "arbitrary")),
        name="mlstm",
    )(z_all, z_all, z_all, z_all, z_all, bias_row, head_g.reshape(1, -1), C0, n0.reshape(B, H, 1, ML_DQK),
      jnp.broadcast_to(m0[:, :, None, None], (B, H, 1, LANES)))
    return om, C, n.reshape(B, H, ML_DQK), m[:, :, 0, 0]


def _rope_lanes(x, cos, sin):
    swapped = pltpu.roll(x, 32, 1) + pltpu.roll(x, 96, 1)
    return x * cos + swapped * sin


def _mla_pre_kernel(qr_ref, lat_ref, kr_ref, cos_ref, sin_ref, g_ref, qro_ref, lato_ref, kro_ref):
    cos = cos_ref[...]
    sin = sin_ref[...]
    for h in range(MLA_HEADS):
        sl = slice(h * LANES, (h + 1) * LANES)
        qro_ref[:, sl] = (_rope_lanes(qr_ref[:, sl], cos, sin) * MLA_SCALE).astype(qro_ref.dtype)
    kro_ref[...] = _rope_lanes(kr_ref[...], cos, sin)
    lat = lat_ref[...]
    lato_ref[...] = lat * lax.rsqrt(jnp.mean(lat * lat, axis=-1, keepdims=True) + NORM_EPS) * g_ref[...]


def mla_pre(rows, z_all, cos_tab, sin_tab, kv_norm_g):
    tm = min(rows.tm, 256)
    M = rows.M
    zspec = lambda w, off: pl.BlockSpec((tm, w), lambda i: (i, off // w))
    W = MLA_HEADS * LANES
    return pl.pallas_call(
        _mla_pre_kernel,
        grid=(M // tm,),
        in_specs=[zspec(W, OFF_QR), zspec(MLA_LATENT, OFF_LAT), zspec(LANES, OFF_KR),
                  pl.BlockSpec((tm, LANES), lambda i: (i, 0)), pl.BlockSpec((tm, LANES), lambda i: (i, 0)),
                  pl.BlockSpec((1, MLA_LATENT), lambda i: (0, 0))],
        out_specs=[pl.BlockSpec((tm, W), lambda i: (i, 0)), pl.BlockSpec((tm, MLA_LATENT), lambda i: (i, 0)),
                   pl.BlockSpec((tm, LANES), lambda i: (i, 0))],
        out_shape=[jax.ShapeDtypeStruct((M, W), BF16), jax.ShapeDtypeStruct((M, MLA_LATENT), F32),
                   jax.ShapeDtypeStruct((M, LANES), F32)],
        compiler_params=_cparams(("parallel",)),
        name="mla_pre",
    )(z_all, z_all, z_all, cos_tab, sin_tab, kv_norm_g.reshape(1, -1))


def _attn_kernel(qn_ref, qr_ref, kn_ref, kr_ref, v_ref, o_ref, m_sc, l_sc, acc_sc, *, tq, tk, past, nk):
    qi = pl.program_id(2)
    q = jnp.concatenate([(qn_ref[...] * MLA_SCALE).astype(BF16), qr_ref[...]], axis=1)
    m_sc[...] = jnp.full_like(m_sc, -jnp.inf)
    l_sc[...] = jnp.zeros_like(l_sc)
    acc_sc[...] = jnp.zeros_like(acc_sc)
    q0 = past + qi * tq
    n_full = jnp.minimum(((q0 // CHUNK + 1) * CHUNK) // tk, nk)
    n_vis = jnp.minimum((((q0 + tq - 1) // CHUNK + 1) * CHUNK + tk - 1) // tk, nk)

    def tile(kj, masked):
        k0 = pl.multiple_of(kj * tk, tk)
        k = jnp.concatenate([kn_ref[pl.ds(k0, tk), :], kr_ref[pl.ds(k0, tk), :]], axis=1)
        s = _dot_nt(q, k)
        if masked:
            q_chunk = (q0 + lax.broadcasted_iota(jnp.int32, (tq, 1), 0)) // CHUNK
            k_chunk = (k0 + lax.broadcasted_iota(jnp.int32, (1, tk), 1)) // CHUNK
            s = jnp.where(k_chunk <= q_chunk, s, -jnp.inf)
        m_prev = m_sc[...]
        m_new = jnp.maximum(m_prev, jnp.max(s, axis=1, keepdims=True))
        alpha = jnp.exp(m_prev - m_new)
        p = jnp.exp(s - m_new)
        l_sc[...] = alpha * l_sc[...] + jnp.sum(p, axis=1, keepdims=True)
        acc_sc[...] = alpha * acc_sc[...] + _dot(p.astype(BF16), v_ref[pl.ds(k0, tk), :])
        m_sc[...] = m_new

    def full_tile(kj, c):
        tile(kj, False)
        return c

    def edge_tile(kj, c):
        tile(kj, True)
        return c

    lax.fori_loop(0, n_full, full_tile, 0)
    lax.fori_loop(n_full, n_vis, edge_tile, 0)
    o_ref[...] = (acc_sc[...] / l_sc[...]).astype(o_ref.dtype)


def mla_attention(B, T, Sk, past, z_all, qr, kv, kr_all):
    H = MLA_HEADS
    tq = min(T, 512)
    tk = next((t for t in (2048, 1024, 512) if Sk % t == 0), Sk)
    nq, nk = T // tq, Sk // tk
    z3 = z_all.reshape(B, T, NZ)
    qr3 = qr.reshape(B, T, H * LANES)
    kv3 = kv.reshape(B, Sk, H * 2 * LANES)
    kr3 = kr_all.reshape(B, Sk, LANES)
    out = pl.pallas_call(
        functools.partial(_attn_kernel, tq=tq, tk=tk, past=past, nk=nk),
        grid=(B, H, nq),
        in_specs=[pl.BlockSpec((None, tq, LANES), lambda b, h, qi: (b, qi, OFF_QN // LANES + h)),
                  pl.BlockSpec((None, tq, LANES), lambda b, h, qi: (b, qi, h)),
                  pl.BlockSpec((None, Sk, LANES), lambda b, h, qi: (b, 0, 2 * h)),
                  pl.BlockSpec((None, Sk, LANES), lambda b, h, qi: (b, 0, 0)),
                  pl.BlockSpec((None, Sk, LANES), lambda b, h, qi: (b, 0, 2 * h + 1))],
        out_specs=pl.BlockSpec((None, tq, LANES), lambda b, h, qi: (b, qi, h)),
        out_shape=jax.ShapeDtypeStruct((B, T, H * MLA_V), BF16),
        scratch_shapes=[pltpu.VMEM((tq, 1), F32), pltpu.VMEM((tq, 1), F32), pltpu.VMEM((tq, MLA_V), F32)],
        compiler_params=_cparams(("parallel", "parallel", "arbitrary")),
        name="mla_attention",
    )(z3, qr3, kv3, kr3, kv3)
    return out.reshape(B * T, H * MLA_V)


def _pair_sum_matrix():
    r = lax.broadcasted_iota(jnp.int32, (LANES, LANES), 0) // RW_HEAD
    c = lax.broadcasted_iota(jnp.int32, (LANES, LANES), 1) // RW_HEAD
    return (r == c).astype(BF16)


def _head_sum(x, bd):
    m = x.shape[0]
    parts = []
    for p in range(RW_WIDTH // LANES):
        hi, lo = _split_bf16(x[:, p * LANES:(p + 1) * LANES])
        both = _dot(jnp.concatenate([hi, lo], axis=0), bd)
        parts.append(both[:m] + both[m:])
    return jnp.concatenate(parts, axis=1)


def _rwkv_pre_kernel(r_ref, k_ref, v_ref, lo_ref, tr_ref, tk_ref, tv_ref, tlo_ref, fr_ref, fk_ref, fv_ref, flo_ref,
                     mu_ref, mulo_ref, w0_ref, w2_ref, a0_ref, a2_ref, g2_ref, kk_ref, ka_ref, rk_ref,
                     ro_ref, wo_ref, ko_ref, vo_ref, ao_ref, bo_ref, go_ref, bonus_ref, *, T):
    W = RW_WIDTH
    tm = r_ref.shape[0]
    row = lax.broadcasted_iota(jnp.int32, (tm, 1), 0)
    stream_start = (pl.program_id(0) * tm + row) % T == 0

    def mix(z_ref, tail_ref, first_ref, mu):
        z = z_ref[...]
        prev = jnp.where(row == 0, tail_ref[7:8, :], pltpu.roll(z, 1, 0))
        prev = jnp.where(stream_start, first_ref[...], prev)
        return z + (prev - z) * mu

    r = mix(r_ref, tr_ref, fr_ref, mu_ref[:, 0:W])
    k = mix(k_ref, tk_ref, fk_ref, mu_ref[:, W:2 * W])
    v = mix(v_ref, tv_ref, fv_ref, mu_ref[:, 2 * W:3 * W])
    lo = mix(lo_ref, tlo_ref, flo_ref, mulo_ref[...])
    wd, ad, gd = lo[:, 0:64], lo[:, 64:128], lo[:, 128:256]
    hdot = lambda x, w: _mm(x, w, "nn", 3)
    wpre = w0_ref[...] + hdot(jnp.tanh(wd), w2_ref[...])
    w_raw = jnp.minimum(wpre, 0.0) - jnp.log(1.0 + jnp.exp(-jnp.abs(wpre))) - 0.5
    log_decay = -jnp.exp(w_raw)
    a = _sigmoid(a0_ref[...] + hdot(ad, a2_ref[...]))
    g = hdot(_sigmoid(gd), g2_ref[...])
    bd = _pair_sum_matrix()
    kk = k * kk_ref[...]
    kk = kk / jnp.maximum(jnp.sqrt(_head_sum(kk * kk, bd)), 1e-12)
    k = k * (1.0 + (a - 1.0) * ka_ref[...])
    ro_ref[...] = r
    wo_ref[...] = log_decay
    ko_ref[...] = k
    vo_ref[...] = v
    ao_ref[...] = -kk
    bo_ref[...] = kk * a
    go_ref[...] = g
    bonus_ref[...] = _head_sum(r * k * rk_ref[...], bd) * v


def rwkv_pre(rows, z_all, shift0, mu, w0, w2, a0, a2, g2, k_k, k_a, r_k):
    M = rows.M
    tm = min(rows.tm, 256)
    W = RW_WIDTH
    zs = lambda w, off: pl.BlockSpec((tm, w), lambda i: (i, off // w))
    tail = lambda w, off: pl.BlockSpec((8, w), lambda i: (jnp.maximum(i * (tm // 8) - 1, 0), off // w))
    if rows.R == 1:
        first = lambda w: pl.BlockSpec((None, 1, w), lambda i: (i // (rows.T // tm), 0, 0))
    else:
        first = lambda w: pl.BlockSpec((None, tm, w), lambda i: (0, i, 0))
    full = lambda a: pl.BlockSpec(a.shape, lambda i: (0,) * a.ndim)
    row = lambda a: a.reshape(1, -1)
    firsts = [rows.expand(shift0[:, o:o + w]) for o, w in ((0, W), (W, W), (2 * W, W), (3 * W, RW_LORA))]
    params = [row(mu[:3 * W]), row(mu[3 * W:]), row(w0), w2, row(a0), a2, g2, row(k_k), row(k_a), row(r_k)]
    cols = ((W, OFF_RW_R), (W, OFF_RW_K), (W, OFF_RW_V), (RW_LORA, OFF_RW_LORA))
    outs = pl.pallas_call(
        functools.partial(_rwkv_pre_kernel, T=rows.T),
        grid=(M // tm,),
        in_specs=[zs(*c) for c in cols] + [tail(*c) for c in cols] + [first(c[0]) for c in cols]
        + [full(p) for p in params],
        out_specs=[pl.BlockSpec((tm, W), lambda i: (i, 0))] * 8,
        out_shape=[jax.ShapeDtypeStruct((M, W), F32)] * 8,
        compiler_params=_cparams(("parallel",)),
        name="rwkv_pre",
    )(*([z_all] * 8), *firsts, *params)
    return outs


_DIMS = {"nn": (((1,), (0,)), ((), ())), "nt": (((1,), (1,)), ((), ())), "tn": (((0,), (0,)), ((), ()))}


def _split_bf16(x):
    hi = x.astype(BF16)
    return hi, (x - hi.astype(F32)).astype(BF16)


def _mm(a, b, kind, passes):
    dims = _DIMS[kind]
    if passes == 6:
        return lax.dot_general(a, b, dims, precision=HIGHEST, preferred_element_type=F32)
    dg = lambda x, y: lax.dot_general(x, y, dims, preferred_element_type=F32)
    if passes == 1:
        return dg(a.astype(BF16), b.astype(BF16))
    ah, al = _split_bf16(a)
    bh, bl = _split_bf16(b)
    m = a.shape[0]
    if kind == "tn" or m < RW_HEAD:
        return dg(ah, bh) + (dg(al, bh) + dg(ah, bl))
    both = dg(jnp.concatenate([ah, al], axis=0), bh)
    return both[:m] + (both[m:] + dg(ah, bl))


def _rwkv_chunk_kernel(r_ref, lw_ref, k_ref, v_ref, a_ref, b_ref, S0_ref, y_ref, S_ref, *, L, P_G, P_T, P_X, P_S):
    c = pl.program_id(1)

    @pl.when(c == 0)
    def _():
        S_ref[...] = S0_ref[...]

    N = RW_HEAD
    row = lax.broadcasted_iota(jnp.int32, (L, L), 0)
    col = lax.broadcasted_iota(jnp.int32, (L, L), 1)
    incl = row >= col
    strict = row > col
    eye = (row == col).astype(F32)
    tri = incl.astype(F32)
    cum_all = jnp.dot(tri, lw_ref[...], precision=HIGHEST, preferred_element_type=F32)
    sls = [slice(h * N, (h + 1) * N) for h in range(RW_HEADS)]
    each = lambda f, *lists: [f(*xs) for xs in zip(*lists)]
    S0 = [S_ref[0, h] for h in range(RW_HEADS)]
    lw = [lw_ref[:, s] for s in sls]
    cum = [cum_all[:, s] for s in sls]
    V = [v_ref[:, s] for s in sls]
    g_in = each(jnp.exp, cum)
    g_inv = each(lambda c: jnp.exp(-c), cum)
    At = each(lambda s, c, w: a_ref[:, s] * jnp.exp(c - w), sls, cum, lw)
    Bt = each(lambda s, g: b_ref[:, s] * g, sls, g_inv)
    Kt = each(lambda s, g: k_ref[:, s] * g, sls, g_inv)
    Rt = each(lambda s, g: r_ref[:, s] * g, sls, g_in)
    right = each(lambda x, y: jnp.concatenate([x, y], axis=0), Bt, Kt)
    G = each(lambda x, y, rt: _mm(jnp.concatenate([x, y], axis=0), rt, "nt", P_G), At, Rt, right)
    Nab = each(lambda g: jnp.where(strict, g[:L, :L], 0.0), G)
    Nak = each(lambda g: jnp.where(strict, g[:L, L:], 0.0), G)
    Mrb = each(lambda g: jnp.where(incl, g[L:, :L], 0.0), G)
    Mrk = each(lambda g: jnp.where(incl, g[L:, L:], 0.0), G)
    T = each(lambda n: eye + jnp.where(row // 2 == col // 2, n, 0.0), Nab)
    s = 2
    while s < L:
        off = (row // (2 * s) == col // (2 * s)) & (row // s != col // s)
        T = each(lambda t, n: t + _mm(t, _mm(jnp.where(off, n, 0.0), t, "nn", P_T), "nn", P_T), T, Nab)
        s *= 2
    NV = each(lambda n, v: _mm(n, v, "nn", P_X), Nak, V)
    TX = each(lambda t, x, nv: _mm(t, jnp.concatenate([x, nv], axis=1), "nn", P_X), T, At, NV)
    MX = each(lambda m, tx: _mm(m, tx, "nn", P_X), Mrb, TX)
    MV = each(lambda m, v: _mm(m, v, "nn", P_X), Mrk, V)
    Y = each(lambda rt, mx, mv, s0: _mm(rt + mx[:, :N], s0, "nt", P_S) + (mx[:, N:] + mv), Rt, MX, MV, S0)
    U = each(lambda tx, s0: _mm(tx[:, :N], s0, "nt", P_S) + tx[:, N:], TX, S0)
    upd = each(lambda u, v, bt, kt: _mm(u, bt, "tn", P_S) + _mm(v, kt, "tn", P_S), U, V, Bt, Kt)
    for h, s in enumerate(sls):
        y_ref[:, s] = Y[h]
        S_ref[0, h] = (S0[h] + upd[h]) * g_in[h][L - 1:L, :]


def rwkv_scan(B, T, r, lw, k, v, a, b, S0):
    L = CHUNK if T % CHUNK == 0 else T
    assert L & (L - 1) == 0
    nc = T // L
    xs = pl.BlockSpec((L, RW_WIDTH), lambda bi, c: (bi * nc + c, 0))
    ss = pl.BlockSpec((1, RW_HEADS, RW_HEAD, RW_HEAD), lambda bi, c: (bi, 0, 0, 0))
    return pl.pallas_call(
        functools.partial(_rwkv_chunk_kernel, L=L, P_G=3, P_T=3, P_X=3, P_S=3),
        grid=(B, nc),
        in_specs=[xs] * 6 + [ss],
        out_specs=[xs, ss],
        out_shape=[jax.ShapeDtypeStruct((B * T, RW_WIDTH), F32),
                   jax.ShapeDtypeStruct((B, RW_HEADS, RW_HEAD, RW_HEAD), F32)],
        compiler_params=_cparams(("parallel", "arbitrary")),
        name="rwkv_scan",
    )(r, lw, k, v, a, b, S0)


def _rwkv_post_kernel(y_ref, g_ref, bonus_ref, lg_ref, lb_ref, o_ref):
    bd = _pair_sum_matrix()
    y = y_ref[...]
    mean = _head_sum(y, bd) * (1.0 / RW_HEAD)
    d = y - mean
    var = _head_sum(d * d, bd) * (1.0 / RW_HEAD)
    yn = d * lax.rsqrt(var + RW_LN_EPS) * lg_ref[...] + lb_ref[...]
    o_ref[...] = ((yn + bonus_ref[...]) * g_ref[...]).astype(o_ref.dtype)


def rwkv_post(rows, y, g, bonus, lnx_g, lnx_b):
    M = rows.M
    tm = min(rows.tm, 256)
    W = RW_WIDTH
    xs = pl.BlockSpec((tm, W), lambda i: (i, 0))
    ps = pl.BlockSpec((1, W), lambda i: (0, 0))
    return pl.pallas_call(
        _rwkv_post_kernel,
        grid=(M // tm,),
        in_specs=[xs, xs, xs, ps, ps],
        out_specs=xs,
        out_shape=jax.ShapeDtypeStruct((M, W), BF16),
        compiler_params=_cparams(("parallel",)),
        name="rwkv_post",
    )(y, g, bonus, lnx_g.reshape(1, W), lnx_b.reshape(1, W))


def _prep_w_in(w):
    D = w.shape[0]
    o = 0
    take = lambda n: (w[:, o:o + n], o + n)
    ml_q, o = take(512)
    ml_k, o = take(512)
    ml_v, o = take(1024)
    ml_o, o = take(1024)
    ml_i, o = take(4)
    ml_f, o = take(4)
    q, o = take(MLA_HEADS * (MLA_NOPE + MLA_ROPE))
    lat, o = take(MLA_LATENT)
    kr, o = take(MLA_ROPE)
    rw_r, o = take(1024)
    rw_k, o = take(1024)
    rw_v, o = take(1024)
    rw_lo, o = take(RW_LORA)
    gate, o = take(3 * D_MODEL)
    assert o == w.shape[1]
    q = q.reshape(D, MLA_HEADS, MLA_NOPE + MLA_ROPE)
    qn = q[:, :, :MLA_NOPE].reshape(D, MLA_HEADS * MLA_NOPE)
    qr = jnp.pad(q[:, :, MLA_NOPE:], ((0, 0), (0, 0), (0, LANES - MLA_ROPE))).reshape(D, MLA_HEADS * LANES)
    kr = jnp.pad(kr, ((0, 0), (0, LANES - MLA_ROPE)))
    gif = jnp.pad(jnp.concatenate([ml_i, ml_f], axis=1), ((0, 0), (0, LANES - 2 * ML_HEADS)))
    out = jnp.concatenate([ml_q, ml_k, ml_v, ml_o, qn, qr, rw_r, rw_k, rw_v, gate, lat, rw_lo, kr, gif], axis=1)
    assert out.shape[1] == NZ
    return out.astype(BF16)


def _block_cols(w, tn):
    K, N = w.shape
    return w.reshape(K, N // tn, tn).transpose(1, 0, 2)


def _row_tile(M, cap):
    return max(t for t in range(8, cap + 1, 8) if M % t == 0)


def _rope_tables(T, past, reps):
    half = MLA_ROPE // 2
    inv = ROPE_BASE ** (-jnp.arange(half, dtype=F32) / half)
    ang = (past + jnp.arange(T)).astype(F32)[:, None] * inv[None, :]
    cos, sin = jnp.cos(ang), jnp.sin(ang)
    zeros = jnp.zeros((T, LANES - MLA_ROPE), F32)
    cos_t = jnp.concatenate([cos, cos, zeros], axis=1)
    sin_t = jnp.concatenate([-sin, sin, zeros], axis=1)
    return jnp.tile(cos_t, (reps, 1)), jnp.tile(sin_t, (reps, 1))


def _layer(rows, x, mod, l, W, P, lat_past, kr_past, C0, n0, m0, S0, shift0, tabs):
    B, T, M = rows.B, rows.T, rows.M
    D = D_MODEL
    m9 = [rows.expand(mod[:, i * D:(i + 1) * D]) for i in range(N_ADA)]
    sh1, sc1, g1, sh2, sc2, g2, sh3, sc3, g3 = m9
    h = norm_mod(rows, x, P['norm_g'][l, 0], sh1, sc1)
    x = mm_resid(rows, mm_swiglu(rows, h, W['ffn_in'][l][0]), W['ffn_out'][l][0], x, g1, 0.5)
    h = norm_mod(rows, x, P['norm_g'][l, 1], sh2, sc2)
    z_all = matmul(h, W['w_in'][l], rows.tm, name="mm_in")
    bias_row = jnp.pad(jnp.concatenate([P['mlstm_i_bias'][l], P['mlstm_f_bias'][l]]), (0, LANES - 2 * ML_HEADS))[None]
    om, C, n, m = mlstm(B, T, z_all, bias_row, P['mlstm_head_g'][l], C0, n0, m0)
    past = lat_past.shape[1]
    qr, lat, kr = mla_pre(rows, z_all, tabs[0], tabs[1], P['mla_kv_norm_g'][l])
    Sk = past + T
    if past:
        lat_all = jnp.concatenate([lat_past, lat.reshape(B, T, -1)], axis=1).reshape(B * Sk, -1)
        kr_all = jnp.concatenate([jnp.pad(kr_past, ((0, 0), (0, 0), (0, LANES - MLA_ROPE))), kr.reshape(B, T, -1)],
                                 axis=1).reshape(B * Sk, LANES)
    else:
        lat_all, kr_all = lat, kr
    kv = matmul(lat_all, W['kv_b'][l], _row_tile(B * Sk, 1536), out_dtype=BF16, name="mm_kv")
    oa = mla_attention(B, T, Sk, past, z_all, qr, kv, kr_all.astype(BF16))
    r_, w_, k_, v_, a_, b_, g_, bonus = rwkv_pre(rows, z_all, shift0, P['rwkv_mu'][l], P['rwkv_w0'][l], P['rwkv_w2'][l],
                                                 P['rwkv_a0'][l], P['rwkv_a2'][l], P['rwkv_g2'][l], P['rwkv_k_k'][l],
                                                 P['rwkv_k_a'][l], P['rwkv_r_k'][l].reshape(-1))
    y, S = rwkv_scan(B, T, r_, w_, k_, v_, a_, b_, S0)
    orw = rwkv_post(rows, y, g_, bonus, P['rwkv_lnx_g'][l], P['rwkv_lnx_b'][l])
    z_last = z_all.reshape(B, T, NZ)[:, -1]
    shift = jnp.concatenate([z_last[:, OFF_RW_R:OFF_RW_R + 3 * RW_WIDTH], z_last[:, OFF_RW_LORA:OFF_RW_LORA + RW_LORA]],
                            axis=1)
    mixed = mm_branch(rows, om, oa, orw, W['branch'][l], z_all, P['b_merge'][l][None])
    x = mm_resid(rows, mixed, W['w_out'][l], x, g2, 1.0)
    h = norm_mod(rows, x, P['norm_g'][l, 2], sh3, sc3)
    x = mm_resid(rows, mm_swiglu(rows, h, W['ffn_in'][l][1]), W['ffn_out'][l][1], x, g3, 0.5)
    new = (lat.reshape(B, T, MLA_LATENT), kr.reshape(B, T, LANES)[:, :, :MLA_ROPE], C, n, m, S, shift)
    return x, new


def _trunk(rows, x, mods, W, P, lat_cache, kr_cache, C_st, n_st, m_st, S_st, sh_st):
    depth = len(mods)
    past = lat_cache.shape[2]
    tabs = _rope_tables(rows.T, past, rows.B)
    x = x.reshape(rows.M, D_MODEL)
    new = []
    for l in range(depth):
        x, st = _layer(rows, x, mods[l], l, W, P, lat_cache[l], kr_cache[l], C_st[l], n_st[l], m_st[l], S_st[l],
                       sh_st[l], tabs)
        new.append(st)
    y = final_norm(rows, x, P['final_norm_g']).reshape(rows.B, rows.T, D_MODEL)
    return y, tuple(jnp.stack(s, axis=0) for s in zip(*new))


def kernel(x_prompt, x_sample, c_prompt, c_sample, cache_mla_latent, cache_mla_krope, state_mlstm_C, state_mlstm_n,
           state_mlstm_m, state_rwkv_S, state_rwkv_shift, ada_w, ada_b, norm_g, ffn_w_in, ffn_w_out, w_in,
           mlstm_i_bias, mlstm_f_bias, mlstm_head_g, mla_kv_norm_g, mla_w_kv_b, rwkv_mu, rwkv_w0, rwkv_w2, rwkv_a0,
           rwkv_a2, rwkv_g2, rwkv_k_k, rwkv_k_a, rwkv_r_k, rwkv_lnx_g, rwkv_lnx_b, w_branch, b_merge, w_out,
           final_norm_g):
    P = dict(norm_g=norm_g, mlstm_i_bias=mlstm_i_bias, mlstm_f_bias=mlstm_f_bias, mlstm_head_g=mlstm_head_g,
             mla_kv_norm_g=mla_kv_norm_g, rwkv_mu=rwkv_mu, rwkv_w0=rwkv_w0, rwkv_w2=rwkv_w2, rwkv_a0=rwkv_a0,
             rwkv_a2=rwkv_a2, rwkv_g2=rwkv_g2, rwkv_k_k=rwkv_k_k, rwkv_k_a=rwkv_k_a, rwkv_r_k=rwkv_r_k,
             rwkv_lnx_g=rwkv_lnx_g, rwkv_lnx_b=rwkv_lnx_b, b_merge=b_merge, final_norm_g=final_norm_g)
    depth = w_in.shape[0]
    blk = lambda w: _block_cols(w.astype(BF16), TN)
    W = dict(ffn_in=[[blk(ffn_w_in[l, i]) for i in range(2)] for l in range(depth)],
             ffn_out=[[blk(ffn_w_out[l, i]) for i in range(2)] for l in range(depth)],
             w_in=[_block_cols(_prep_w_in(w_in[l]), TN) for l in range(depth)],
             kv_b=[blk(mla_w_kv_b[l]) for l in range(depth)],
             branch=[jnp.concatenate([blk(w_branch[l, n]) for n in range(3)], axis=0) for l in range(depth)],
             w_out=[blk(w_out[l]) for l in range(depth)])
    Bp, Tp, _ = x_prompt.shape
    Bs, Ts, _ = x_sample.shape
    n_c = Bp + Bs
    c_all = jnp.pad(jnp.concatenate([c_prompt, c_sample], axis=0), ((0, (-n_c) % 8), (0, 0)))
    mods = [ada_mod(c_all, ada_w[l], ada_b[l]) for l in range(depth)]
    rows_p = Rows(Bp, Tp, TM_MAX)
    rows_s = Rows(Bs, Ts, TM_MAX)
    zeros = lambda *s: jnp.zeros((depth, Bp) + s, F32)
    y_p, st_p = _trunk(rows_p, x_prompt, [m[:Bp] for m in mods], W, P,
                       zeros(0, MLA_LATENT), zeros(0, MLA_ROPE), zeros(ML_HEADS, ML_DV, ML_DQK), zeros(ML_HEADS, ML_DQK),
                       zeros(ML_HEADS), zeros(RW_HEADS, RW_HEAD, RW_HEAD), zeros(RW_COLS))
    y_s, st_s = _trunk(rows_s, x_sample, [m[Bp:n_c] for m in mods], W, P,
                       cache_mla_latent, cache_mla_krope, state_mlstm_C, state_mlstm_n, state_mlstm_m, state_rwkv_S,
                       state_rwkv_shift)
    return (y_p, y_s) + st_p + st_s
```

```python
import functools

import jax
import jax.numpy as jnp
import numpy as np
from jax import lax
from jax.experimental import pallas as pl
from jax.experimental.pallas import tpu as pltpu

F32 = jnp.float32
BF16 = jnp.bfloat16
HIGHEST = lax.Precision.HIGHEST

D_MODEL = 2048
CHUNK = 64
NORM_EPS = 1e-6
D_FF = 5632
N_ADA = 9
ML_HEADS, ML_DQK, ML_DV, ML_GATE_CAP = 4, 128, 256, 15.0
ML_CHUNK = 256
MLA_HEADS, MLA_NOPE, MLA_ROPE, MLA_V, MLA_LATENT = 8, 128, 64, 128, 512
MLA_SCALE = (MLA_NOPE + MLA_ROPE) ** -0.5
LATENT_ATTN_MAX_T = 64
ROPE_BASE = 10000.0
RW_HEADS, RW_HEAD = 16, 64
RW_WIDTH = RW_HEADS * RW_HEAD
RW_LN_EPS = 64e-5
RW_LORA = 256
RW_COLS = 3 * RW_WIDTH + RW_LORA
BRANCH_WIDTH = 1024

LANES = 128
VMEM_LIMIT = 56 * 1024 * 1024
TM_MAX = 1024
TN = 512

OFF_ML_Q, OFF_ML_K, OFF_ML_V, OFF_ML_O = 0, 512, 1024, 2048
OFF_QN, OFF_QR = 3072, 4096
OFF_RW_R, OFF_RW_K, OFF_RW_V = 5120, 6144, 7168
OFF_GATE = 8192
OFF_LAT = 14336
OFF_RW_LORA = 14848
OFF_KR = 15104
OFF_IF = 15232
NZ = 15360
IN_COLS = 14664


def _cparams(sem):
    return pltpu.CompilerParams(dimension_semantics=sem, vmem_limit_bytes=VMEM_LIMIT)


def _sigmoid(x):
    return 1.0 / (1.0 + jnp.exp(-x))


def _silu(x):
    return x * _sigmoid(x)


def _dot(a, b):
    return jnp.dot(a, b, preferred_element_type=F32)


def _dot_nt(a, b, precision=None):
    return lax.dot_general(a, b, (((1,), (1,)), ((), ())), precision=precision, preferred_element_type=F32)


def _dot_tn(a, b, precision=None):
    return lax.dot_general(a, b, (((0,), (0,)), ((), ())), precision=precision, preferred_element_type=F32)


class Rows:
    def __init__(self, B, T, tm_max):
        self.B, self.T, self.M = B, T, B * T
        if T >= 256:
            self.tm = min(tm_max, T)
            assert T % self.tm == 0
            self.tpg = T // self.tm
            self.R = 1
        else:
            self.tm = self.M
            self.tpg = 1
            self.R = self.M
        self.nm = self.M // self.tm

    def expand(self, v):
        if self.R == 1:
            return v[:, None, :]
        return jnp.repeat(v, self.T, axis=0)[None]

    def vec_spec(self, tn, col=lambda j: j):
        return pl.BlockSpec((None, self.R, tn), lambda i, j: (i // self.tpg, 0, col(j)))


def _ada_kernel(c_ref, w_ref, b_ref, o_ref):
    c = c_ref[...]
    o_ref[...] = _dot(_silu(c).astype(BF16), w_ref[...].astype(BF16)) + b_ref[...]


def ada_mod(c, w, b):
    R, D = c.shape
    N = w.shape[1]
    tn = 1024
    return pl.pallas_call(
        _ada_kernel,
        grid=(N // tn,),
        in_specs=[pl.BlockSpec((R, D), lambda j: (0, 0)),
                  pl.BlockSpec((D, tn), lambda j: (0, j)),
                  pl.BlockSpec((1, tn), lambda j: (0, j))],
        out_specs=pl.BlockSpec((R, tn), lambda j: (0, j)),
        out_shape=jax.ShapeDtypeStruct((R, N), F32),
        compiler_params=_cparams(("arbitrary",)),
        name="ada_mod",
    )(c, w, b.reshape(1, N))


def _norm_mod_kernel(x_ref, g_ref, sh_ref, sc_ref, o_ref):
    x = x_ref[...]
    y = x * lax.rsqrt(jnp.mean(x * x, axis=-1, keepdims=True) + NORM_EPS) * g_ref[...]
    o_ref[...] = (y * (1.0 + sc_ref[...]) + sh_ref[...]).astype(o_ref.dtype)


def norm_mod(rows, x, g, shift, scale):
    D = x.shape[1]
    tm = min(rows.tm, 512)
    sub = rows.tm // tm
    vec = pl.BlockSpec((None, rows.R if rows.R == 1 else tm, D),
                       (lambda i: (i // (rows.tpg * sub), 0, 0)) if rows.R == 1 else (lambda i: (0, i, 0)))
    return pl.pallas_call(
        _norm_mod_kernel,
        grid=(rows.M // tm,),
        in_specs=[pl.BlockSpec((tm, D), lambda i: (i, 0)),
                  pl.BlockSpec((1, D), lambda i: (0, 0)), vec, vec],
        out_specs=pl.BlockSpec((tm, D), lambda i: (i, 0)),
        out_shape=jax.ShapeDtypeStruct((rows.M, D), BF16),
        compiler_params=_cparams(("parallel",)),
        name="norm_mod",
    )(x, g.reshape(1, D), shift, scale)


def _final_norm_kernel(x_ref, g_ref, o_ref):
    x = x_ref[...]
    o_ref[...] = x * lax.rsqrt(jnp.mean(x * x, axis=-1, keepdims=True) + NORM_EPS) * g_ref[...]


def final_norm(rows, x, g):
    D = x.shape[1]
    tm = min(rows.tm, 512)
    return pl.pallas_call(
        _final_norm_kernel,
        grid=(rows.M // tm,),
        in_specs=[pl.BlockSpec((tm, D), lambda i: (i, 0)), pl.BlockSpec((1, D), lambda i: (0, 0))],
        out_specs=pl.BlockSpec((tm, D), lambda i: (i, 0)),
        out_shape=jax.ShapeDtypeStruct((rows.M, D), F32),
        compiler_params=_cparams(("parallel",)),
        name="final_norm",
    )(x, g.reshape(1, D))


def _mm_kernel(a_ref, w_ref, o_ref):
    o_ref[...] = _dot(a_ref[...].astype(BF16), w_ref[...]).astype(o_ref.dtype)


def matmul(a, w, tm, out_dtype=F32, name="matmul"):
    M, K = a.shape
    w, g = w
    _, nj, _, tn = w.shape
    N = nj * tn
    return pl.pallas_call(
        _mm_kernel,
        grid=(M // tm, nj),
        in_specs=[pl.BlockSpec((tm, K), lambda i, j: (i, 0)),
                  pl.BlockSpec((None, None, K, tn), lambda i, j: (g, j, 0, 0))],
        out_specs=pl.BlockSpec((tm, tn), lambda i, j: (i, j)),
        out_shape=jax.ShapeDtypeStruct((M, N), out_dtype),
        compiler_params=_cparams(("parallel", "arbitrary")),
        name=name,
    )(a, w)


def _swiglu_kernel(h_ref, wg_ref, wu_ref, o_ref):
    h = h_ref[...]
    g = _dot(h, wg_ref[...])
    u = _dot(h, wu_ref[...])
    o_ref[...] = (_silu(g) * u).astype(o_ref.dtype)


def mm_swiglu(rows, h, w_in):
    K = h.shape[1]
    w_in, g = w_in
    tn = w_in.shape[3]
    nj = D_FF // tn
    return pl.pallas_call(
        _swiglu_kernel,
        grid=(rows.nm, nj),
        in_specs=[pl.BlockSpec((rows.tm, K), lambda i, j: (i, 0)),
                  pl.BlockSpec((None, None, K, tn), lambda i, j: (g, j, 0, 0)),
                  pl.BlockSpec((None, None, K, tn), lambda i, j: (g, j + nj, 0, 0))],
        out_specs=pl.BlockSpec((rows.tm, tn), lambda i, j: (i, j)),
        out_shape=jax.ShapeDtypeStruct((rows.M, D_FF), BF16),
        compiler_params=_cparams(("parallel", "arbitrary")),
        name="mm_swiglu",
    )(h, w_in, w_in)


def _resid_kernel(a_ref, w_ref, x_ref, g_ref, o_ref, *, coef):
    o_ref[...] = x_ref[...] + (coef * g_ref[...]) * _dot(a_ref[...], w_ref[...])


def mm_resid(rows, a, w, x, gate, coef):
    K = a.shape[1]
    w, g = w
    _, nj, _, tn = w.shape
    N = nj * tn
    return pl.pallas_call(
        functools.partial(_resid_kernel, coef=coef),
        grid=(rows.nm, nj),
        in_specs=[pl.BlockSpec((rows.tm, K), lambda i, j: (i, 0)),
                  pl.BlockSpec((None, None, K, tn), lambda i, j: (g, j, 0, 0)),
                  pl.BlockSpec((rows.tm, tn), lambda i, j: (i, j)),
                  rows.vec_spec(tn)],
        out_specs=pl.BlockSpec((rows.tm, tn), lambda i, j: (i, j)),
        out_shape=jax.ShapeDtypeStruct((rows.M, N), F32),
        compiler_params=_cparams(("parallel", "arbitrary")),
        name="mm_resid",
    )(a, w, x, gate)


def _branch_kernel(om_ref, oa_ref, or_ref, w0_ref, w1_ref, w2_ref, z0_ref, z1_ref, z2_ref, b0_ref, b1_ref, b2_ref,
                   o_ref):
    acc = _sigmoid(z0_ref[...] + b0_ref[...]) * _dot(om_ref[...], w0_ref[...])
    acc += _sigmoid(z1_ref[...] + b1_ref[...]) * _dot(oa_ref[...], w1_ref[...])
    acc += _sigmoid(z2_ref[...] + b2_ref[...]) * _dot(or_ref[...], w2_ref[...])
    o_ref[...] = acc.astype(o_ref.dtype)


def mm_branch(rows, om, oa, orw, w_branch, z_all, b_merge):
    w_branch, g = w_branch
    tn = w_branch.shape[3]
    D = D_MODEL
    nj = D // tn
    a_spec = pl.BlockSpec((rows.tm, BRANCH_WIDTH), lambda i, j: (i, 0))
    w_specs = [pl.BlockSpec((None, None, BRANCH_WIDTH, tn), lambda i, j, n=n: (g + n, j, 0, 0)) for n in range(3)]
    z_specs = [pl.BlockSpec((rows.tm, tn), lambda i, j, n=n: (i, (OFF_GATE + n * D) // tn + j)) for n in range(3)]
    b_specs = [pl.BlockSpec((1, tn), lambda i, j, n=n: (0, n * D // tn + j)) for n in range(3)]
    return pl.pallas_call(
        _branch_kernel,
        grid=(rows.nm, nj),
        in_specs=[a_spec, a_spec, a_spec] + w_specs + z_specs + b_specs,
        out_specs=pl.BlockSpec((rows.tm, tn), lambda i, j: (i, j)),
        out_shape=jax.ShapeDtypeStruct((rows.M, D), BF16),
        compiler_params=_cparams(("parallel", "arbitrary")),
        name="mm_branch",
    )(om, oa, orw, w_branch, w_branch, w_branch, z_all, z_all, z_all, b_merge, b_merge, b_merge)


def _mlstm_kernel(q_ref, k_ref, v_ref, o_ref, if_ref, bias_ref, hg_ref, C0_ref, n0_ref, m0_ref,
                  om_ref, C_ref, n_ref, m_ref, *, L):
    c = pl.program_id(1)

    @pl.when(c == 0)
    def _():
        C_ref[...] = C0_ref[...]
        n_ref[...] = n0_ref[...]
        m_ref[...] = m0_ref[...]

    H = ML_HEADS
    sc = ML_GATE_CAP * jnp.tanh((if_ref[...] + bias_ref[...]) * (1.0 / ML_GATE_CAP))
    lf = jnp.minimum(sc, 0.0) - jnp.log(1.0 + jnp.exp(-jnp.abs(sc)))
    row = lax.broadcasted_iota(jnp.int32, (L, L), 0)
    col = lax.broadcasted_iota(jnp.int32, (L, L), 1)
    causal = row >= col
    bcum = jnp.dot(causal.astype(F32), lf, precision=HIGHEST, preferred_element_type=F32)
    sel = (lax.broadcasted_iota(jnp.int32, (8, LANES), 0) == lax.broadcasted_iota(jnp.int32, (8, LANES), 1)).astype(F32)
    scT = _dot_nt(sel, sc, HIGHEST)
    bT = _dot_nt(sel, bcum, HIGHEST)
    hs = range(H)
    each = lambda f, *lists: [f(*xs) for xs in zip(*lists)]
    qsl = [slice(h * ML_DQK, (h + 1) * ML_DQK) for h in hs]
    vsl = [slice(h * ML_DV, (h + 1) * ML_DV) for h in hs]
    b_col = [bcum[:, H + h:H + h + 1] for h in hs]
    b_row = [bT[H + h:H + h + 1, :] for h in hs]
    ig_row = [scT[h:h + 1, :] for h in hs]
    ig_col = [sc[:, h:h + 1] for h in hs]
    m_prev = [m_ref[0, h, :, 0:1] for h in hs]
    C = [C_ref[0, h] for h in hs]
    n = [n_ref[0, h] for h in hs]
    qf = [q_ref[:, s] for s in qsl]
    q = each(lambda x: x.astype(BF16), qf)
    kb = [(k_ref[:, s] * (ML_DQK ** -0.5)).astype(BF16) for s in qsl]
    v = [v_ref[:, s] for s in vsl]
    logD = each(lambda bc, br, ir: jnp.where(causal, bc - br + ir, -jnp.inf), b_col, b_row, ig_row)
    g = each(lambda bc, mp: bc + mp, b_col, m_prev)
    m_t = each(lambda gg, ld: jnp.maximum(gg, jnp.max(ld, axis=1, keepdims=True)), g, logD)
    Dm = each(lambda ld, mt: jnp.exp(ld - mt), logD, m_t)
    inter = each(lambda gg, mt: jnp.exp(gg - mt), g, m_t)
    qk = each(lambda qq, kk, d: _dot_nt(qq, kk) * d, q, kb, Dm)
    qC = each(lambda qq, cc: _dot_nt(qq, cc.astype(BF16)), q, C)
    num = each(lambda s, vv, it, x: _dot(s.astype(BF16), vv.astype(BF16)) + it * x, qk, v, inter, qC)
    qn = each(lambda x, nn: jnp.sum(x * nn, axis=1, keepdims=True), qf, n)
    den = each(lambda s, it, x: jnp.sum(s, axis=1, keepdims=True) + it * x, qk, inter, qn)
    hh = each(lambda nu, de, mt: nu / jnp.maximum(jnp.abs(de), jnp.exp(-mt)), num, den, m_t)
    m_new = each(lambda mt: mt[L - 1:L, :], m_t)
    b_last = each(lambda bc: bc[L - 1:L, :], b_col)
    carry_w = each(lambda bl, mp, mn: jnp.exp(bl + mp - mn), b_last, m_prev, m_new)
    w_s = each(lambda bl, bc, ic, mn: jnp.exp(bl - bc + ic - mn), b_last, b_col, ig_col, m_new)
    dC = each(lambda w, vv, kk: _dot_tn((w * vv).astype(BF16), kk), w_s, v, kb)
    dn = each(lambda w, kk: jnp.sum(w * kk.astype(F32), axis=0, keepdims=True), w_s, kb)
    for h in hs:
        y = hh[h] * lax.rsqrt(jnp.mean(hh[h] * hh[h], axis=-1, keepdims=True) + NORM_EPS) * hg_ref[:, vsl[h]]
        om_ref[:, vsl[h]] = (_sigmoid(o_ref[:, vsl[h]]) * y).astype(om_ref.dtype)
        C_ref[0, h] = carry_w[h] * C[h] + dC[h]
        n_ref[0, h] = carry_w[h] * n[h] + dn[h]
        m_ref[0, h] = jnp.broadcast_to(m_new[h], (1, LANES))


def mlstm(B, T, z_all, bias_row, head_g, C0, n0, m0):
    L = next((c for c in (ML_CHUNK, CHUNK) if T % c == 0), T)
    nc = T // L
    H = ML_HEADS
    zrow = lambda w, off: pl.BlockSpec((L, w), lambda b, c: (b * nc + c, off // w))
    st = lambda *shape: pl.BlockSpec((1,) + shape, lambda b, c: (b,) + (0,) * len(shape))
    om, C, n, m = pl.pallas_call(
        functools.partial(_mlstm_kernel, L=L),
        grid=(B, nc),
        in_specs=[zrow(H * ML_DQK, OFF_ML_Q), zrow(H * ML_DQK, OFF_ML_K), zrow(H * ML_DV, OFF_ML_V),
                  zrow(H * ML_DV, OFF_ML_O), zrow(LANES, OFF_IF),
                  pl.BlockSpec((1, LANES), lambda b, c: (0, 0)),
                  pl.BlockSpec((1, H * ML_DV), lambda b, c: (0, 0)),
                  st(H, ML_DV, ML_DQK), st(H, 1, ML_DQK), st(H, 1, LANES)],
        out_specs=[pl.BlockSpec((L, H * ML_DV), lambda b, c: (b * nc + c, 0)),
                   st(H, ML_DV, ML_DQK), st(H, 1, ML_DQK), st(H, 1, LANES)],
        out_shape=[jax.ShapeDtypeStruct((B * T, H * ML_DV), BF16),
                   jax.ShapeDtypeStruct((B, H, ML_DV, ML_DQK), F32),
                   jax.ShapeDtypeStruct((B, H, 1, ML_DQK), F32),
                   jax.ShapeDtypeStruct((B, H, 1, LANES), F32)],
        compiler_params=_cparams(("parallel", "arbitrary")),
        name="mlstm",
    )(z_all, z_all, z_all, z_all, z_all, bias_row, head_g.reshape(1, -1), C0, n0.reshape(B, H, 1, ML_DQK),
      jnp.broadcast_to(m0[:, :, None, None], (B, H, 1, LANES)))
    return om, C, n.reshape(B, H, ML_DQK), m[:, :, 0, 0]


def _rope_lanes(x, cos, sin):
    swapped = pltpu.roll(x, 32, 1) + pltpu.roll(x, 96, 1)
    return x * cos + swapped * sin


def _mla_pre_kernel(qr_ref, lat_ref, kr_ref, cos_ref, sin_ref, g_ref, qro_ref, lato_ref, kro_ref):
    cos = cos_ref[...]
    sin = sin_ref[...]
    for h in range(MLA_HEADS):
        sl = slice(h * LANES, (h + 1) * LANES)
        qro_ref[:, sl] = (_rope_lanes(qr_ref[:, sl], cos, sin) * MLA_SCALE).astype(qro_ref.dtype)
    kro_ref[...] = _rope_lanes(kr_ref[...], cos, sin)
    lat = lat_ref[...]
    lato_ref[...] = lat * lax.rsqrt(jnp.mean(lat * lat, axis=-1, keepdims=True) + NORM_EPS) * g_ref[...]


def mla_pre(rows, z_all, cos_tab, sin_tab, kv_norm_g):
    tm = min(rows.tm, 256)
    M = rows.M
    zspec = lambda w, off: pl.BlockSpec((tm, w), lambda i: (i, off // w))
    W = MLA_HEADS * LANES
    return pl.pallas_call(
        _mla_pre_kernel,
        grid=(M // tm,),
        in_specs=[zspec(W, OFF_QR), zspec(MLA_LATENT, OFF_LAT), zspec(LANES, OFF_KR),
                  pl.BlockSpec((tm, LANES), lambda i: (i, 0)), pl.BlockSpec((tm, LANES), lambda i: (i, 0)),
                  pl.BlockSpec((1, MLA_LATENT), lambda i: (0, 0))],
        out_specs=[pl.BlockSpec((tm, W), lambda i: (i, 0)), pl.BlockSpec((tm, MLA_LATENT), lambda i: (i, 0)),
                   pl.BlockSpec((tm, LANES), lambda i: (i, 0))],
        out_shape=[jax.ShapeDtypeStruct((M, W), BF16), jax.ShapeDtypeStruct((M, MLA_LATENT), F32),
                   jax.ShapeDtypeStruct((M, LANES), F32)],
        compiler_params=_cparams(("parallel",)),
        name="mla_pre",
    )(z_all, z_all, z_all, cos_tab, sin_tab, kv_norm_g.reshape(1, -1))


def _attn_kernel(qn_ref, qr_ref, kn_ref, kr_ref, v_ref, o_ref, m_sc, l_sc, acc_sc, *, tq, tk, past, nk):
    qi = pl.program_id(2)
    q = jnp.concatenate([(qn_ref[...] * MLA_SCALE).astype(BF16), qr_ref[...]], axis=1)
    m_sc[...] = jnp.full_like(m_sc, -jnp.inf)
    l_sc[...] = jnp.zeros_like(l_sc)
    acc_sc[...] = jnp.zeros_like(acc_sc)
    q0 = past + qi * tq
    n_full = jnp.minimum(((q0 // CHUNK + 1) * CHUNK) // tk, nk)
    n_vis = jnp.minimum((((q0 + tq - 1) // CHUNK + 1) * CHUNK + tk - 1) // tk, nk)

    def tile(kj, masked):
        k0 = pl.multiple_of(kj * tk, tk)
        k = jnp.concatenate([kn_ref[pl.ds(k0, tk), :], kr_ref[pl.ds(k0, tk), :]], axis=1)
        s = _dot_nt(q, k)
        if masked:
            q_chunk = (q0 + lax.broadcasted_iota(jnp.int32, (tq, 1), 0)) // CHUNK
            k_chunk = (k0 + lax.broadcasted_iota(jnp.int32, (1, tk), 1)) // CHUNK
            s = jnp.where(k_chunk <= q_chunk, s, -jnp.inf)
        m_prev = m_sc[...]
        m_new = jnp.maximum(m_prev, jnp.max(s, axis=1, keepdims=True))
        alpha = jnp.exp(m_prev - m_new)
        p = jnp.exp(s - m_new)
        l_sc[...] = alpha * l_sc[...] + jnp.sum(p, axis=1, keepdims=True)
        acc_sc[...] = alpha * acc_sc[...] + _dot(p.astype(BF16), v_ref[pl.ds(k0, tk), :])
        m_sc[...] = m_new

    def full_tile(kj, c):
        tile(kj, False)
        return c

    def edge_tile(kj, c):
        tile(kj, True)
        return c

    lax.fori_loop(0, n_full, full_tile, 0)
    lax.fori_loop(n_full, n_vis, edge_tile, 0)
    o_ref[...] = (acc_sc[...] / l_sc[...]).astype(o_ref.dtype)


def mla_attention(B, T, Sk, past, z_all, qr, kv, kr_all):
    H = MLA_HEADS
    tq = min(T, 512)
    tk = next((t for t in (2048, 1024, 512) if Sk % t == 0), Sk)
    nq, nk = T // tq, Sk // tk
    z3 = z_all.reshape(B, T, NZ)
    qr3 = qr.reshape(B, T, H * LANES)
    kv3 = kv.reshape(B, Sk, H * 2 * LANES)
    kr3 = kr_all.reshape(B, Sk, LANES)
    out = pl.pallas_call(
        functools.partial(_attn_kernel, tq=tq, tk=tk, past=past, nk=nk),
        grid=(B, H, nq),
        in_specs=[pl.BlockSpec((None, tq, LANES), lambda b, h, qi: (b, qi, OFF_QN // LANES + h)),
                  pl.BlockSpec((None, tq, LANES), lambda b, h, qi: (b, qi, h)),
                  pl.BlockSpec((None, Sk, LANES), lambda b, h, qi: (b, 0, 2 * h)),
                  pl.BlockSpec((None, Sk, LANES), lambda b, h, qi: (b, 0, 0)),
                  pl.BlockSpec((None, Sk, LANES), lambda b, h, qi: (b, 0, 2 * h + 1))],
        out_specs=pl.BlockSpec((None, tq, LANES), lambda b, h, qi: (b, qi, h)),
        out_shape=jax.ShapeDtypeStruct((B, T, H * MLA_V), BF16),
        scratch_shapes=[pltpu.VMEM((tq, 1), F32), pltpu.VMEM((tq, 1), F32), pltpu.VMEM((tq, MLA_V), F32)],
        compiler_params=_cparams(("parallel", "parallel", "arbitrary")),
        name="mla_attention",
    )(z3, qr3, kv3, kr3, kv3)
    return out.reshape(B * T, H * MLA_V)


def _attn_latent_kernel(qn_ref, qr_ref, lat_ref, kr_ref, w_ref, o_ref, *, T, Sk, past):
    lat = lat_ref[...]
    kr = kr_ref[...]
    q_chunk = (past + lax.broadcasted_iota(jnp.int32, (T, 1), 0)) // CHUNK
    k_chunk = lax.broadcasted_iota(jnp.int32, (1, Sk), 1) // CHUNK
    visible = k_chunk <= q_chunk
    per_block = w_ref.shape[2] // (MLA_NOPE + MLA_V)
    for h in range(MLA_HEADS):
        sl = slice(h * LANES, (h + 1) * LANES)
        c0 = (h % per_block) * (MLA_NOPE + MLA_V)
        wk = w_ref[h // per_block, :, c0:c0 + MLA_NOPE]
        wv = w_ref[h // per_block, :, c0 + MLA_NOPE:c0 + MLA_NOPE + MLA_V]
        qa = _dot_nt((qn_ref[:, sl] * MLA_SCALE).astype(BF16), wk).astype(BF16)
        s = _dot_nt(qa, lat) + _dot_nt(qr_ref[:, sl], kr)
        s = jnp.where(visible, s, -jnp.inf)
        p = jnp.exp(s - jnp.max(s, axis=1, keepdims=True))
        ctx = _dot(p.astype(BF16), lat) / jnp.sum(p, axis=1, keepdims=True)
        o_ref[:, sl] = _dot(ctx.astype(BF16), wv).astype(o_ref.dtype)


def mla_attention_latent(B, T, Sk, past, z_all, qr, lat_all, kr_all, w_kv):
    H = MLA_HEADS
    w, g = w_kv
    _, nj, K, tn = w.shape
    W = H * LANES
    out = pl.pallas_call(
        functools.partial(_attn_latent_kernel, T=T, Sk=Sk, past=past),
        grid=(B,),
        in_specs=[pl.BlockSpec((None, T, W), lambda b: (b, 0, OFF_QN // W)),
                  pl.BlockSpec((None, T, W), lambda b: (b, 0, 0)),
                  pl.BlockSpec((None, Sk, MLA_LATENT), lambda b: (b, 0, 0)),
                  pl.BlockSpec((None, Sk, LANES), lambda b: (b, 0, 0)),
                  pl.BlockSpec((None, nj, K, tn), lambda b: (g, 0, 0, 0))],
        out_specs=pl.BlockSpec((None, T, W), lambda b: (b, 0, 0)),
        out_shape=jax.ShapeDtypeStruct((B, T, H * MLA_V), BF16),
        compiler_params=_cparams(("parallel",)),
        name="mla_attention_latent",
    )(z_all.reshape(B, T, NZ), qr.reshape(B, T, W), lat_all.reshape(B, Sk, MLA_LATENT), kr_all.reshape(B, Sk, LANES), w)
    return out.reshape(B * T, H * MLA_V)


def _pair_sum_matrix():
    r = lax.broadcasted_iota(jnp.int32, (LANES, LANES), 0) // RW_HEAD
    c = lax.broadcasted_iota(jnp.int32, (LANES, LANES), 1) // RW_HEAD
    return (r == c).astype(BF16)


def _head_sum(x, bd):
    m = x.shape[0]
    parts = []
    for p in range(RW_WIDTH // LANES):
        hi, lo = _split_bf16(x[:, p * LANES:(p + 1) * LANES])
        both = _dot(jnp.concatenate([hi, lo], axis=0), bd)
        parts.append(both[:m] + both[m:])
    return jnp.concatenate(parts, axis=1)


def _rwkv_pre_kernel(r_ref, k_ref, v_ref, lo_ref, tr_ref, tk_ref, tv_ref, tlo_ref, fr_ref, fk_ref, fv_ref, flo_ref,
                     mu_ref, mulo_ref, w0_ref, w2_ref, a0_ref, a2_ref, g2_ref, kk_ref, ka_ref, rk_ref,
                     ro_ref, wo_ref, ko_ref, vo_ref, ao_ref, bo_ref, go_ref, bonus_ref, *, T):
    W = RW_WIDTH
    tm = r_ref.shape[0]
    row = lax.broadcasted_iota(jnp.int32, (tm, 1), 0)
    stream_start = (pl.program_id(0) * tm + row) % T == 0

    def mix(z_ref, tail_ref, first_ref, mu):
        z = z_ref[...]
        prev = jnp.where(row == 0, tail_ref[7:8, :], pltpu.roll(z, 1, 0))
        prev = jnp.where(stream_start, first_ref[...], prev)
        return z + (prev - z) * mu

    r = mix(r_ref, tr_ref, fr_ref, mu_ref[:, 0:W])
    k = mix(k_ref, tk_ref, fk_ref, mu_ref[:, W:2 * W])
    v = mix(v_ref, tv_ref, fv_ref, mu_ref[:, 2 * W:3 * W])
    lo = mix(lo_ref, tlo_ref, flo_ref, mulo_ref[...])
    wd, ad, gd = lo[:, 0:64], lo[:, 64:128], lo[:, 128:256]
    hdot = lambda x, w: _mm(x, w, "nn", 3)
    wpre = w0_ref[...] + hdot(jnp.tanh(wd), w2_ref[...])
    w_raw = jnp.minimum(wpre, 0.0) - jnp.log(1.0 + jnp.exp(-jnp.abs(wpre))) - 0.5
    log_decay = -jnp.exp(w_raw)
    a = _sigmoid(a0_ref[...] + hdot(ad, a2_ref[...]))
    g = hdot(_sigmoid(gd), g2_ref[...])
    bd = _pair_sum_matrix()
    kk = k * kk_ref[...]
    kk = kk / jnp.maximum(jnp.sqrt(_head_sum(kk * kk, bd)), 1e-12)
    k = k * (1.0 + (a - 1.0) * ka_ref[...])
    ro_ref[...] = r
    wo_ref[...] = log_decay
    ko_ref[...] = k
    vo_ref[...] = v
    ao_ref[...] = -kk
    bo_ref[...] = kk * a
    go_ref[...] = g
    bonus_ref[...] = _head_sum(r * k * rk_ref[...], bd) * v


def rwkv_pre(rows, z_all, shift0, mu, w0, w2, a0, a2, g2, k_k, k_a, r_k):
    M = rows.M
    tm = min(rows.tm, 256)
    W = RW_WIDTH
    zs = lambda w, off: pl.BlockSpec((tm, w), lambda i: (i, off // w))
    tail = lambda w, off: pl.BlockSpec((8, w), lambda i: (jnp.maximum(i * (tm // 8) - 1, 0), off // w))
    if rows.R == 1:
        first = lambda w: pl.BlockSpec((None, 1, w), lambda i: (i // (rows.T // tm), 0, 0))
    else:
        first = lambda w: pl.BlockSpec((None, tm, w), lambda i: (0, i, 0))
    full = lambda a: pl.BlockSpec(a.shape, lambda i: (0,) * a.ndim)
    row = lambda a: a.reshape(1, -1)
    firsts = [rows.expand(shift0[:, o:o + w]) for o, w in ((0, W), (W, W), (2 * W, W), (3 * W, RW_LORA))]
    params = [row(mu[:3 * W]), row(mu[3 * W:]), row(w0), w2, row(a0), a2, g2, row(k_k), row(k_a), row(r_k)]
    cols = ((W, OFF_RW_R), (W, OFF_RW_K), (W, OFF_RW_V), (RW_LORA, OFF_RW_LORA))
    outs = pl.pallas_call(
        functools.partial(_rwkv_pre_kernel, T=rows.T),
        grid=(M // tm,),
        in_specs=[zs(*c) for c in cols] + [tail(*c) for c in cols] + [first(c[0]) for c in cols]
        + [full(p) for p in params],
        out_specs=[pl.BlockSpec((tm, W), lambda i: (i, 0))] * 8,
        out_shape=[jax.ShapeDtypeStruct((M, W), F32)] * 8,
        compiler_params=_cparams(("parallel",)),
        name="rwkv_pre",
    )(*([z_all] * 8), *firsts, *params)
    return outs


_DIMS = {"nn": (((1,), (0,)), ((), ())), "nt": (((1,), (1,)), ((), ())), "tn": (((0,), (0,)), ((), ()))}


def _split_bf16(x):
    hi = x.astype(BF16)
    return hi, (x - hi.astype(F32)).astype(BF16)


def _mm(a, b, kind, passes):
    dims = _DIMS[kind]
    if passes == 6:
        return lax.dot_general(a, b, dims, precision=HIGHEST, preferred_element_type=F32)
    dg = lambda x, y: lax.dot_general(x, y, dims, preferred_element_type=F32)
    if passes == 1:
        return dg(a.astype(BF16), b.astype(BF16))
    ah, al = _split_bf16(a)
    bh, bl = _split_bf16(b)
    m = a.shape[0]
    if kind == "tn" or m < RW_HEAD:
        return dg(ah, bh) + (dg(al, bh) + dg(ah, bl))
    both = dg(jnp.concatenate([ah, al], axis=0), bh)
    return both[:m] + (both[m:] + dg(ah, bl))


def _rwkv_chunk_kernel(r_ref, lw_ref, k_ref, v_ref, a_ref, b_ref, S0_ref, y_ref, S_ref, *, L, P_G, P_T, P_X, P_S):
    c = pl.program_id(1)

    @pl.when(c == 0)
    def _():
        S_ref[...] = S0_ref[...]

    N = RW_HEAD
    row = lax.broadcasted_iota(jnp.int32, (L, L), 0)
    col = lax.broadcasted_iota(jnp.int32, (L, L), 1)
    incl = row >= col
    strict = row > col
    eye = (row == col).astype(F32)
    tri = incl.astype(F32)
    cum_all = jnp.dot(tri, lw_ref[...], precision=HIGHEST, preferred_element_type=F32)
    sls = [slice(h * N, (h + 1) * N) for h in range(RW_HEADS)]
    each = lambda f, *lists: [f(*xs) for xs in zip(*lists)]
    S0 = [S_ref[0, h] for h in range(RW_HEADS)]
    lw = [lw_ref[:, s] for s in sls]
    cum = [cum_all[:, s] for s in sls]
    V = [v_ref[:, s] for s in sls]
    g_in = each(jnp.exp, cum)
    g_inv = each(lambda c: jnp.exp(-c), cum)
    At = each(lambda s, c, w: a_ref[:, s] * jnp.exp(c - w), sls, cum, lw)
    Bt = each(lambda s, g: b_ref[:, s] * g, sls, g_inv)
    Kt = each(lambda s, g: k_ref[:, s] * g, sls, g_inv)
    Rt = each(lambda s, g: r_ref[:, s] * g, sls, g_in)
    right = each(lambda x, y: jnp.concatenate([x, y], axis=0), Bt, Kt)
    G = each(lambda x, y, rt: _mm(jnp.concatenate([x, y], axis=0), rt, "nt", P_G), At, Rt, right)
    Nab = each(lambda g: jnp.where(strict, g[:L, :L], 0.0), G)
    Nak = each(lambda g: jnp.where(strict, g[:L, L:], 0.0), G)
    Mrb = each(lambda g: jnp.where(incl, g[L:, :L], 0.0), G)
    Mrk = each(lambda g: jnp.where(incl, g[L:, L:], 0.0), G)
    T = each(lambda n: eye + jnp.where(row // 2 == col // 2, n, 0.0), Nab)
    s = 2
    while s < L:
        off = (row // (2 * s) == col // (2 * s)) & (row // s != col // s)
        T = each(lambda t, n: t + _mm(t, _mm(jnp.where(off, n, 0.0), t, "nn", P_T), "nn", P_T), T, Nab)
        s *= 2
    NV = each(lambda n, v: _mm(n, v, "nn", P_X), Nak, V)
    TX = each(lambda t, x, nv: _mm(t, jnp.concatenate([x, nv], axis=1), "nn", P_X), T, At, NV)
    MX = each(lambda m, tx: _mm(m, tx, "nn", P_X), Mrb, TX)
    MV = each(lambda m, v: _mm(m, v, "nn", P_X), Mrk, V)
    Y = each(lambda rt, mx, mv, s0: _mm(rt + mx[:, :N], s0, "nt", P_S) + (mx[:, N:] + mv), Rt, MX, MV, S0)
    U = each(lambda tx, s0: _mm(tx[:, :N], s0, "nt", P_S) + tx[:, N:], TX, S0)
    upd = each(lambda u, v, bt, kt: _mm(u, bt, "tn", P_S) + _mm(v, kt, "tn", P_S), U, V, Bt, Kt)
    for h, s in enumerate(sls):
        y_ref[:, s] = Y[h]
        S_ref[0, h] = (S0[h] + upd[h]) * g_in[h][L - 1:L, :]


def rwkv_scan(B, T, r, lw, k, v, a, b, S0):
    L = CHUNK if T % CHUNK == 0 else T
    assert L & (L - 1) == 0
    nc = T // L
    xs = pl.BlockSpec((L, RW_WIDTH), lambda bi, c: (bi * nc + c, 0))
    ss = pl.BlockSpec((1, RW_HEADS, RW_HEAD, RW_HEAD), lambda bi, c: (bi, 0, 0, 0))
    return pl.pallas_call(
        functools.partial(_rwkv_chunk_kernel, L=L, P_G=3, P_T=3, P_X=3, P_S=3),
        grid=(B, nc),
        in_specs=[xs] * 6 + [ss],
        out_specs=[xs, ss],
        out_shape=[jax.ShapeDtypeStruct((B * T, RW_WIDTH), F32),
                   jax.ShapeDtypeStruct((B, RW_HEADS, RW_HEAD, RW_HEAD), F32)],
        compiler_params=_cparams(("parallel", "arbitrary")),
        name="rwkv_scan",
    )(r, lw, k, v, a, b, S0)


def _rwkv_post_kernel(y_ref, g_ref, bonus_ref, lg_ref, lb_ref, o_ref):
    bd = _pair_sum_matrix()
    y = y_ref[...]
    mean = _head_sum(y, bd) * (1.0 / RW_HEAD)
    d = y - mean
    var = _head_sum(d * d, bd) * (1.0 / RW_HEAD)
    yn = d * lax.rsqrt(var + RW_LN_EPS) * lg_ref[...] + lb_ref[...]
    o_ref[...] = ((yn + bonus_ref[...]) * g_ref[...]).astype(o_ref.dtype)


def rwkv_post(rows, y, g, bonus, lnx_g, lnx_b):
    M = rows.M
    tm = min(rows.tm, 256)
    W = RW_WIDTH
    xs = pl.BlockSpec((tm, W), lambda i: (i, 0))
    ps = pl.BlockSpec((1, W), lambda i: (0, 0))
    return pl.pallas_call(
        _rwkv_post_kernel,
        grid=(M // tm,),
        in_specs=[xs, xs, xs, ps, ps],
        out_specs=xs,
        out_shape=jax.ShapeDtypeStruct((M, W), BF16),
        compiler_params=_cparams(("parallel",)),
        name="rwkv_post",
    )(y, g, bonus, lnx_g.reshape(1, W), lnx_b.reshape(1, W))


def _cast_blocks_kernel(x_ref, o_ref):
    o_ref[...] = x_ref[...].astype(o_ref.dtype)


def cast_blocks(w):
    G, K, N = w.shape
    tk = _row_tile(K, 1024)
    return pl.pallas_call(
        _cast_blocks_kernel,
        grid=(G, N // TN, K // tk),
        in_specs=[pl.BlockSpec((None, tk, TN), lambda g, j, k: (g, k, j))],
        out_specs=pl.BlockSpec((None, None, tk, TN), lambda g, j, k: (g, j, k, 0)),
        out_shape=jax.ShapeDtypeStruct((G, N // TN, K, TN), BF16),
        compiler_params=_cparams(("parallel", "parallel", "parallel")),
        name="cast_blocks",
    )(w)


def _w_in_segments():
    o_ml = 0
    o_if = 3072
    o_q = o_if + 2 * ML_HEADS
    o_lat = o_q + MLA_HEADS * (MLA_NOPE + MLA_ROPE)
    o_kr = o_lat + MLA_LATENT
    o_rw = o_kr + MLA_ROPE
    o_lo = o_rw + 3 * RW_WIDTH
    o_gate = o_lo + RW_LORA
    segs = [(OFF_ML_Q, o_ml, 3072), (OFF_RW_R, o_rw, 3 * RW_WIDTH), (OFF_GATE, o_gate, 3 * D_MODEL),
            (OFF_LAT, o_lat, MLA_LATENT), (OFF_RW_LORA, o_lo, RW_LORA),
            (OFF_KR, o_kr, MLA_ROPE), (OFF_KR + MLA_ROPE, None, LANES - MLA_ROPE),
            (OFF_IF, o_if, 2 * ML_HEADS), (OFF_IF + 2 * ML_HEADS, None, LANES - 2 * ML_HEADS)]
    for h in range(MLA_HEADS):
        src = o_q + h * (MLA_NOPE + MLA_ROPE)
        segs += [(OFF_QN + h * MLA_NOPE, src, MLA_NOPE), (OFF_QR + h * LANES, src + MLA_NOPE, MLA_ROPE),
                 (OFF_QR + h * LANES + MLA_ROPE, None, LANES - MLA_ROPE)]
    assert sum(s[2] for s in segs) == NZ and o_gate + 3 * D_MODEL == IN_COLS
    return segs


def _prep_w_in_kernel(x_ref, o_ref):
    rows = x_ref.shape[0]
    for dst, src, width in _w_in_segments():
        while width:
            j, off = divmod(dst, TN)
            n = min(width, TN - off)
            if src is None:
                o_ref[j, :, off:off + n] = jnp.zeros((rows, n), o_ref.dtype)
            else:
                o_ref[j, :, off:off + n] = x_ref[:, src:src + n].astype(o_ref.dtype)
                src += n
            dst, width = dst + n, width - n


def prep_w_in(w):
    G, K, N = w.shape
    tk = 128
    return pl.pallas_call(
        _prep_w_in_kernel,
        grid=(G, K // tk),
        in_specs=[pl.BlockSpec((None, tk, N), lambda g, k: (g, k, 0))],
        out_specs=pl.BlockSpec((None, NZ // TN, tk, TN), lambda g, k: (g, 0, k, 0)),
        out_shape=jax.ShapeDtypeStruct((G, NZ // TN, K, TN), BF16),
        compiler_params=_cparams(("parallel", "parallel")),
        name="prep_w_in",
    )(w)


def _row_tile(M, cap):
    return max(t for t in range(8, cap + 1, 8) if M % t == 0)


def _rope_tables(T, past, reps):
    half = MLA_ROPE // 2
    inv = ROPE_BASE ** (-jnp.arange(half, dtype=F32) / half)
    ang = (past + jnp.arange(T)).astype(F32)[:, None] * inv[None, :]
    cos, sin = jnp.cos(ang), jnp.sin(ang)
    zeros = jnp.zeros((T, LANES - MLA_ROPE), F32)
    cos_t = jnp.concatenate([cos, cos, zeros], axis=1)
    sin_t = jnp.concatenate([-sin, sin, zeros], axis=1)
    return jnp.tile(cos_t, (reps, 1)), jnp.tile(sin_t, (reps, 1))


def _layer(rows, x, mod, l, W, P, lat_past, kr_past, C0, n0, m0, S0, shift0, tabs):
    B, T, M = rows.B, rows.T, rows.M
    D = D_MODEL
    m9 = [rows.expand(mod[:, i * D:(i + 1) * D]) for i in range(N_ADA)]
    sh1, sc1, g1, sh2, sc2, g2, sh3, sc3, g3 = m9
    h = norm_mod(rows, x, P['norm_g'][l, 0], sh1, sc1)
    x = mm_resid(rows, mm_swiglu(rows, h, (W['ffn_in'], 2 * l)), (W['ffn_out'], 2 * l), x, g1, 0.5)
    h = norm_mod(rows, x, P['norm_g'][l, 1], sh2, sc2)
    z_all = matmul(h, (W['w_in'], l), rows.tm, name="mm_in")
    bias_row = jnp.pad(jnp.concatenate([P['mlstm_i_bias'][l], P['mlstm_f_bias'][l]]), (0, LANES - 2 * ML_HEADS))[None]
    om, C, n, m = mlstm(B, T, z_all, bias_row, P['mlstm_head_g'][l], C0, n0, m0)
    past = lat_past.shape[1]
    qr, lat, kr = mla_pre(rows, z_all, tabs[0], tabs[1], P['mla_kv_norm_g'][l])
    Sk = past + T
    if past:
        lat_all = jnp.concatenate([lat_past, lat.reshape(B, T, -1)], axis=1).reshape(B * Sk, -1)
        kr_all = jnp.concatenate([jnp.pad(kr_past, ((0, 0), (0, 0), (0, LANES - MLA_ROPE))), kr.reshape(B, T, -1)],
                                 axis=1).reshape(B * Sk, LANES)
    else:
        lat_all, kr_all = lat, kr
    if T <= LATENT_ATTN_MAX_T:
        oa = mla_attention_latent(B, T, Sk, past, z_all, qr, lat_all.astype(BF16), kr_all.astype(BF16), (W['kv_b'], l))
    else:
        kv = matmul(lat_all, (W['kv_b'], l), _row_tile(B * Sk, 1536), out_dtype=BF16, name="mm_kv")
        oa = mla_attention(B, T, Sk, past, z_all, qr, kv, kr_all.astype(BF16))
    r_, w_, k_, v_, a_, b_, g_, bonus = rwkv_pre(rows, z_all, shift0, P['rwkv_mu'][l], P['rwkv_w0'][l], P['rwkv_w2'][l],
                                                 P['rwkv_a0'][l], P['rwkv_a2'][l], P['rwkv_g2'][l], P['rwkv_k_k'][l],
                                                 P['rwkv_k_a'][l], P['rwkv_r_k'][l].reshape(-1))
    y, S = rwkv_scan(B, T, r_, w_, k_, v_, a_, b_, S0)
    orw = rwkv_post(rows, y, g_, bonus, P['rwkv_lnx_g'][l], P['rwkv_lnx_b'][l])
    z_last = z_all.reshape(B, T, NZ)[:, -1]
    shift = jnp.concatenate([z_last[:, OFF_RW_R:OFF_RW_R + 3 * RW_WIDTH], z_last[:, OFF_RW_LORA:OFF_RW_LORA + RW_LORA]],
                            axis=1)
    mixed = mm_branch(rows, om, oa, orw, (W['branch'], 3 * l), z_all, P['b_merge'][l][None])
    x = mm_resid(rows, mixed, (W['w_out'], l), x, g2, 1.0)
    h = norm_mod(rows, x, P['norm_g'][l, 2], sh3, sc3)
    x = mm_resid(rows, mm_swiglu(rows, h, (W['ffn_in'], 2 * l + 1)), (W['ffn_out'], 2 * l + 1), x, g3, 0.5)
    new = (lat.reshape(B, T, MLA_LATENT), kr.reshape(B, T, LANES)[:, :, :MLA_ROPE], C, n, m, S, shift)
    return x, new


def _trunk(rows, x, mods, W, P, lat_cache, kr_cache, C_st, n_st, m_st, S_st, sh_st):
    depth = len(mods)
    past = lat_cache.shape[2]
    tabs = _rope_tables(rows.T, past, rows.B)
    x = x.reshape(rows.M, D_MODEL)
    new = []
    for l in range(depth):
        x, st = _layer(rows, x, mods[l], l, W, P, lat_cache[l], kr_cache[l], C_st[l], n_st[l], m_st[l], S_st[l],
                       sh_st[l], tabs)
        new.append(st)
    y = final_norm(rows, x, P['final_norm_g']).reshape(rows.B, rows.T, D_MODEL)
    return y, tuple(jnp.stack(s, axis=0) for s in zip(*new))


def kernel(x_prompt, x_sample, c_prompt, c_sample, cache_mla_latent, cache_mla_krope, state_mlstm_C, state_mlstm_n,
           state_mlstm_m, state_rwkv_S, state_rwkv_shift, ada_w, ada_b, norm_g, ffn_w_in, ffn_w_out, w_in,
           mlstm_i_bias, mlstm_f_bias, mlstm_head_g, mla_kv_norm_g, mla_w_kv_b, rwkv_mu, rwkv_w0, rwkv_w2, rwkv_a0,
           rwkv_a2, rwkv_g2, rwkv_k_k, rwkv_k_a, rwkv_r_k, rwkv_lnx_g, rwkv_lnx_b, w_branch, b_merge, w_out,
           final_norm_g):
    P = dict(norm_g=norm_g, mlstm_i_bias=mlstm_i_bias, mlstm_f_bias=mlstm_f_bias, mlstm_head_g=mlstm_head_g,
             mla_kv_norm_g=mla_kv_norm_g, rwkv_mu=rwkv_mu, rwkv_w0=rwkv_w0, rwkv_w2=rwkv_w2, rwkv_a0=rwkv_a0,
             rwkv_a2=rwkv_a2, rwkv_g2=rwkv_g2, rwkv_k_k=rwkv_k_k, rwkv_k_a=rwkv_k_a, rwkv_r_k=rwkv_r_k,
             rwkv_lnx_g=rwkv_lnx_g, rwkv_lnx_b=rwkv_lnx_b, b_merge=b_merge, final_norm_g=final_norm_g)
    depth = w_in.shape[0]
    flat = lambda w: w.reshape((-1,) + w.shape[-2:])
    W = dict(ffn_in=cast_blocks(flat(ffn_w_in)), ffn_out=cast_blocks(flat(ffn_w_out)), w_in=prep_w_in(w_in),
             kv_b=cast_blocks(mla_w_kv_b), branch=cast_blocks(flat(w_branch)), w_out=cast_blocks(w_out))
    Bp, Tp, _ = x_prompt.shape
    Bs, Ts, _ = x_sample.shape
    n_c = Bp + Bs
    c_all = jnp.pad(jnp.concatenate([c_prompt, c_sample], axis=0), ((0, (-n_c) % 8), (0, 0)))
    mods = [ada_mod(c_all, ada_w[l], ada_b[l]) for l in range(depth)]
    rows_p = Rows(Bp, Tp, TM_MAX)
    rows_s = Rows(Bs, Ts, TM_MAX)
    zeros = lambda *s: jnp.zeros((depth, Bp) + s, F32)
    y_p, st_p = _trunk(rows_p, x_prompt, [m[:Bp] for m in mods], W, P,
                       zeros(0, MLA_LATENT), zeros(0, MLA_ROPE), zeros(ML_HEADS, ML_DV, ML_DQK), zeros(ML_HEADS, ML_DQK),
                       zeros(ML_HEADS), zeros(RW_HEADS, RW_HEAD, RW_HEAD), zeros(RW_COLS))
    y_s, st_s = _trunk(rows_s, x_sample, [m[Bp:n_c] for m in mods], W, P,
                       cache_mla_latent, cache_mla_krope, state_mlstm_C, state_mlstm_n, state_mlstm_m, state_rwkv_S,
                       state_rwkv_shift)
    return (y_p, y_s) + st_p + st_s
```

```python
import functools

import jax
import jax.numpy as jnp
import numpy as np
from jax import lax
from jax.experimental import pallas as pl
from jax.experimental.pallas import tpu as pltpu

F32 = jnp.float32
BF16 = jnp.bfloat16
HIGHEST = lax.Precision.HIGHEST

D_MODEL = 2048
CHUNK = 64
NORM_EPS = 1e-6
D_FF = 5632
N_ADA = 9
ML_HEADS, ML_DQK, ML_DV, ML_GATE_CAP = 4, 128, 256, 15.0
ML_CHUNK = 256
MLA_HEADS, MLA_NOPE, MLA_ROPE, MLA_V, MLA_LATENT = 8, 128, 64, 128, 512
MLA_SCALE = (MLA_NOPE + MLA_ROPE) ** -0.5
LATENT_ATTN_MAX_T = 64
ROPE_BASE = 10000.0
RW_HEADS, RW_HEAD = 16, 64
RW_WIDTH = RW_HEADS * RW_HEAD
RW_LN_EPS = 64e-5
RW_LORA = 256
RW_COLS = 3 * RW_WIDTH + RW_LORA
BRANCH_WIDTH = 1024

LANES = 128
VMEM_LIMIT = 56 * 1024 * 1024
TM_MAX = 1024
TN = 512

OFF_ML_Q, OFF_ML_K, OFF_ML_V, OFF_ML_O = 0, 512, 1024, 2048
OFF_QN, OFF_QR = 3072, 4096
OFF_RW_R, OFF_RW_K, OFF_RW_V = 5120, 6144, 7168
OFF_GATE = 8192
OFF_LAT = 14336
OFF_RW_LORA = 14848
OFF_KR = 15104
OFF_IF = 15232
NZ = 15360
IN_COLS = 14664


def _cparams(sem):
    return pltpu.CompilerParams(dimension_semantics=sem, vmem_limit_bytes=VMEM_LIMIT)


def _sigmoid(x):
    return 1.0 / (1.0 + jnp.exp(-x))


def _silu(x):
    return x * _sigmoid(x)


def _dot(a, b):
    return jnp.dot(a, b, preferred_element_type=F32)


def _dot_nt(a, b, precision=None):
    return lax.dot_general(a, b, (((1,), (1,)), ((), ())), precision=precision, preferred_element_type=F32)


def _dot_tn(a, b, precision=None):
    return lax.dot_general(a, b, (((0,), (0,)), ((), ())), precision=precision, preferred_element_type=F32)


class Rows:
    def __init__(self, B, T, tm_max):
        self.B, self.T, self.M = B, T, B * T
        if T >= 256:
            self.tm = min(tm_max, T)
            assert T % self.tm == 0
            self.tpg = T // self.tm
            self.R = 1
        else:
            self.tm = self.M
            self.tpg = 1
            self.R = self.M
        self.nm = self.M // self.tm

    def expand(self, v):
        if self.R == 1:
            return v[:, None, :]
        return jnp.repeat(v, self.T, axis=0)[None]

    def vec_spec(self, tn, col=lambda j: j):
        return pl.BlockSpec((None, self.R, tn), lambda i, j: (i // self.tpg, 0, col(j)))


def _ada_kernel(c_ref, w_ref, b_ref, o_ref):
    c = c_ref[...]
    o_ref[...] = _dot(_silu(c).astype(BF16), w_ref[...].astype(BF16)) + b_ref[...]


def ada_mod(c, w, b):
    R, D = c.shape
    N = w.shape[1]
    tn = 1024
    return pl.pallas_call(
        _ada_kernel,
        grid=(N // tn,),
        in_specs=[pl.BlockSpec((R, D), lambda j: (0, 0)),
                  pl.BlockSpec((D, tn), lambda j: (0, j)),
                  pl.BlockSpec((1, tn), lambda j: (0, j))],
        out_specs=pl.BlockSpec((R, tn), lambda j: (0, j)),
        out_shape=jax.ShapeDtypeStruct((R, N), F32),
        compiler_params=_cparams(("arbitrary",)),
        name="ada_mod",
    )(c, w, b.reshape(1, N))


def _norm_mod_kernel(x_ref, g_ref, sh_ref, sc_ref, o_ref):
    x = x_ref[...]
    y = x * lax.rsqrt(jnp.mean(x * x, axis=-1, keepdims=True) + NORM_EPS) * g_ref[...]
    o_ref[...] = (y * (1.0 + sc_ref[...]) + sh_ref[...]).astype(o_ref.dtype)


def norm_mod(rows, x, g, shift, scale):
    D = x.shape[1]
    tm = min(rows.tm, 512)
    sub = rows.tm // tm
    vec = pl.BlockSpec((None, rows.R if rows.R == 1 else tm, D),
                       (lambda i: (i // (rows.tpg * sub), 0, 0)) if rows.R == 1 else (lambda i: (0, i, 0)))
    return pl.pallas_call(
        _norm_mod_kernel,
        grid=(rows.M // tm,),
        in_specs=[pl.BlockSpec((tm, D), lambda i: (i, 0)),
                  pl.BlockSpec((1, D), lambda i: (0, 0)), vec, vec],
        out_specs=pl.BlockSpec((tm, D), lambda i: (i, 0)),
        out_shape=jax.ShapeDtypeStruct((rows.M, D), BF16),
        compiler_params=_cparams(("parallel",)),
        name="norm_mod",
    )(x, g.reshape(1, D), shift, scale)


def _final_norm_kernel(x_ref, g_ref, o_ref):
    x = x_ref[...]
    o_ref[...] = x * lax.rsqrt(jnp.mean(x * x, axis=-1, keepdims=True) + NORM_EPS) * g_ref[...]


def final_norm(rows, x, g):
    D = x.shape[1]
    tm = min(rows.tm, 512)
    return pl.pallas_call(
        _final_norm_kernel,
        grid=(rows.M // tm,),
        in_specs=[pl.BlockSpec((tm, D), lambda i: (i, 0)), pl.BlockSpec((1, D), lambda i: (0, 0))],
        out_specs=pl.BlockSpec((tm, D), lambda i: (i, 0)),
        out_shape=jax.ShapeDtypeStruct((rows.M, D), F32),
        compiler_params=_cparams(("parallel",)),
        name="final_norm",
    )(x, g.reshape(1, D))


def _mm_kernel(a_ref, w_ref, o_ref):
    o_ref[...] = _dot(a_ref[...].astype(BF16), w_ref[...]).astype(o_ref.dtype)


def matmul(a, w, tm, out_dtype=F32, name="matmul"):
    M, K = a.shape
    w, g = w
    _, nj, _, tn = w.shape
    N = nj * tn
    return pl.pallas_call(
        _mm_kernel,
        grid=(M // tm, nj),
        in_specs=[pl.BlockSpec((tm, K), lambda i, j: (i, 0)),
                  pl.BlockSpec((None, None, K, tn), lambda i, j: (g, j, 0, 0))],
        out_specs=pl.BlockSpec((tm, tn), lambda i, j: (i, j)),
        out_shape=jax.ShapeDtypeStruct((M, N), out_dtype),
        compiler_params=_cparams(("parallel", "arbitrary")),
        name=name,
    )(a, w)


def _swiglu_kernel(h_ref, wg_ref, wu_ref, o_ref):
    h = h_ref[...]
    g = _dot(h, wg_ref[...])
    u = _dot(h, wu_ref[...])
    o_ref[...] = (_silu(g) * u).astype(o_ref.dtype)


def mm_swiglu(rows, h, w_in):
    K = h.shape[1]
    w_in, g = w_in
    tn = w_in.shape[3]
    nj = D_FF // tn
    return pl.pallas_call(
        _swiglu_kernel,
        grid=(rows.nm, nj),
        in_specs=[pl.BlockSpec((rows.tm, K), lambda i, j: (i, 0)),
                  pl.BlockSpec((None, None, K, tn), lambda i, j: (g, j, 0, 0)),
                  pl.BlockSpec((None, None, K, tn), lambda i, j: (g, j + nj, 0, 0))],
        out_specs=pl.BlockSpec((rows.tm, tn), lambda i, j: (i, j)),
        out_shape=jax.ShapeDtypeStruct((rows.M, D_FF), BF16),
        compiler_params=_cparams(("parallel", "arbitrary")),
        name="mm_swiglu",
    )(h, w_in, w_in)


def _resid_kernel(a_ref, w_ref, x_ref, g_ref, o_ref, *, coef):
    o_ref[...] = x_ref[...] + (coef * g_ref[...]) * _dot(a_ref[...], w_ref[...])


def mm_resid(rows, a, w, x, gate, coef):
    K = a.shape[1]
    w, g = w
    _, nj, _, tn = w.shape
    N = nj * tn
    return pl.pallas_call(
        functools.partial(_resid_kernel, coef=coef),
        grid=(rows.nm, nj),
        in_specs=[pl.BlockSpec((rows.tm, K), lambda i, j: (i, 0)),
                  pl.BlockSpec((None, None, K, tn), lambda i, j: (g, j, 0, 0)),
                  pl.BlockSpec((rows.tm, tn), lambda i, j: (i, j)),
                  rows.vec_spec(tn)],
        out_specs=pl.BlockSpec((rows.tm, tn), lambda i, j: (i, j)),
        out_shape=jax.ShapeDtypeStruct((rows.M, N), F32),
        compiler_params=_cparams(("parallel", "arbitrary")),
        name="mm_resid",
    )(a, w, x, gate)


def _branch_kernel(om_ref, oa_ref, or_ref, w0_ref, w1_ref, w2_ref, z0_ref, z1_ref, z2_ref, b0_ref, b1_ref, b2_ref,
                   o_ref):
    acc = _sigmoid(z0_ref[...] + b0_ref[...]) * _dot(om_ref[...], w0_ref[...])
    acc += _sigmoid(z1_ref[...] + b1_ref[...]) * _dot(oa_ref[...], w1_ref[...])
    acc += _sigmoid(z2_ref[...] + b2_ref[...]) * _dot(or_ref[...], w2_ref[...])
    o_ref[...] = acc.astype(o_ref.dtype)


def mm_branch(rows, om, oa, orw, w_branch, z_all, b_merge):
    w_branch, g = w_branch
    tn = w_branch.shape[3]
    D = D_MODEL
    nj = D // tn
    a_spec = pl.BlockSpec((rows.tm, BRANCH_WIDTH), lambda i, j: (i, 0))
    w_specs = [pl.BlockSpec((None, None, BRANCH_WIDTH, tn), lambda i, j, n=n: (g + n, j, 0, 0)) for n in range(3)]
    z_specs = [pl.BlockSpec((rows.tm, tn), lambda i, j, n=n: (i, (OFF_GATE + n * D) // tn + j)) for n in range(3)]
    b_specs = [pl.BlockSpec((1, tn), lambda i, j, n=n: (0, n * D // tn + j)) for n in range(3)]
    return pl.pallas_call(
        _branch_kernel,
        grid=(rows.nm, nj),
        in_specs=[a_spec, a_spec, a_spec] + w_specs + z_specs + b_specs,
        out_specs=pl.BlockSpec((rows.tm, tn), lambda i, j: (i, j)),
        out_shape=jax.ShapeDtypeStruct((rows.M, D), BF16),
        compiler_params=_cparams(("parallel", "arbitrary")),
        name="mm_branch",
    )(om, oa, orw, w_branch, w_branch, w_branch, z_all, z_all, z_all, b_merge, b_merge, b_merge)


def _mlstm_kernel(q_ref, k_ref, v_ref, o_ref, if_ref, bias_ref, hg_ref, C0_ref, n0_ref, m0_ref,
                  om_ref, C_ref, n_ref, m_ref, *, L):
    c = pl.program_id(1)

    @pl.when(c == 0)
    def _():
        C_ref[...] = C0_ref[...]
        n_ref[...] = n0_ref[...]
        m_ref[...] = m0_ref[...]

    H = ML_HEADS
    sc = ML_GATE_CAP * jnp.tanh((if_ref[...] + bias_ref[...]) * (1.0 / ML_GATE_CAP))
    lf = jnp.minimum(sc, 0.0) - jnp.log(1.0 + jnp.exp(-jnp.abs(sc)))
    row = lax.broadcasted_iota(jnp.int32, (L, L), 0)
    col = lax.broadcasted_iota(jnp.int32, (L, L), 1)
    causal = row >= col
    bcum = jnp.dot(causal.astype(F32), lf, precision=HIGHEST, preferred_element_type=F32)
    sel = (lax.broadcasted_iota(jnp.int32, (8, LANES), 0) == lax.broadcasted_iota(jnp.int32, (8, LANES), 1)).astype(F32)
    scT = _dot_nt(sel, sc, HIGHEST)
    bT = _dot_nt(sel, bcum, HIGHEST)
    hs = range(H)
    each = lambda f, *lists: [f(*xs) for xs in zip(*lists)]
    qsl = [slice(h * ML_DQK, (h + 1) * ML_DQK) for h in hs]
    vsl = [slice(h * ML_DV, (h + 1) * ML_DV) for h in hs]
    b_col = [bcum[:, H + h:H + h + 1] for h in hs]
    b_row = [bT[H + h:H + h + 1, :] for h in hs]
    ig_row = [scT[h:h + 1, :] for h in hs]
    ig_col = [sc[:, h:h + 1] for h in hs]
    m_prev = [m_ref[0, h, :, 0:1] for h in hs]
    C = [C_ref[0, h] for h in hs]
    n = [n_ref[0, h] for h in hs]
    qf = [q_ref[:, s] for s in qsl]
    q = each(lambda x: x.astype(BF16), qf)
    kb = [(k_ref[:, s] * (ML_DQK ** -0.5)).astype(BF16) for s in qsl]
    v = [v_ref[:, s] for s in vsl]
    logD = each(lambda bc, br, ir: jnp.where(causal, bc - br + ir, -jnp.inf), b_col, b_row, ig_row)
    g = each(lambda bc, mp: bc + mp, b_col, m_prev)
    m_t = each(lambda gg, ld: jnp.maximum(gg, jnp.max(ld, axis=1, keepdims=True)), g, logD)
    Dm = each(lambda ld, mt: jnp.exp(ld - mt), logD, m_t)
    inter = each(lambda gg, mt: jnp.exp(gg - mt), g, m_t)
    qk = each(lambda qq, kk, d: _dot_nt(qq, kk) * d, q, kb, Dm)
    qC = each(lambda qq, cc: _dot_nt(qq, cc.astype(BF16)), q, C)
    num = each(lambda s, vv, it, x: _dot(s.astype(BF16), vv.astype(BF16)) + it * x, qk, v, inter, qC)
    qn = each(lambda x, nn: jnp.sum(x * nn, axis=1, keepdims=True), qf, n)
    den = each(lambda s, it, x: jnp.sum(s, axis=1, keepdims=True) + it * x, qk, inter, qn)
    hh = each(lambda nu, de, mt: nu / jnp.maximum(jnp.abs(de), jnp.exp(-mt)), num, den, m_t)
    m_new = each(lambda mt: mt[L - 1:L, :], m_t)
    b_last = each(lambda bc: bc[L - 1:L, :], b_col)
    carry_w = each(lambda bl, mp, mn: jnp.exp(bl + mp - mn), b_last, m_prev, m_new)
    w_s = each(lambda bl, bc, ic, mn: jnp.exp(bl - bc + ic - mn), b_last, b_col, ig_col, m_new)
    dC = each(lambda w, vv, kk: _dot_tn((w * vv).astype(BF16), kk), w_s, v, kb)
    dn = each(lambda w, kk: jnp.sum(w * kk.astype(F32), axis=0, keepdims=True), w_s, kb)
    for h in hs:
        y = hh[h] * lax.rsqrt(jnp.mean(hh[h] * hh[h], axis=-1, keepdims=True) + NORM_EPS) * hg_ref[:, vsl[h]]
        om_ref[:, vsl[h]] = (_sigmoid(o_ref[:, vsl[h]]) * y).astype(om_ref.dtype)
        C_ref[0, h] = carry_w[h] * C[h] + dC[h]
        n_ref[0, h] = carry_w[h] * n[h] + dn[h]
        m_ref[0, h] = jnp.broadcast_to(m_new[h], (1, LANES))


def mlstm(B, T, z_all, bias_row, head_g, C0, n0, m0):
    L = next((c for c in (ML_CHUNK, CHUNK) if T % c == 0), T)
    nc = T // L
    H = ML_HEADS
    zrow = lambda w, off: pl.BlockSpec((L, w), lambda b, c: (b * nc + c, off // w))
    st = lambda *shape: pl.BlockSpec((1,) + shape, lambda b, c: (b,) + (0,) * len(shape))
    om, C, n, m = pl.pallas_call(
        functools.partial(_mlstm_kernel, L=L),
        grid=(B, nc),
        in_specs=[zrow(H * ML_DQK, OFF_ML_Q), zrow(H * ML_DQK, OFF_ML_K), zrow(H * ML_DV, OFF_ML_V),
                  zrow(H * ML_DV, OFF_ML_O), zrow(LANES, OFF_IF),
                  pl.BlockSpec((1, LANES), lambda b, c: (0, 0)),
                  pl.BlockSpec((1, H * ML_DV), lambda b, c: (0, 0)),
                  st(H, ML_DV, ML_DQK), st(H, 1, ML_DQK), st(H, 1, LANES)],
        out_specs=[pl.BlockSpec((L, H * ML_DV), lambda b, c: (b * nc + c, 0)),
                   st(H, ML_DV, ML_DQK), st(H, 1, ML_DQK), st(H, 1, LANES)],
        out_shape=[jax.ShapeDtypeStruct((B * T, H * ML_DV), BF16),
                   jax.ShapeDtypeStruct((B, H, ML_DV, ML_DQK), F32),
                   jax.ShapeDtypeStruct((B, H, 1, ML_DQK), F32),
                   jax.ShapeDtypeStruct((B, H, 1, LANES), F32)],
        compiler_params=_cparams(("parallel", "arbitrary")),
        name="mlstm",
    )(z_all, z_all, z_all, z_all, z_all, bias_row, head_g.reshape(1, -1), C0, n0.reshape(B, H, 1, ML_DQK),
      jnp.broadcast_to(m0[:, :, None, None], (B, H, 1, LANES)))
    return om, C, n.reshape(B, H, ML_DQK), m[:, :, 0, 0]


def _rope_lanes(x, cos, sin):
    swapped = pltpu.roll(x, 32, 1) + pltpu.roll(x, 96, 1)
    return x * cos + swapped * sin


def _mla_pre_kernel(qr_ref, lat_ref, kr_ref, cos_ref, sin_ref, g_ref, qro_ref, lato_ref, kro_ref):
    cos = cos_ref[...]
    sin = sin_ref[...]
    for h in range(MLA_HEADS):
        sl = slice(h * LANES, (h + 1) * LANES)
        qro_ref[:, sl] = (_rope_lanes(qr_ref[:, sl], cos, sin) * MLA_SCALE).astype(qro_ref.dtype)
    kro_ref[...] = _rope_lanes(kr_ref[...], cos, sin)
    lat = lat_ref[...]
    lato_ref[...] = lat * lax.rsqrt(jnp.mean(lat * lat, axis=-1, keepdims=True) + NORM_EPS) * g_ref[...]


def mla_pre(rows, z_all, cos_tab, sin_tab, kv_norm_g):
    tm = min(rows.tm, 256)
    M = rows.M
    zspec = lambda w, off: pl.BlockSpec((tm, w), lambda i: (i, off // w))
    W = MLA_HEADS * LANES
    return pl.pallas_call(
        _mla_pre_kernel,
        grid=(M // tm,),
        in_specs=[zspec(W, OFF_QR), zspec(MLA_LATENT, OFF_LAT), zspec(LANES, OFF_KR),
                  pl.BlockSpec((tm, LANES), lambda i: (i, 0)), pl.BlockSpec((tm, LANES), lambda i: (i, 0)),
                  pl.BlockSpec((1, MLA_LATENT), lambda i: (0, 0))],
        out_specs=[pl.BlockSpec((tm, W), lambda i: (i, 0)), pl.BlockSpec((tm, MLA_LATENT), lambda i: (i, 0)),
                   pl.BlockSpec((tm, LANES), lambda i: (i, 0))],
        out_shape=[jax.ShapeDtypeStruct((M, W), BF16), jax.ShapeDtypeStruct((M, MLA_LATENT), F32),
                   jax.ShapeDtypeStruct((M, LANES), F32)],
        compiler_params=_cparams(("parallel",)),
        name="mla_pre",
    )(z_all, z_all, z_all, cos_tab, sin_tab, kv_norm_g.reshape(1, -1))


def _attn_kernel(qn_ref, qr_ref, kn_ref, kr_ref, v_ref, o_ref, m_sc, l_sc, acc_sc, *, tq, tk, past, nk):
    qi = pl.program_id(2)
    q = jnp.concatenate([(qn_ref[...] * MLA_SCALE).astype(BF16), qr_ref[...]], axis=1)
    m_sc[...] = jnp.full_like(m_sc, -jnp.inf)
    l_sc[...] = jnp.zeros_like(l_sc)
    acc_sc[...] = jnp.zeros_like(acc_sc)
    q0 = past + qi * tq
    n_full = jnp.minimum(((q0 // CHUNK + 1) * CHUNK) // tk, nk)
    n_vis = jnp.minimum((((q0 + tq - 1) // CHUNK + 1) * CHUNK + tk - 1) // tk, nk)

    def tile(kj, masked):
        k0 = pl.multiple_of(kj * tk, tk)
        k = jnp.concatenate([kn_ref[pl.ds(k0, tk), :], kr_ref[pl.ds(k0, tk), :]], axis=1)
        s = _dot_nt(q, k)
        if masked:
            q_chunk = (q0 + lax.broadcasted_iota(jnp.int32, (tq, 1), 0)) // CHUNK
            k_chunk = (k0 + lax.broadcasted_iota(jnp.int32, (1, tk), 1)) // CHUNK
            s = jnp.where(k_chunk <= q_chunk, s, -jnp.inf)
        m_prev = m_sc[...]
        m_new = jnp.maximum(m_prev, jnp.max(s, axis=1, keepdims=True))
        alpha = jnp.exp(m_prev - m_new)
        p = jnp.exp(s - m_new)
        l_sc[...] = alpha * l_sc[...] + jnp.sum(p, axis=1, keepdims=True)
        acc_sc[...] = alpha * acc_sc[...] + _dot(p.astype(BF16), v_ref[pl.ds(k0, tk), :])
        m_sc[...] = m_new

    def full_tile(kj, c):
        tile(kj, False)
        return c

    def edge_tile(kj, c):
        tile(kj, True)
        return c

    lax.fori_loop(0, n_full, full_tile, 0)
    lax.fori_loop(n_full, n_vis, edge_tile, 0)
    o_ref[...] = (acc_sc[...] / l_sc[...]).astype(o_ref.dtype)


def mla_attention(B, T, Sk, past, z_all, qr, kv, kr_all):
    H = MLA_HEADS
    tq = min(T, 512)
    tk = next((t for t in (2048, 1024, 512) if Sk % t == 0), Sk)
    nq, nk = T // tq, Sk // tk
    z3 = z_all.reshape(B, T, NZ)
    qr3 = qr.reshape(B, T, H * LANES)
    kv3 = kv.reshape(B, Sk, H * 2 * LANES)
    kr3 = kr_all.reshape(B, Sk, LANES)
    out = pl.pallas_call(
        functools.partial(_attn_kernel, tq=tq, tk=tk, past=past, nk=nk),
        grid=(B, H, nq),
        in_specs=[pl.BlockSpec((None, tq, LANES), lambda b, h, qi: (b, qi, OFF_QN // LANES + h)),
                  pl.BlockSpec((None, tq, LANES), lambda b, h, qi: (b, qi, h)),
                  pl.BlockSpec((None, Sk, LANES), lambda b, h, qi: (b, 0, 2 * h)),
                  pl.BlockSpec((None, Sk, LANES), lambda b, h, qi: (b, 0, 0)),
                  pl.BlockSpec((None, Sk, LANES), lambda b, h, qi: (b, 0, 2 * h + 1))],
        out_specs=pl.BlockSpec((None, tq, LANES), lambda b, h, qi: (b, qi, h)),
        out_shape=jax.ShapeDtypeStruct((B, T, H * MLA_V), BF16),
        scratch_shapes=[pltpu.VMEM((tq, 1), F32), pltpu.VMEM((tq, 1), F32), pltpu.VMEM((tq, MLA_V), F32)],
        compiler_params=_cparams(("parallel", "parallel", "arbitrary")),
        name="mla_attention",
    )(z3, qr3, kv3, kr3, kv3)
    return out.reshape(B * T, H * MLA_V)


def _attn_latent_kernel(qn_ref, qr_ref, lat_ref, kr_ref, w_ref, o_ref, *, T, Sk, past):
    lat = lat_ref[...]
    kr = kr_ref[...]
    q_chunk = (past + lax.broadcasted_iota(jnp.int32, (T, 1), 0)) // CHUNK
    k_chunk = lax.broadcasted_iota(jnp.int32, (1, Sk), 1) // CHUNK
    visible = k_chunk <= q_chunk
    per_block = w_ref.shape[2] // (MLA_NOPE + MLA_V)
    hs = range(MLA_HEADS)
    sls = [slice(h * LANES, (h + 1) * LANES) for h in hs]
    c0 = [(h % per_block) * (MLA_NOPE + MLA_V) for h in hs]
    wk = [w_ref[h // per_block, :, c0[h]:c0[h] + MLA_NOPE] for h in hs]
    wv = [w_ref[h // per_block, :, c0[h] + MLA_NOPE:c0[h] + MLA_NOPE + MLA_V] for h in hs]
    qa = [_dot_nt((qn_ref[:, sls[h]] * MLA_SCALE).astype(BF16), wk[h]).astype(BF16) for h in hs]
    s = [jnp.where(visible, _dot_nt(qa[h], lat) + _dot_nt(qr_ref[:, sls[h]], kr), -jnp.inf) for h in hs]
    p = [jnp.exp(x - jnp.max(x, axis=1, keepdims=True)) for x in s]
    ctx = [_dot(x.astype(BF16), lat) / jnp.sum(x, axis=1, keepdims=True) for x in p]
    for h in hs:
        o_ref[:, sls[h]] = _dot(ctx[h].astype(BF16), wv[h]).astype(o_ref.dtype)


def mla_attention_latent(B, T, Sk, past, z_all, qr, lat_all, kr_all, w_kv):
    H = MLA_HEADS
    w, g = w_kv
    _, nj, K, tn = w.shape
    W = H * LANES
    out = pl.pallas_call(
        functools.partial(_attn_latent_kernel, T=T, Sk=Sk, past=past),
        grid=(B,),
        in_specs=[pl.BlockSpec((None, T, W), lambda b: (b, 0, OFF_QN // W)),
                  pl.BlockSpec((None, T, W), lambda b: (b, 0, 0)),
                  pl.BlockSpec((None, Sk, MLA_LATENT), lambda b: (b, 0, 0)),
                  pl.BlockSpec((None, Sk, LANES), lambda b: (b, 0, 0)),
                  pl.BlockSpec((None, nj, K, tn), lambda b: (g, 0, 0, 0))],
        out_specs=pl.BlockSpec((None, T, W), lambda b: (b, 0, 0)),
        out_shape=jax.ShapeDtypeStruct((B, T, H * MLA_V), BF16),
        compiler_params=_cparams(("parallel",)),
        name="mla_attention_latent",
    )(z_all.reshape(B, T, NZ), qr.reshape(B, T, W), lat_all.reshape(B, Sk, MLA_LATENT), kr_all.reshape(B, Sk, LANES), w)
    return out.reshape(B * T, H * MLA_V)


def _pair_sum_matrix():
    r = lax.broadcasted_iota(jnp.int32, (LANES, LANES), 0) // RW_HEAD
    c = lax.broadcasted_iota(jnp.int32, (LANES, LANES), 1) // RW_HEAD
    return (r == c).astype(BF16)


def _head_sum(x, bd):
    m = x.shape[0]
    parts = []
    for p in range(RW_WIDTH // LANES):
        hi, lo = _split_bf16(x[:, p * LANES:(p + 1) * LANES])
        both = _dot(jnp.concatenate([hi, lo], axis=0), bd)
        parts.append(both[:m] + both[m:])
    return jnp.concatenate(parts, axis=1)


def _rwkv_pre_kernel(r_ref, k_ref, v_ref, lo_ref, tr_ref, tk_ref, tv_ref, tlo_ref, fr_ref, fk_ref, fv_ref, flo_ref,
                     mu_ref, mulo_ref, w0_ref, w2_ref, a0_ref, a2_ref, g2_ref, kk_ref, ka_ref, rk_ref,
                     ro_ref, wo_ref, ko_ref, vo_ref, ao_ref, bo_ref, go_ref, bonus_ref, *, T):
    W = RW_WIDTH
    tm = r_ref.shape[0]
    row = lax.broadcasted_iota(jnp.int32, (tm, 1), 0)
    stream_start = (pl.program_id(0) * tm + row) % T == 0

    def mix(z_ref, tail_ref, first_ref, mu):
        z = z_ref[...]
        prev = jnp.where(row == 0, tail_ref[7:8, :], pltpu.roll(z, 1, 0))
        prev = jnp.where(stream_start, first_ref[...], prev)
        return z + (prev - z) * mu

    r = mix(r_ref, tr_ref, fr_ref, mu_ref[:, 0:W])
    k = mix(k_ref, tk_ref, fk_ref, mu_ref[:, W:2 * W])
    v = mix(v_ref, tv_ref, fv_ref, mu_ref[:, 2 * W:3 * W])
    lo = mix(lo_ref, tlo_ref, flo_ref, mulo_ref[...])
    wd, ad, gd = lo[:, 0:64], lo[:, 64:128], lo[:, 128:256]
    hdot = lambda x, w: _mm(x, w, "nn", 3)
    wpre = w0_ref[...] + hdot(jnp.tanh(wd), w2_ref[...])
    w_raw = jnp.minimum(wpre, 0.0) - jnp.log(1.0 + jnp.exp(-jnp.abs(wpre))) - 0.5
    log_decay = -jnp.exp(w_raw)
    a = _sigmoid(a0_ref[...] + hdot(ad, a2_ref[...]))
    g = hdot(_sigmoid(gd), g2_ref[...])
    bd = _pair_sum_matrix()
    kk = k * kk_ref[...]
    kk = kk / jnp.maximum(jnp.sqrt(_head_sum(kk * kk, bd)), 1e-12)
    k = k * (1.0 + (a - 1.0) * ka_ref[...])
    ro_ref[...] = r
    wo_ref[...] = log_decay
    ko_ref[...] = k
    vo_ref[...] = v
    ao_ref[...] = -kk
    bo_ref[...] = kk * a
    go_ref[...] = g
    bonus_ref[...] = _head_sum(r * k * rk_ref[...], bd) * v


def rwkv_pre(rows, z_all, shift0, mu, w0, w2, a0, a2, g2, k_k, k_a, r_k):
    M = rows.M
    tm = min(rows.tm, 256)
    W = RW_WIDTH
    zs = lambda w, off: pl.BlockSpec((tm, w), lambda i: (i, off // w))
    tail = lambda w, off: pl.BlockSpec((8, w), lambda i: (jnp.maximum(i * (tm // 8) - 1, 0), off // w))
    if rows.R == 1:
        first = lambda w: pl.BlockSpec((None, 1, w), lambda i: (i // (rows.T // tm), 0, 0))
    else:
        first = lambda w: pl.BlockSpec((None, tm, w), lambda i: (0, i, 0))
    full = lambda a: pl.BlockSpec(a.shape, lambda i: (0,) * a.ndim)
    row = lambda a: a.reshape(1, -1)
    firsts = [rows.expand(shift0[:, o:o + w]) for o, w in ((0, W), (W, W), (2 * W, W), (3 * W, RW_LORA))]
    params = [row(mu[:3 * W]), row(mu[3 * W:]), row(w0), w2, row(a0), a2, g2, row(k_k), row(k_a), row(r_k)]
    cols = ((W, OFF_RW_R), (W, OFF_RW_K), (W, OFF_RW_V), (RW_LORA, OFF_RW_LORA))
    outs = pl.pallas_call(
        functools.partial(_rwkv_pre_kernel, T=rows.T),
        grid=(M // tm,),
        in_specs=[zs(*c) for c in cols] + [tail(*c) for c in cols] + [first(c[0]) for c in cols]
        + [full(p) for p in params],
        out_specs=[pl.BlockSpec((tm, W), lambda i: (i, 0))] * 8,
        out_shape=[jax.ShapeDtypeStruct((M, W), F32)] * 8,
        compiler_params=_cparams(("parallel",)),
        name="rwkv_pre",
    )(*([z_all] * 8), *firsts, *params)
    return outs


_DIMS = {"nn": (((1,), (0,)), ((), ())), "nt": (((1,), (1,)), ((), ())), "tn": (((0,), (0,)), ((), ()))}


def _split_bf16(x):
    hi = x.astype(BF16)
    return hi, (x - hi.astype(F32)).astype(BF16)


def _mm(a, b, kind, passes):
    dims = _DIMS[kind]
    if passes == 6:
        return lax.dot_general(a, b, dims, precision=HIGHEST, preferred_element_type=F32)
    dg = lambda x, y: lax.dot_general(x, y, dims, preferred_element_type=F32)
    if passes == 1:
        return dg(a.astype(BF16), b.astype(BF16))
    ah, al = _split_bf16(a)
    bh, bl = _split_bf16(b)
    m = a.shape[0]
    if kind == "tn" or m < RW_HEAD:
        return dg(ah, bh) + (dg(al, bh) + dg(ah, bl))
    both = dg(jnp.concatenate([ah, al], axis=0), bh)
    return both[:m] + (both[m:] + dg(ah, bl))


def _rwkv_chunk_kernel(r_ref, lw_ref, k_ref, v_ref, a_ref, b_ref, S0_ref, y_ref, S_ref, *, L, P_G, P_T, P_X, P_S):
    c = pl.program_id(1)

    @pl.when(c == 0)
    def _():
        S_ref[...] = S0_ref[...]

    N = RW_HEAD
    row = lax.broadcasted_iota(jnp.int32, (L, L), 0)
    col = lax.broadcasted_iota(jnp.int32, (L, L), 1)
    incl = row >= col
    strict = row > col
    eye = (row == col).astype(F32)
    tri = incl.astype(F32)
    cum_all = jnp.dot(tri, lw_ref[...], precision=HIGHEST, preferred_element_type=F32)
    sls = [slice(h * N, (h + 1) * N) for h in range(RW_HEADS)]
    each = lambda f, *lists: [f(*xs) for xs in zip(*lists)]
    S0 = [S_ref[0, h] for h in range(RW_HEADS)]
    lw = [lw_ref[:, s] for s in sls]
    cum = [cum_all[:, s] for s in sls]
    V = [v_ref[:, s] for s in sls]
    g_in = each(jnp.exp, cum)
    g_inv = each(lambda c: jnp.exp(-c), cum)
    At = each(lambda s, c, w: a_ref[:, s] * jnp.exp(c - w), sls, cum, lw)
    Bt = each(lambda s, g: b_ref[:, s] * g, sls, g_inv)
    Kt = each(lambda s, g: k_ref[:, s] * g, sls, g_inv)
    Rt = each(lambda s, g: r_ref[:, s] * g, sls, g_in)
    right = each(lambda x, y: jnp.concatenate([x, y], axis=0), Bt, Kt)
    G = each(lambda x, y, rt: _mm(jnp.concatenate([x, y], axis=0), rt, "nt", P_G), At, Rt, right)
    Nab = each(lambda g: jnp.where(strict, g[:L, :L], 0.0), G)
    Nak = each(lambda g: jnp.where(strict, g[:L, L:], 0.0), G)
    Mrb = each(lambda g: jnp.where(incl, g[L:, :L], 0.0), G)
    Mrk = each(lambda g: jnp.where(incl, g[L:, L:], 0.0), G)
    T = each(lambda n: eye + jnp.where(row // 2 == col // 2, n, 0.0), Nab)
    s = 2
    while s < L:
        off = (row // (2 * s) == col // (2 * s)) & (row // s != col // s)
        T = each(lambda t, n: t + _mm(t, _mm(jnp.where(off, n, 0.0), t, "nn", P_T), "nn", P_T), T, Nab)
        s *= 2
    NV = each(lambda n, v: _mm(n, v, "nn", P_X), Nak, V)
    TX = each(lambda t, x, nv: _mm(t, jnp.concatenate([x, nv], axis=1), "nn", P_X), T, At, NV)
    MX = each(lambda m, tx: _mm(m, tx, "nn", P_X), Mrb, TX)
    MV = each(lambda m, v: _mm(m, v, "nn", P_X), Mrk, V)
    Y = each(lambda rt, mx, mv, s0: _mm(rt + mx[:, :N], s0, "nt", P_S) + (mx[:, N:] + mv), Rt, MX, MV, S0)
    U = each(lambda tx, s0: _mm(tx[:, :N], s0, "nt", P_S) + tx[:, N:], TX, S0)
    upd = each(lambda u, v, bt, kt: _mm(u, bt, "tn", P_S) + _mm(v, kt, "tn", P_S), U, V, Bt, Kt)
    for h, s in enumerate(sls):
        y_ref[:, s] = Y[h]
        S_ref[0, h] = (S0[h] + upd[h]) * g_in[h][L - 1:L, :]


def rwkv_scan(B, T, r, lw, k, v, a, b, S0):
    L = CHUNK if T % CHUNK == 0 else T
    assert L & (L - 1) == 0
    nc = T // L
    xs = pl.BlockSpec((L, RW_WIDTH), lambda bi, c: (bi * nc + c, 0))
    ss = pl.BlockSpec((1, RW_HEADS, RW_HEAD, RW_HEAD), lambda bi, c: (bi, 0, 0, 0))
    return pl.pallas_call(
        functools.partial(_rwkv_chunk_kernel, L=L, P_G=3, P_T=3, P_X=3, P_S=3),
        grid=(B, nc),
        in_specs=[xs] * 6 + [ss],
        out_specs=[xs, ss],
        out_shape=[jax.ShapeDtypeStruct((B * T, RW_WIDTH), F32),
                   jax.ShapeDtypeStruct((B, RW_HEADS, RW_HEAD, RW_HEAD), F32)],
        compiler_params=_cparams(("parallel", "arbitrary")),
        name="rwkv_scan",
    )(r, lw, k, v, a, b, S0)


def _rwkv_post_kernel(y_ref, g_ref, bonus_ref, lg_ref, lb_ref, o_ref):
    bd = _pair_sum_matrix()
    y = y_ref[...]
    mean = _head_sum(y, bd) * (1.0 / RW_HEAD)
    d = y - mean
    var = _head_sum(d * d, bd) * (1.0 / RW_HEAD)
    yn = d * lax.rsqrt(var + RW_LN_EPS) * lg_ref[...] + lb_ref[...]
    o_ref[...] = ((yn + bonus_ref[...]) * g_ref[...]).astype(o_ref.dtype)


def rwkv_post(rows, y, g, bonus, lnx_g, lnx_b):
    M = rows.M
    tm = min(rows.tm, 256)
    W = RW_WIDTH
    xs = pl.BlockSpec((tm, W), lambda i: (i, 0))
    ps = pl.BlockSpec((1, W), lambda i: (0, 0))
    return pl.pallas_call(
        _rwkv_post_kernel,
        grid=(M // tm,),
        in_specs=[xs, xs, xs, ps, ps],
        out_specs=xs,
        out_shape=jax.ShapeDtypeStruct((M, W), BF16),
        compiler_params=_cparams(("parallel",)),
        name="rwkv_post",
    )(y, g, bonus, lnx_g.reshape(1, W), lnx_b.reshape(1, W))


def _cast_blocks_kernel(x_ref, o_ref):
    o_ref[...] = x_ref[...].astype(o_ref.dtype)


def cast_blocks(w):
    lead = w.shape[:-2]
    K, N = w.shape[-2:]
    G = int(np.prod(lead))
    tk = _row_tile(K, 1024)

    def src(g, j, k):
        idx = []
        for d in reversed(lead):
            idx.append(g % d)
            g = g // d
        return tuple(reversed(idx)) + (k, j)

    return pl.pallas_call(
        _cast_blocks_kernel,
        grid=(G, N // TN, K // tk),
        in_specs=[pl.BlockSpec((None,) * len(lead) + (tk, TN), src)],
        out_specs=pl.BlockSpec((None, None, tk, TN), lambda g, j, k: (g, j, k, 0)),
        out_shape=jax.ShapeDtypeStruct((G, N // TN, K, TN), BF16),
        compiler_params=_cparams(("parallel", "parallel", "parallel")),
        name="cast_blocks",
    )(w)


def _w_in_segments():
    o_ml = 0
    o_if = 3072
    o_q = o_if + 2 * ML_HEADS
    o_lat = o_q + MLA_HEADS * (MLA_NOPE + MLA_ROPE)
    o_kr = o_lat + MLA_LATENT
    o_rw = o_kr + MLA_ROPE
    o_lo = o_rw + 3 * RW_WIDTH
    o_gate = o_lo + RW_LORA
    segs = [(OFF_ML_Q, o_ml, 3072), (OFF_RW_R, o_rw, 3 * RW_WIDTH), (OFF_GATE, o_gate, 3 * D_MODEL),
            (OFF_LAT, o_lat, MLA_LATENT), (OFF_RW_LORA, o_lo, RW_LORA),
            (OFF_KR, o_kr, MLA_ROPE), (OFF_KR + MLA_ROPE, None, LANES - MLA_ROPE),
            (OFF_IF, o_if, 2 * ML_HEADS), (OFF_IF + 2 * ML_HEADS, None, LANES - 2 * ML_HEADS)]
    for h in range(MLA_HEADS):
        src = o_q + h * (MLA_NOPE + MLA_ROPE)
        segs += [(OFF_QN + h * MLA_NOPE, src, MLA_NOPE), (OFF_QR + h * LANES, src + MLA_NOPE, MLA_ROPE),
                 (OFF_QR + h * LANES + MLA_ROPE, None, LANES - MLA_ROPE)]
    assert sum(s[2] for s in segs) == NZ and o_gate + 3 * D_MODEL == IN_COLS
    return segs


def _prep_w_in_kernel(x_ref, o_ref):
    rows = x_ref.shape[0]
    for dst, src, width in _w_in_segments():
        while width:
            j, off = divmod(dst, TN)
            n = min(width, TN - off)
            if src is None:
                o_ref[j, :, off:off + n] = jnp.zeros((rows, n), o_ref.dtype)
            else:
                o_ref[j, :, off:off + n] = x_ref[:, src:src + n].astype(o_ref.dtype)
                src += n
            dst, width = dst + n, width - n


def prep_w_in(w):
    G, K, N = w.shape
    tk = 128
    return pl.pallas_call(
        _prep_w_in_kernel,
        grid=(G, K // tk),
        in_specs=[pl.BlockSpec((None, tk, N), lambda g, k: (g, k, 0))],
        out_specs=pl.BlockSpec((None, NZ // TN, tk, TN), lambda g, k: (g, 0, k, 0)),
        out_shape=jax.ShapeDtypeStruct((G, NZ // TN, K, TN), BF16),
        compiler_params=_cparams(("parallel", "parallel")),
        name="prep_w_in",
    )(w)


def _row_tile(M, cap):
    return max(t for t in range(8, cap + 1, 8) if M % t == 0)


def _rope_tables(T, past, reps):
    half = MLA_ROPE // 2
    inv = ROPE_BASE ** (-jnp.arange(half, dtype=F32) / half)
    ang = (past + jnp.arange(T)).astype(F32)[:, None] * inv[None, :]
    cos, sin = jnp.cos(ang), jnp.sin(ang)
    zeros = jnp.zeros((T, LANES - MLA_ROPE), F32)
    cos_t = jnp.concatenate([cos, cos, zeros], axis=1)
    sin_t = jnp.concatenate([-sin, sin, zeros], axis=1)
    return jnp.tile(cos_t, (reps, 1)), jnp.tile(sin_t, (reps, 1))


def _layer(rows, x, mod, l, W, P, lat_past, kr_past, C0, n0, m0, S0, shift0, tabs):
    B, T, M = rows.B, rows.T, rows.M
    D = D_MODEL
    m9 = [rows.expand(mod[:, i * D:(i + 1) * D]) for i in range(N_ADA)]
    sh1, sc1, g1, sh2, sc2, g2, sh3, sc3, g3 = m9
    h = norm_mod(rows, x, P['norm_g'][l, 0], sh1, sc1)
    x = mm_resid(rows, mm_swiglu(rows, h, (W['ffn_in'], 2 * l)), (W['ffn_out'], 2 * l), x, g1, 0.5)
    h = norm_mod(rows, x, P['norm_g'][l, 1], sh2, sc2)
    z_all = matmul(h, (W['w_in'], l), rows.tm, name="mm_in")
    bias_row = jnp.pad(jnp.concatenate([P['mlstm_i_bias'][l], P['mlstm_f_bias'][l]]), (0, LANES - 2 * ML_HEADS))[None]
    om, C, n, m = mlstm(B, T, z_all, bias_row, P['mlstm_head_g'][l], C0, n0, m0)
    past = lat_past.shape[1]
    qr, lat, kr = mla_pre(rows, z_all, tabs[0], tabs[1], P['mla_kv_norm_g'][l])
    Sk = past + T
    if past:
        lat_all = jnp.concatenate([lat_past, lat.reshape(B, T, -1)], axis=1).reshape(B * Sk, -1)
        kr_all = jnp.concatenate([jnp.pad(kr_past, ((0, 0), (0, 0), (0, LANES - MLA_ROPE))), kr.reshape(B, T, -1)],
                                 axis=1).reshape(B * Sk, LANES)
    else:
        lat_all, kr_all = lat, kr
    if T <= LATENT_ATTN_MAX_T:
        oa = mla_attention_latent(B, T, Sk, past, z_all, qr, lat_all.astype(BF16), kr_all.astype(BF16), (W['kv_b'], l))
    else:
        kv = matmul(lat_all, (W['kv_b'], l), _row_tile(B * Sk, 1536), out_dtype=BF16, name="mm_kv")
        oa = mla_attention(B, T, Sk, past, z_all, qr, kv, kr_all.astype(BF16))
    r_, w_, k_, v_, a_, b_, g_, bonus = rwkv_pre(rows, z_all, shift0, P['rwkv_mu'][l], P['rwkv_w0'][l], P['rwkv_w2'][l],
                                                 P['rwkv_a0'][l], P['rwkv_a2'][l], P['rwkv_g2'][l], P['rwkv_k_k'][l],
                                                 P['rwkv_k_a'][l], P['rwkv_r_k'][l].reshape(-1))
    y, S = rwkv_scan(B, T, r_, w_, k_, v_, a_, b_, S0)
    orw = rwkv_post(rows, y, g_, bonus, P['rwkv_lnx_g'][l], P['rwkv_lnx_b'][l])
    z_last = z_all.reshape(B, T, NZ)[:, -1]
    shift = jnp.concatenate([z_last[:, OFF_RW_R:OFF_RW_R + 3 * RW_WIDTH], z_last[:, OFF_RW_LORA:OFF_RW_LORA + RW_LORA]],
                            axis=1)
    mixed = mm_branch(rows, om, oa, orw, (W['branch'], 3 * l), z_all, P['b_merge'][l][None])
    x = mm_resid(rows, mixed, (W['w_out'], l), x, g2, 1.0)
    h = norm_mod(rows, x, P['norm_g'][l, 2], sh3, sc3)
    x = mm_resid(rows, mm_swiglu(rows, h, (W['ffn_in'], 2 * l + 1)), (W['ffn_out'], 2 * l + 1), x, g3, 0.5)
    new = (lat.reshape(B, T, MLA_LATENT), kr.reshape(B, T, LANES)[:, :, :MLA_ROPE], C, n, m, S, shift)
    return x, new


def _trunk(rows, x, mods, W, P, lat_cache, kr_cache, C_st, n_st, m_st, S_st, sh_st):
    depth = len(mods)
    past = lat_cache.shape[2]
    tabs = _rope_tables(rows.T, past, rows.B)
    x = x.reshape(rows.M, D_MODEL)
    new = []
    for l in range(depth):
        x, st = _layer(rows, x, mods[l], l, W, P, lat_cache[l], kr_cache[l], C_st[l], n_st[l], m_st[l], S_st[l],
                       sh_st[l], tabs)
        new.append(st)
    y = final_norm(rows, x, P['final_norm_g']).reshape(rows.B, rows.T, D_MODEL)
    return y, tuple(jnp.stack(s, axis=0) for s in zip(*new))


def kernel(x_prompt, x_sample, c_prompt, c_sample, cache_mla_latent, cache_mla_krope, state_mlstm_C, state_mlstm_n,
           state_mlstm_m, state_rwkv_S, state_rwkv_shift, ada_w, ada_b, norm_g, ffn_w_in, ffn_w_out, w_in,
           mlstm_i_bias, mlstm_f_bias, mlstm_head_g, mla_kv_norm_g, mla_w_kv_b, rwkv_mu, rwkv_w0, rwkv_w2, rwkv_a0,
           rwkv_a2, rwkv_g2, rwkv_k_k, rwkv_k_a, rwkv_r_k, rwkv_lnx_g, rwkv_lnx_b, w_branch, b_merge, w_out,
           final_norm_g):
    P = dict(norm_g=norm_g, mlstm_i_bias=mlstm_i_bias, mlstm_f_bias=mlstm_f_bias, mlstm_head_g=mlstm_head_g,
             mla_kv_norm_g=mla_kv_norm_g, rwkv_mu=rwkv_mu, rwkv_w0=rwkv_w0, rwkv_w2=rwkv_w2, rwkv_a0=rwkv_a0,
             rwkv_a2=rwkv_a2, rwkv_g2=rwkv_g2, rwkv_k_k=rwkv_k_k, rwkv_k_a=rwkv_k_a, rwkv_r_k=rwkv_r_k,
             rwkv_lnx_g=rwkv_lnx_g, rwkv_lnx_b=rwkv_lnx_b, b_merge=b_merge, final_norm_g=final_norm_g)
    depth = w_in.shape[0]
    W = dict(ffn_in=cast_blocks(ffn_w_in), ffn_out=cast_blocks(ffn_w_out), w_in=prep_w_in(w_in),
             kv_b=cast_blocks(mla_w_kv_b), branch=cast_blocks(w_branch), w_out=cast_blocks(w_out))
    Bp, Tp, _ = x_prompt.shape
    Bs, Ts, _ = x_sample.shape
    n_c = Bp + Bs
    c_all = jnp.pad(jnp.concatenate([c_prompt, c_sample], axis=0), ((0, (-n_c) % 8), (0, 0)))
    mods = [ada_mod(c_all, ada_w[l], ada_b[l]) for l in range(depth)]
    rows_p = Rows(Bp, Tp, TM_MAX)
    rows_s = Rows(Bs, Ts, TM_MAX)
    zeros = lambda *s: jnp.zeros((depth, Bp) + s, F32)
    y_p, st_p = _trunk(rows_p, x_prompt, [m[:Bp] for m in mods], W, P,
                       zeros(0, MLA_LATENT), zeros(0, MLA_ROPE), zeros(ML_HEADS, ML_DV, ML_DQK), zeros(ML_HEADS, ML_DQK),
                       zeros(ML_HEADS), zeros(RW_HEADS, RW_HEAD, RW_HEAD), zeros(RW_COLS))
    y_s, st_s = _trunk(rows_s, x_sample, [m[Bp:n_c] for m in mods], W, P,
                       cache_mla_latent, cache_mla_krope, state_mlstm_C, state_mlstm_n, state_mlstm_m, state_rwkv_S,
                       state_rwkv_shift)
    return (y_p, y_s) + st_p + st_s
```

```python
import functools

import jax
import jax.numpy as jnp
import numpy as np
from jax import lax
from jax.experimental import pallas as pl
from jax.experimental.pallas import tpu as pltpu

F32 = jnp.float32
BF16 = jnp.bfloat16
HIGHEST = lax.Precision.HIGHEST

D_MODEL = 2048
CHUNK = 64
NORM_EPS = 1e-6
D_FF = 5632
N_ADA = 9
ML_HEADS, ML_DQK, ML_DV, ML_GATE_CAP = 4, 128, 256, 15.0
ML_CHUNK = 256
MLA_HEADS, MLA_NOPE, MLA_ROPE, MLA_V, MLA_LATENT = 8, 128, 64, 128, 512
MLA_SCALE = (MLA_NOPE + MLA_ROPE) ** -0.5
LATENT_ATTN_MAX_T = 64
ROPE_BASE = 10000.0
RW_HEADS, RW_HEAD = 16, 64
RW_WIDTH = RW_HEADS * RW_HEAD
RW_LN_EPS = 64e-5
RW_LORA = 256
RW_COLS = 3 * RW_WIDTH + RW_LORA
BRANCH_WIDTH = 1024

LANES = 128
VMEM_LIMIT = 56 * 1024 * 1024
TM_MAX = 1024
TN = 512
TN_IN = 1024

OFF_ML_Q, OFF_ML_K, OFF_ML_V, OFF_ML_O = 0, 512, 1024, 2048
OFF_QN, OFF_QR = 3072, 4096
OFF_RW_R, OFF_RW_K, OFF_RW_V = 5120, 6144, 7168
OFF_GATE = 8192
OFF_LAT = 14336
OFF_RW_LORA = 14848
OFF_KR = 15104
OFF_IF = 15232
NZ = 15360
IN_COLS = 14664


def _cparams(sem):
    return pltpu.CompilerParams(dimension_semantics=sem, vmem_limit_bytes=VMEM_LIMIT)


def _sigmoid(x):
    return 1.0 / (1.0 + jnp.exp(-x))


def _silu(x):
    return x * _sigmoid(x)


def _dot(a, b):
    return jnp.dot(a, b, preferred_element_type=F32)


def _dot_nt(a, b, precision=None):
    return lax.dot_general(a, b, (((1,), (1,)), ((), ())), precision=precision, preferred_element_type=F32)


def _dot_tn(a, b, precision=None):
    return lax.dot_general(a, b, (((0,), (0,)), ((), ())), precision=precision, preferred_element_type=F32)


class Rows:
    def __init__(self, B, T, tm_max):
        self.B, self.T, self.M = B, T, B * T
        if T >= 256:
            self.tm = min(tm_max, T)
            assert T % self.tm == 0
            self.tpg = T // self.tm
            self.R = 1
        else:
            self.tm = self.M
            self.tpg = 1
            self.R = self.M
        self.nm = self.M // self.tm

    def expand(self, v):
        if self.R == 1:
            return v[:, None, :]
        return jnp.repeat(v, self.T, axis=0)[None]

    def vec_spec(self, tn, col=lambda j: j):
        return pl.BlockSpec((None, self.R, tn), lambda i, j: (i // self.tpg, 0, col(j)))


def _ada_kernel(c_ref, w_ref, b_ref, o_ref):
    c = c_ref[...]
    o_ref[...] = _dot(_silu(c).astype(BF16), w_ref[...].astype(BF16)) + b_ref[...]


def ada_mod(c, w, b, l):
    R, D = c.shape
    N = w.shape[2]
    tn = 1024
    return pl.pallas_call(
        _ada_kernel,
        grid=(N // tn,),
        in_specs=[pl.BlockSpec((R, D), lambda j: (0, 0)),
                  pl.BlockSpec((None, D, tn), lambda j: (l, 0, j)),
                  pl.BlockSpec((1, tn), lambda j: (0, j))],
        out_specs=pl.BlockSpec((R, tn), lambda j: (0, j)),
        out_shape=jax.ShapeDtypeStruct((R, N), F32),
        compiler_params=_cparams(("arbitrary",)),
        name="ada_mod",
    )(c, w, b[l].reshape(1, N))


def _final_norm_kernel(x_ref, g_ref, o_ref):
    x = x_ref[...]
    o_ref[...] = x * lax.rsqrt(jnp.mean(x * x, axis=-1, keepdims=True) + NORM_EPS) * g_ref[...]


def final_norm(rows, x, g):
    D = x.shape[1]
    tm = min(rows.tm, 512)
    return pl.pallas_call(
        _final_norm_kernel,
        grid=(rows.M // tm,),
        in_specs=[pl.BlockSpec((tm, D), lambda i: (i, 0)), pl.BlockSpec((1, D), lambda i: (0, 0))],
        out_specs=pl.BlockSpec((tm, D), lambda i: (i, 0)),
        out_shape=jax.ShapeDtypeStruct((rows.M, D), F32),
        compiler_params=_cparams(("parallel",)),
        name="final_norm",
    )(x, g.reshape(1, D))


def _mm_kernel(a_ref, w_ref, o_ref):
    o_ref[...] = _dot(a_ref[...].astype(BF16), w_ref[...]).astype(o_ref.dtype)


def matmul(a, w, tm, out_dtype=F32, name="matmul"):
    M, K = a.shape
    w, g = w
    _, nj, _, tn = w.shape
    N = nj * tn
    return pl.pallas_call(
        _mm_kernel,
        grid=(M // tm, nj),
        in_specs=[pl.BlockSpec((tm, K), lambda i, j: (i, 0)),
                  pl.BlockSpec((None, None, K, tn), lambda i, j: (g, j, 0, 0))],
        out_specs=pl.BlockSpec((tm, tn), lambda i, j: (i, j)),
        out_shape=jax.ShapeDtypeStruct((M, N), out_dtype),
        compiler_params=_cparams(("parallel", "arbitrary")),
        name=name,
    )(a, w)


def _normed_rows(x_ref, g_ref, sh_ref, sc_ref, h_sc):
    @pl.when(pl.program_id(1) == 0)
    def _():
        x = x_ref[...]
        y = x * lax.rsqrt(jnp.mean(x * x, axis=-1, keepdims=True) + NORM_EPS) * g_ref[...]
        h_sc[...] = (y * (1.0 + sc_ref[...]) + sh_ref[...]).astype(h_sc.dtype)


def _norm_specs(rows, D):
    vec = (pl.BlockSpec((None, 1, D), lambda i, j: (i // rows.tpg, 0, 0)) if rows.R == 1
           else pl.BlockSpec((None, rows.tm, D), lambda i, j: (0, i, 0)))
    return [pl.BlockSpec((rows.tm, D), lambda i, j: (i, 0)), pl.BlockSpec((1, D), lambda i, j: (0, 0)), vec, vec]


def _mm_norm_kernel(x_ref, g_ref, sh_ref, sc_ref, w_ref, o_ref, h_sc):
    _normed_rows(x_ref, g_ref, sh_ref, sc_ref, h_sc)
    o_ref[...] = _dot(h_sc[...], w_ref[...]).astype(o_ref.dtype)


def mm_norm(rows, x, g, shift, scale, w, name):
    D = x.shape[1]
    w, gi = w
    _, nj, _, tn = w.shape
    return pl.pallas_call(
        _mm_norm_kernel,
        grid=(rows.nm, nj),
        in_specs=_norm_specs(rows, D) + [pl.BlockSpec((None, None, D, tn), lambda i, j: (gi, j, 0, 0))],
        out_specs=pl.BlockSpec((rows.tm, tn), lambda i, j: (i, j)),
        out_shape=jax.ShapeDtypeStruct((rows.M, nj * tn), F32),
        scratch_shapes=[pltpu.VMEM((rows.tm, D), BF16)],
        compiler_params=_cparams(("parallel", "arbitrary")),
        name=name,
    )(x, g.reshape(1, D), shift, scale, w)


def _swiglu_kernel(x_ref, g_ref, sh_ref, sc_ref, wg_ref, wu_ref, o_ref, h_sc):
    _normed_rows(x_ref, g_ref, sh_ref, sc_ref, h_sc)
    h = h_sc[...]
    g = _dot(h, wg_ref[...])
    u = _dot(h, wu_ref[...])
    o_ref[...] = (_silu(g) * u).astype(o_ref.dtype)


def mm_swiglu(rows, x, g, shift, scale, w_in):
    D = x.shape[1]
    w_in, gi = w_in
    tn = w_in.shape[3]
    nj = D_FF // tn
    return pl.pallas_call(
        _swiglu_kernel,
        grid=(rows.nm, nj),
        in_specs=_norm_specs(rows, D) + [pl.BlockSpec((None, None, D, tn), lambda i, j: (gi, j, 0, 0)),
                                         pl.BlockSpec((None, None, D, tn), lambda i, j: (gi, j + nj, 0, 0))],
        out_specs=pl.BlockSpec((rows.tm, tn), lambda i, j: (i, j)),
        out_shape=jax.ShapeDtypeStruct((rows.M, D_FF), BF16),
        scratch_shapes=[pltpu.VMEM((rows.tm, D), BF16)],
        compiler_params=_cparams(("parallel", "arbitrary")),
        name="mm_swiglu",
    )(x, g.reshape(1, D), shift, scale, w_in, w_in)


def _resid_kernel(a_ref, w_ref, x_ref, g_ref, o_ref, *, coef):
    o_ref[...] = x_ref[...] + (coef * g_ref[...]) * _dot(a_ref[...], w_ref[...])


def mm_resid(rows, a, w, x, gate, coef):
    K = a.shape[1]
    w, g = w
    _, nj, _, tn = w.shape
    N = nj * tn
    return pl.pallas_call(
        functools.partial(_resid_kernel, coef=coef),
        grid=(rows.nm, nj),
        in_specs=[pl.BlockSpec((rows.tm, K), lambda i, j: (i, 0)),
                  pl.BlockSpec((None, None, K, tn), lambda i, j: (g, j, 0, 0)),
                  pl.BlockSpec((rows.tm, tn), lambda i, j: (i, j)),
                  rows.vec_spec(tn)],
        out_specs=pl.BlockSpec((rows.tm, tn), lambda i, j: (i, j)),
        out_shape=jax.ShapeDtypeStruct((rows.M, N), F32),
        compiler_params=_cparams(("parallel", "arbitrary")),
        name="mm_resid",
    )(a, w, x, gate)


def _branch_kernel(om_ref, oa_ref, or_ref, w0_ref, w1_ref, w2_ref, z0_ref, z1_ref, z2_ref, b0_ref, b1_ref, b2_ref,
                   o_ref):
    acc = _sigmoid(z0_ref[...] + b0_ref[...]) * _dot(om_ref[...], w0_ref[...])
    acc += _sigmoid(z1_ref[...] + b1_ref[...]) * _dot(oa_ref[...], w1_ref[...])
    acc += _sigmoid(z2_ref[...] + b2_ref[...]) * _dot(or_ref[...], w2_ref[...])
    o_ref[...] = acc.astype(o_ref.dtype)


def mm_branch(rows, om, oa, orw, w_branch, z_all, b_merge):
    w_branch, g = w_branch
    tn = w_branch.shape[3]
    D = D_MODEL
    nj = D // tn
    a_spec = pl.BlockSpec((rows.tm, BRANCH_WIDTH), lambda i, j: (i, 0))
    w_specs = [pl.BlockSpec((None, None, BRANCH_WIDTH, tn), lambda i, j, n=n: (g + n, j, 0, 0)) for n in range(3)]
    z_specs = [pl.BlockSpec((rows.tm, tn), lambda i, j, n=n: (i, (OFF_GATE + n * D) // tn + j)) for n in range(3)]
    b_specs = [pl.BlockSpec((1, tn), lambda i, j, n=n: (0, n * D // tn + j)) for n in range(3)]
    return pl.pallas_call(
        _branch_kernel,
        grid=(rows.nm, nj),
        in_specs=[a_spec, a_spec, a_spec] + w_specs + z_specs + b_specs,
        out_specs=pl.BlockSpec((rows.tm, tn), lambda i, j: (i, j)),
        out_shape=jax.ShapeDtypeStruct((rows.M, D), BF16),
        compiler_params=_cparams(("parallel", "arbitrary")),
        name="mm_branch",
    )(om, oa, orw, w_branch, w_branch, w_branch, z_all, z_all, z_all, b_merge, b_merge, b_merge)


def _mlstm_kernel(q_ref, k_ref, v_ref, o_ref, if_ref, bias_ref, hg_ref, C0_ref, n0_ref, m0_ref,
                  om_ref, C_ref, n_ref, m_ref, *, L):
    c = pl.program_id(1)

    @pl.when(c == 0)
    def _():
        C_ref[...] = C0_ref[...]
        n_ref[...] = n0_ref[...]
        m_ref[...] = m0_ref[...]

    H = ML_HEADS
    sc = ML_GATE_CAP * jnp.tanh((if_ref[...] + bias_ref[...]) * (1.0 / ML_GATE_CAP))
    lf = jnp.minimum(sc, 0.0) - jnp.log(1.0 + jnp.exp(-jnp.abs(sc)))
    row = lax.broadcasted_iota(jnp.int32, (L, L), 0)
    col = lax.broadcasted_iota(jnp.int32, (L, L), 1)
    causal = row >= col
    bcum = jnp.dot(causal.astype(F32), lf, precision=HIGHEST, preferred_element_type=F32)
    sel = (lax.broadcasted_iota(jnp.int32, (8, LANES), 0) == lax.broadcasted_iota(jnp.int32, (8, LANES), 1)).astype(F32)
    scT = _dot_nt(sel, sc, HIGHEST)
    bT = _dot_nt(sel, bcum, HIGHEST)
    hs = range(H)
    each = lambda f, *lists: [f(*xs) for xs in zip(*lists)]
    qsl = [slice(h * ML_DQK, (h + 1) * ML_DQK) for h in hs]
    vsl = [slice(h * ML_DV, (h + 1) * ML_DV) for h in hs]
    b_col = [bcum[:, H + h:H + h + 1] for h in hs]
    b_row = [bT[H + h:H + h + 1, :] for h in hs]
    ig_row = [scT[h:h + 1, :] for h in hs]
    ig_col = [sc[:, h:h + 1] for h in hs]
    m_prev = [m_ref[0, h, :, 0:1] for h in hs]
    C = [C_ref[0, h] for h in hs]
    n = [n_ref[0, h] for h in hs]
    qf = [q_ref[:, s] for s in qsl]
    q = each(lambda x: x.astype(BF16), qf)
    kb = [(k_ref[:, s] * (ML_DQK ** -0.5)).astype(BF16) for s in qsl]
    v = [v_ref[:, s] for s in vsl]
    logD = each(lambda bc, br, ir: jnp.where(causal, bc - br + ir, -jnp.inf), b_col, b_row, ig_row)
    g = each(lambda bc, mp: bc + mp, b_col, m_prev)
    m_t = each(lambda gg, ld: jnp.maximum(gg, jnp.max(ld, axis=1, keepdims=True)), g, logD)
    Dm = each(lambda ld, mt: jnp.exp(ld - mt), logD, m_t)
    inter = each(lambda gg, mt: jnp.exp(gg - mt), g, m_t)
    qk = each(lambda qq, kk, d: _dot_nt(qq, kk) * d, q, kb, Dm)
    qC = each(lambda qq, cc: _dot_nt(qq, cc.astype(BF16)), q, C)
    num = each(lambda s, vv, it, x: _dot(s.astype(BF16), vv.astype(BF16)) + it * x, qk, v, inter, qC)
    qn = each(lambda x, nn: jnp.sum(x * nn, axis=1, keepdims=True), qf, n)
    den = each(lambda s, it, x: jnp.sum(s, axis=1, keepdims=True) + it * x, qk, inter, qn)
    hh = each(lambda nu, de, mt: nu / jnp.maximum(jnp.abs(de), jnp.exp(-mt)), num, den, m_t)
    m_new = each(lambda mt: mt[L - 1:L, :], m_t)
    b_last = each(lambda bc: bc[L - 1:L, :], b_col)
    carry_w = each(lambda bl, mp, mn: jnp.exp(bl + mp - mn), b_last, m_prev, m_new)
    w_s = each(lambda bl, bc, ic, mn: jnp.exp(bl - bc + ic - mn), b_last, b_col, ig_col, m_new)
    dC = each(lambda w, vv, kk: _dot_tn((w * vv).astype(BF16), kk), w_s, v, kb)
    dn = each(lambda w, kk: jnp.sum(w * kk.astype(F32), axis=0, keepdims=True), w_s, kb)
    for h in hs:
        y = hh[h] * lax.rsqrt(jnp.mean(hh[h] * hh[h], axis=-1, keepdims=True) + NORM_EPS) * hg_ref[:, vsl[h]]
        om_ref[:, vsl[h]] = (_sigmoid(o_ref[:, vsl[h]]) * y).astype(om_ref.dtype)
        C_ref[0, h] = carry_w[h] * C[h] + dC[h]
        n_ref[0, h] = carry_w[h] * n[h] + dn[h]
        m_ref[0, h] = jnp.broadcast_to(m_new[h], (1, LANES))


def mlstm(B, T, z_all, bias_row, head_g, C0, n0, m0):
    L = next((c for c in (ML_CHUNK, CHUNK) if T % c == 0), T)
    nc = T // L
    H = ML_HEADS
    zrow = lambda w, off: pl.BlockSpec((L, w), lambda b, c: (b * nc + c, off // w))
    st = lambda *shape: pl.BlockSpec((1,) + shape, lambda b, c: (b,) + (0,) * len(shape))
    om, C, n, m = pl.pallas_call(
        functools.partial(_mlstm_kernel, L=L),
        grid=(B, nc),
        in_specs=[zrow(H * ML_DQK, OFF_ML_Q), zrow(H * ML_DQK, OFF_ML_K), zrow(H * ML_DV, OFF_ML_V),
                  zrow(H * ML_DV, OFF_ML_O), zrow(LANES, OFF_IF),
                  pl.BlockSpec((1, LANES), lambda b, c: (0, 0)),
                  pl.BlockSpec((1, H * ML_DV), lambda b, c: (0, 0)),
                  st(H, ML_DV, ML_DQK), st(H, 1, ML_DQK), st(H, 1, LANES)],
        out_specs=[pl.BlockSpec((L, H * ML_DV), lambda b, c: (b * nc + c, 0)),
                   st(H, ML_DV, ML_DQK), st(H, 1, ML_DQK), st(H, 1, LANES)],
        out_shape=[jax.ShapeDtypeStruct((B * T, H * ML_DV), BF16),
                   jax.ShapeDtypeStruct((B, H, ML_DV, ML_DQK), F32),
                   jax.ShapeDtypeStruct((B, H, 1, ML_DQK), F32),
                   jax.ShapeDtypeStruct((B, H, 1, LANES), F32)],
        compiler_params=_cparams(("parallel", "arbitrary")),
        name="mlstm",
    )(z_all, z_all, z_all, z_all, z_all, bias_row, head_g.reshape(1, -1), C0, n0.reshape(B, H, 1, ML_DQK),
      jnp.broadcast_to(m0[:, :, None, None], (B, H, 1, LANES)))
    return om, C, n.reshape(B, H, ML_DQK), m[:, :, 0, 0]


def _rope_lanes(x, cos, sin):
    swapped = pltpu.roll(x, 32, 1) + pltpu.roll(x, 96, 1)
    return x * cos + swapped * sin


def _mla_pre_kernel(qr_ref, lat_ref, kr_ref, cos_ref, sin_ref, g_ref, qro_ref, lato_ref, kro_ref):
    cos = cos_ref[...]
    sin = sin_ref[...]
    for h in range(MLA_HEADS):
        sl = slice(h * LANES, (h + 1) * LANES)
        qro_ref[:, sl] = (_rope_lanes(qr_ref[:, sl], cos, sin) * MLA_SCALE).astype(qro_ref.dtype)
    kro_ref[...] = _rope_lanes(kr_ref[...], cos, sin)
    lat = lat_ref[...]
    lato_ref[...] = lat * lax.rsqrt(jnp.mean(lat * lat, axis=-1, keepdims=True) + NORM_EPS) * g_ref[...]


def mla_pre(rows, z_all, cos_tab, sin_tab, kv_norm_g):
    tm = min(rows.tm, 256)
    M = rows.M
    zspec = lambda w, off: pl.BlockSpec((tm, w), lambda i: (i, off // w))
    W = MLA_HEADS * LANES
    return pl.pallas_call(
        _mla_pre_kernel,
        grid=(M // tm,),
        in_specs=[zspec(W, OFF_QR), zspec(MLA_LATENT, OFF_LAT), zspec(LANES, OFF_KR),
                  pl.BlockSpec((tm, LANES), lambda i: (i, 0)), pl.BlockSpec((tm, LANES), lambda i: (i, 0)),
                  pl.BlockSpec((1, MLA_LATENT), lambda i: (0, 0))],
        out_specs=[pl.BlockSpec((tm, W), lambda i: (i, 0)), pl.BlockSpec((tm, MLA_LATENT), lambda i: (i, 0)),
                   pl.BlockSpec((tm, LANES), lambda i: (i, 0))],
        out_shape=[jax.ShapeDtypeStruct((M, W), BF16), jax.ShapeDtypeStruct((M, MLA_LATENT), F32),
                   jax.ShapeDtypeStruct((M, LANES), F32)],
        compiler_params=_cparams(("parallel",)),
        name="mla_pre",
    )(z_all, z_all, z_all, cos_tab, sin_tab, kv_norm_g.reshape(1, -1))


def _attn_kernel(qn_ref, qr_ref, kn_ref, kr_ref, v_ref, o_ref, m_sc, l_sc, acc_sc, *, tq, tk, past, nk):
    qi = pl.program_id(2)
    q = jnp.concatenate([(qn_ref[...] * MLA_SCALE).astype(BF16), qr_ref[...]], axis=1)
    m_sc[...] = jnp.full_like(m_sc, -jnp.inf)
    l_sc[...] = jnp.zeros_like(l_sc)
    acc_sc[...] = jnp.zeros_like(acc_sc)
    q0 = past + qi * tq
    n_full = jnp.minimum(((q0 // CHUNK + 1) * CHUNK) // tk, nk)
    n_vis = jnp.minimum((((q0 + tq - 1) // CHUNK + 1) * CHUNK + tk - 1) // tk, nk)

    def tile(kj, masked):
        k0 = pl.multiple_of(kj * tk, tk)
        k = jnp.concatenate([kn_ref[pl.ds(k0, tk), :], kr_ref[pl.ds(k0, tk), :]], axis=1)
        s = _dot_nt(q, k)
        if masked:
            q_chunk = (q0 + lax.broadcasted_iota(jnp.int32, (tq, 1), 0)) // CHUNK
            k_chunk = (k0 + lax.broadcasted_iota(jnp.int32, (1, tk), 1)) // CHUNK
            s = jnp.where(k_chunk <= q_chunk, s, -jnp.inf)
        m_prev = m_sc[...]
        m_new = jnp.maximum(m_prev, jnp.max(s, axis=1, keepdims=True))
        alpha = jnp.exp(m_prev - m_new)
        p = jnp.exp(s - m_new)
        l_sc[...] = alpha * l_sc[...] + jnp.sum(p, axis=1, keepdims=True)
        acc_sc[...] = alpha * acc_sc[...] + _dot(p.astype(BF16), v_ref[pl.ds(k0, tk), :])
        m_sc[...] = m_new

    def full_tile(kj, c):
        tile(kj, False)
        return c

    def edge_tile(kj, c):
        tile(kj, True)
        return c

    lax.fori_loop(0, n_full, full_tile, 0)
    lax.fori_loop(n_full, n_vis, edge_tile, 0)
    o_ref[...] = (acc_sc[...] / l_sc[...]).astype(o_ref.dtype)


def mla_attention(B, T, Sk, past, z_all, qr, kv, kr_all):
    H = MLA_HEADS
    tq = min(T, 512)
    tk = next((t for t in (2048, 1024, 512) if Sk % t == 0), Sk)
    nq, nk = T // tq, Sk // tk
    z3 = z_all.reshape(B, T, NZ)
    qr3 = qr.reshape(B, T, H * LANES)
    kv3 = kv.reshape(B, Sk, H * 2 * LANES)
    kr3 = kr_all.reshape(B, Sk, LANES)
    out = pl.pallas_call(
        functools.partial(_attn_kernel, tq=tq, tk=tk, past=past, nk=nk),
        grid=(B, H, nq),
        in_specs=[pl.BlockSpec((None, tq, LANES), lambda b, h, qi: (b, qi, OFF_QN // LANES + h)),
                  pl.BlockSpec((None, tq, LANES), lambda b, h, qi: (b, qi, h)),
                  pl.BlockSpec((None, Sk, LANES), lambda b, h, qi: (b, 0, 2 * h)),
                  pl.BlockSpec((None, Sk, LANES), lambda b, h, qi: (b, 0, 0)),
                  pl.BlockSpec((None, Sk, LANES), lambda b, h, qi: (b, 0, 2 * h + 1))],
        out_specs=pl.BlockSpec((None, tq, LANES), lambda b, h, qi: (b, qi, h)),
        out_shape=jax.ShapeDtypeStruct((B, T, H * MLA_V), BF16),
        scratch_shapes=[pltpu.VMEM((tq, 1), F32), pltpu.VMEM((tq, 1), F32), pltpu.VMEM((tq, MLA_V), F32)],
        compiler_params=_cparams(("parallel", "parallel", "arbitrary")),
        name="mla_attention",
    )(z3, qr3, kv3, kr3, kv3)
    return out.reshape(B * T, H * MLA_V)


def _attn_latent_kernel(qn_ref, qr_ref, lat_ref, kr_ref, w_ref, o_ref, *, T, Sk, past):
    lat = lat_ref[...]
    kr = kr_ref[...]
    q_chunk = (past + lax.broadcasted_iota(jnp.int32, (T, 1), 0)) // CHUNK
    k_chunk = lax.broadcasted_iota(jnp.int32, (1, Sk), 1) // CHUNK
    visible = k_chunk <= q_chunk
    per_block = w_ref.shape[2] // (MLA_NOPE + MLA_V)
    hs = range(MLA_HEADS)
    sls = [slice(h * LANES, (h + 1) * LANES) for h in hs]
    c0 = [(h % per_block) * (MLA_NOPE + MLA_V) for h in hs]
    wk = [w_ref[h // per_block, :, c0[h]:c0[h] + MLA_NOPE] for h in hs]
    wv = [w_ref[h // per_block, :, c0[h] + MLA_NOPE:c0[h] + MLA_NOPE + MLA_V] for h in hs]
    qa = [_dot_nt((qn_ref[:, sls[h]] * MLA_SCALE).astype(BF16), wk[h]).astype(BF16) for h in hs]
    s = [jnp.where(visible, _dot_nt(qa[h], lat) + _dot_nt(qr_ref[:, sls[h]], kr), -jnp.inf) for h in hs]
    p = [jnp.exp(x - jnp.max(x, axis=1, keepdims=True)) for x in s]
    ctx = [_dot(x.astype(BF16), lat) / jnp.sum(x, axis=1, keepdims=True) for x in p]
    for h in hs:
        o_ref[:, sls[h]] = _dot(ctx[h].astype(BF16), wv[h]).astype(o_ref.dtype)


def mla_attention_latent(B, T, Sk, past, z_all, qr, lat_all, kr_all, w_kv):
    H = MLA_HEADS
    w, g = w_kv
    _, nj, K, tn = w.shape
    W = H * LANES
    out = pl.pallas_call(
        functools.partial(_attn_latent_kernel, T=T, Sk=Sk, past=past),
        grid=(B,),
        in_specs=[pl.BlockSpec((None, T, W), lambda b: (b, 0, OFF_QN // W)),
                  pl.BlockSpec((None, T, W), lambda b: (b, 0, 0)),
                  pl.BlockSpec((None, Sk, MLA_LATENT), lambda b: (b, 0, 0)),
                  pl.BlockSpec((None, Sk, LANES), lambda b: (b, 0, 0)),
                  pl.BlockSpec((None, nj, K, tn), lambda b: (g, 0, 0, 0))],
        out_specs=pl.BlockSpec((None, T, W), lambda b: (b, 0, 0)),
        out_shape=jax.ShapeDtypeStruct((B, T, H * MLA_V), BF16),
        compiler_params=_cparams(("parallel",)),
        name="mla_attention_latent",
    )(z_all.reshape(B, T, NZ), qr.reshape(B, T, W), lat_all.reshape(B, Sk, MLA_LATENT), kr_all.reshape(B, Sk, LANES), w)
    return out.reshape(B * T, H * MLA_V)


def _pair_sum_matrix():
    r = lax.broadcasted_iota(jnp.int32, (LANES, LANES), 0) // RW_HEAD
    c = lax.broadcasted_iota(jnp.int32, (LANES, LANES), 1) // RW_HEAD
    return (r == c).astype(BF16)


def _head_sum(x, bd):
    m = x.shape[0]
    parts = []
    for p in range(RW_WIDTH // LANES):
        hi, lo = _split_bf16(x[:, p * LANES:(p + 1) * LANES])
        both = _dot(jnp.concatenate([hi, lo], axis=0), bd)
        parts.append(both[:m] + both[m:])
    return jnp.concatenate(parts, axis=1)


def _rwkv_pre_kernel(r_ref, k_ref, v_ref, lo_ref, tr_ref, tk_ref, tv_ref, tlo_ref, fr_ref, fk_ref, fv_ref, flo_ref,
                     mu_ref, mulo_ref, w0_ref, w2_ref, a0_ref, a2_ref, g2_ref, kk_ref, ka_ref, rk_ref,
                     ro_ref, wo_ref, ko_ref, vo_ref, ao_ref, bo_ref, go_ref, bonus_ref, *, T):
    W = RW_WIDTH
    tm = r_ref.shape[0]
    row = lax.broadcasted_iota(jnp.int32, (tm, 1), 0)
    stream_start = (pl.program_id(0) * tm + row) % T == 0

    def mix(z_ref, tail_ref, first_ref, mu):
        z = z_ref[...]
        prev = jnp.where(row == 0, tail_ref[7:8, :], pltpu.roll(z, 1, 0))
        prev = jnp.where(stream_start, first_ref[...], prev)
        return z + (prev - z) * mu

    r = mix(r_ref, tr_ref, fr_ref, mu_ref[:, 0:W])
    k = mix(k_ref, tk_ref, fk_ref, mu_ref[:, W:2 * W])
    v = mix(v_ref, tv_ref, fv_ref, mu_ref[:, 2 * W:3 * W])
    lo = mix(lo_ref, tlo_ref, flo_ref, mulo_ref[...])
    wd, ad, gd = lo[:, 0:64], lo[:, 64:128], lo[:, 128:256]
    hdot = lambda x, w: _mm(x, w, "nn", 3)
    wpre = w0_ref[...] + hdot(jnp.tanh(wd), w2_ref[...])
    w_raw = jnp.minimum(wpre, 0.0) - jnp.log(1.0 + jnp.exp(-jnp.abs(wpre))) - 0.5
    log_decay = -jnp.exp(w_raw)
    a = _sigmoid(a0_ref[...] + hdot(ad, a2_ref[...]))
    g = hdot(_sigmoid(gd), g2_ref[...])
    bd = _pair_sum_matrix()
    kk = k * kk_ref[...]
    kk = kk / jnp.maximum(jnp.sqrt(_head_sum(kk * kk, bd)), 1e-12)
    k = k * (1.0 + (a - 1.0) * ka_ref[...])
    ro_ref[...] = r
    wo_ref[...] = log_decay
    ko_ref[...] = k
    vo_ref[...] = v
    ao_ref[...] = -kk
    bo_ref[...] = kk * a
    go_ref[...] = g
    bonus_ref[...] = _head_sum(r * k * rk_ref[...], bd) * v


def rwkv_pre(rows, z_all, shift0, mu, w0, w2, a0, a2, g2, k_k, k_a, r_k):
    M = rows.M
    tm = min(rows.tm, 256)
    W = RW_WIDTH
    zs = lambda w, off: pl.BlockSpec((tm, w), lambda i: (i, off // w))
    tail = lambda w, off: pl.BlockSpec((8, w), lambda i: (jnp.maximum(i * (tm // 8) - 1, 0), off // w))
    if rows.R == 1:
        first = lambda w: pl.BlockSpec((None, 1, w), lambda i: (i // (rows.T // tm), 0, 0))
    else:
        first = lambda w: pl.BlockSpec((None, tm, w), lambda i: (0, i, 0))
    full = lambda a: pl.BlockSpec(a.shape, lambda i: (0,) * a.ndim)
    row = lambda a: a.reshape(1, -1)
    firsts = [rows.expand(shift0[:, o:o + w]) for o, w in ((0, W), (W, W), (2 * W, W), (3 * W, RW_LORA))]
    params = [row(mu[:3 * W]), row(mu[3 * W:]), row(w0), w2, row(a0), a2, g2, row(k_k), row(k_a), row(r_k)]
    cols = ((W, OFF_RW_R), (W, OFF_RW_K), (W, OFF_RW_V), (RW_LORA, OFF_RW_LORA))
    outs = pl.pallas_call(
        functools.partial(_rwkv_pre_kernel, T=rows.T),
        grid=(M // tm,),
        in_specs=[zs(*c) for c in cols] + [tail(*c) for c in cols] + [first(c[0]) for c in cols]
        + [full(p) for p in params],
        out_specs=[pl.BlockSpec((tm, W), lambda i: (i, 0))] * 8,
        out_shape=[jax.ShapeDtypeStruct((M, W), F32)] * 8,
        compiler_params=_cparams(("parallel",)),
        name="rwkv_pre",
    )(*([z_all] * 8), *firsts, *params)
    return outs


_DIMS = {"nn": (((1,), (0,)), ((), ())), "nt": (((1,), (1,)), ((), ())), "tn": (((0,), (0,)), ((), ()))}


def _split_bf16(x):
    hi = x.astype(BF16)
    return hi, (x - hi.astype(F32)).astype(BF16)


def _mm(a, b, kind, passes):
    dims = _DIMS[kind]
    if passes == 6:
        return lax.dot_general(a, b, dims, precision=HIGHEST, preferred_element_type=F32)
    dg = lambda x, y: lax.dot_general(x, y, dims, preferred_element_type=F32)
    if passes == 1:
        return dg(a.astype(BF16), b.astype(BF16))
    ah, al = _split_bf16(a)
    bh, bl = _split_bf16(b)
    m = a.shape[0]
    if kind == "tn" or m < RW_HEAD:
        return dg(ah, bh) + (dg(al, bh) + dg(ah, bl))
    both = dg(jnp.concatenate([ah, al], axis=0), bh)
    return both[:m] + (both[m:] + dg(ah, bl))


def _rwkv_chunk_kernel(r_ref, lw_ref, k_ref, v_ref, a_ref, b_ref, S0_ref, y_ref, S_ref, *, L, P_G, P_T, P_X, P_S):
    c = pl.program_id(1)

    @pl.when(c == 0)
    def _():
        S_ref[...] = S0_ref[...]

    N = RW_HEAD
    row = lax.broadcasted_iota(jnp.int32, (L, L), 0)
    col = lax.broadcasted_iota(jnp.int32, (L, L), 1)
    incl = row >= col
    strict = row > col
    eye = (row == col).astype(F32)
    tri = incl.astype(F32)
    cum_all = jnp.dot(tri, lw_ref[...], precision=HIGHEST, preferred_element_type=F32)
    sls = [slice(h * N, (h + 1) * N) for h in range(RW_HEADS)]
    each = lambda f, *lists: [f(*xs) for xs in zip(*lists)]
    S0 = [S_ref[0, h] for h in range(RW_HEADS)]
    lw = [lw_ref[:, s] for s in sls]
    cum = [cum_all[:, s] for s in sls]
    V = [v_ref[:, s] for s in sls]
    g_in = each(jnp.exp, cum)
    g_inv = each(lambda c: jnp.exp(-c), cum)
    At = each(lambda s, c, w: a_ref[:, s] * jnp.exp(c - w), sls, cum, lw)
    Bt = each(lambda s, g: b_ref[:, s] * g, sls, g_inv)
    Kt = each(lambda s, g: k_ref[:, s] * g, sls, g_inv)
    Rt = each(lambda s, g: r_ref[:, s] * g, sls, g_in)
    right = each(lambda x, y: jnp.concatenate([x, y], axis=0), Bt, Kt)
    G = each(lambda x, y, rt: _mm(jnp.concatenate([x, y], axis=0), rt, "nt", P_G), At, Rt, right)
    Nab = each(lambda g: jnp.where(strict, g[:L, :L], 0.0), G)
    Nak = each(lambda g: jnp.where(strict, g[:L, L:], 0.0), G)
    Mrb = each(lambda g: jnp.where(incl, g[L:, :L], 0.0), G)
    Mrk = each(lambda g: jnp.where(incl, g[L:, L:], 0.0), G)
    T = each(lambda n: eye + jnp.where(row // 2 == col // 2, n, 0.0), Nab)
    s = 2
    while s < L:
        off = (row // (2 * s) == col // (2 * s)) & (row // s != col // s)
        T = each(lambda t, n: t + _mm(t, _mm(jnp.where(off, n, 0.0), t, "nn", P_T), "nn", P_T), T, Nab)
        s *= 2
    NV = each(lambda n, v: _mm(n, v, "nn", P_X), Nak, V)
    TX = each(lambda t, x, nv: _mm(t, jnp.concatenate([x, nv], axis=1), "nn", P_X), T, At, NV)
    MX = each(lambda m, tx: _mm(m, tx, "nn", P_X), Mrb, TX)
    MV = each(lambda m, v: _mm(m, v, "nn", P_X), Mrk, V)
    Y = each(lambda rt, mx, mv, s0: _mm(rt + mx[:, :N], s0, "nt", P_S) + (mx[:, N:] + mv), Rt, MX, MV, S0)
    U = each(lambda tx, s0: _mm(tx[:, :N], s0, "nt", P_S) + tx[:, N:], TX, S0)
    upd = each(lambda u, v, bt, kt: _mm(u, bt, "tn", P_S) + _mm(v, kt, "tn", P_S), U, V, Bt, Kt)
    for h, s in enumerate(sls):
        y_ref[:, s] = Y[h]
        S_ref[0, h] = (S0[h] + upd[h]) * g_in[h][L - 1:L, :]


def rwkv_scan(B, T, r, lw, k, v, a, b, S0):
    L = CHUNK if T % CHUNK == 0 else T
    assert L & (L - 1) == 0
    nc = T // L
    xs = pl.BlockSpec((L, RW_WIDTH), lambda bi, c: (bi * nc + c, 0))
    ss = pl.BlockSpec((1, RW_HEADS, RW_HEAD, RW_HEAD), lambda bi, c: (bi, 0, 0, 0))
    return pl.pallas_call(
        functools.partial(_rwkv_chunk_kernel, L=L, P_G=3, P_T=3, P_X=3, P_S=3),
        grid=(B, nc),
        in_specs=[xs] * 6 + [ss],
        out_specs=[xs, ss],
        out_shape=[jax.ShapeDtypeStruct((B * T, RW_WIDTH), F32),
                   jax.ShapeDtypeStruct((B, RW_HEADS, RW_HEAD, RW_HEAD), F32)],
        compiler_params=_cparams(("parallel", "arbitrary")),
        name="rwkv_scan",
    )(r, lw, k, v, a, b, S0)


def _rwkv_post_kernel(y_ref, g_ref, bonus_ref, lg_ref, lb_ref, o_ref):
    bd = _pair_sum_matrix()
    y = y_ref[...]
    mean = _head_sum(y, bd) * (1.0 / RW_HEAD)
    d = y - mean
    var = _head_sum(d * d, bd) * (1.0 / RW_HEAD)
    yn = d * lax.rsqrt(var + RW_LN_EPS) * lg_ref[...] + lb_ref[...]
    o_ref[...] = ((yn + bonus_ref[...]) * g_ref[...]).astype(o_ref.dtype)


def rwkv_post(rows, y, g, bonus, lnx_g, lnx_b):
    M = rows.M
    tm = min(rows.tm, 256)
    W = RW_WIDTH
    xs = pl.BlockSpec((tm, W), lambda i: (i, 0))
    ps = pl.BlockSpec((1, W), lambda i: (0, 0))
    return pl.pallas_call(
        _rwkv_post_kernel,
        grid=(M // tm,),
        in_specs=[xs, xs, xs, ps, ps],
        out_specs=xs,
        out_shape=jax.ShapeDtypeStruct((M, W), BF16),
        compiler_params=_cparams(("parallel",)),
        name="rwkv_post",
    )(y, g, bonus, lnx_g.reshape(1, W), lnx_b.reshape(1, W))


def _cast_blocks_kernel(x_ref, o_ref):
    o_ref[...] = x_ref[...].astype(o_ref.dtype)


def cast_blocks(w):
    lead = w.shape[:-2]
    K, N = w.shape[-2:]
    G = int(np.prod(lead))
    tk = _row_tile(K, 1024)

    def src(g, j, k):
        idx = []
        for d in reversed(lead):
            idx.append(g % d)
            g = g // d
        return tuple(reversed(idx)) + (k, j)

    return pl.pallas_call(
        _cast_blocks_kernel,
        grid=(G, N // TN, K // tk),
        in_specs=[pl.BlockSpec((None,) * len(lead) + (tk, TN), src)],
        out_specs=pl.BlockSpec((None, None, tk, TN), lambda g, j, k: (g, j, k, 0)),
        out_shape=jax.ShapeDtypeStruct((G, N // TN, K, TN), BF16),
        compiler_params=_cparams(("parallel", "parallel", "parallel")),
        name="cast_blocks",
    )(w)


def _w_in_segments():
    o_ml = 0
    o_if = 3072
    o_q = o_if + 2 * ML_HEADS
    o_lat = o_q + MLA_HEADS * (MLA_NOPE + MLA_ROPE)
    o_kr = o_lat + MLA_LATENT
    o_rw = o_kr + MLA_ROPE
    o_lo = o_rw + 3 * RW_WIDTH
    o_gate = o_lo + RW_LORA
    segs = [(OFF_ML_Q, o_ml, 3072), (OFF_RW_R, o_rw, 3 * RW_WIDTH), (OFF_GATE, o_gate, 3 * D_MODEL),
            (OFF_LAT, o_lat, MLA_LATENT), (OFF_RW_LORA, o_lo, RW_LORA),
            (OFF_KR, o_kr, MLA_ROPE), (OFF_KR + MLA_ROPE, None, LANES - MLA_ROPE),
            (OFF_IF, o_if, 2 * ML_HEADS), (OFF_IF + 2 * ML_HEADS, None, LANES - 2 * ML_HEADS)]
    for h in range(MLA_HEADS):
        src = o_q + h * (MLA_NOPE + MLA_ROPE)
        segs += [(OFF_QN + h * MLA_NOPE, src, MLA_NOPE), (OFF_QR + h * LANES, src + MLA_NOPE, MLA_ROPE),
                 (OFF_QR + h * LANES + MLA_ROPE, None, LANES - MLA_ROPE)]
    assert sum(s[2] for s in segs) == NZ and o_gate + 3 * D_MODEL == IN_COLS
    return segs


def _prep_w_in_kernel(x_ref, o_ref):
    rows = x_ref.shape[0]
    for dst, src, width in _w_in_segments():
        while width:
            j, off = divmod(dst, TN_IN)
            n = min(width, TN_IN - off)
            if src is None:
                o_ref[j, :, off:off + n] = jnp.zeros((rows, n), o_ref.dtype)
            else:
                o_ref[j, :, off:off + n] = x_ref[:, src:src + n].astype(o_ref.dtype)
                src += n
            dst, width = dst + n, width - n


def prep_w_in(w):
    G, K, N = w.shape
    tk = 128
    return pl.pallas_call(
        _prep_w_in_kernel,
        grid=(G, K // tk),
        in_specs=[pl.BlockSpec((None, tk, N), lambda g, k: (g, k, 0))],
        out_specs=pl.BlockSpec((None, NZ // TN_IN, tk, TN_IN), lambda g, k: (g, 0, k, 0)),
        out_shape=jax.ShapeDtypeStruct((G, NZ // TN_IN, K, TN_IN), BF16),
        compiler_params=_cparams(("parallel", "parallel")),
        name="prep_w_in",
    )(w)


def _row_tile(M, cap):
    return max(t for t in range(8, cap + 1, 8) if M % t == 0)


def _rope_tables(T, past, reps):
    half = MLA_ROPE // 2
    inv = ROPE_BASE ** (-jnp.arange(half, dtype=F32) / half)
    ang = (past + jnp.arange(T)).astype(F32)[:, None] * inv[None, :]
    cos, sin = jnp.cos(ang), jnp.sin(ang)
    zeros = jnp.zeros((T, LANES - MLA_ROPE), F32)
    cos_t = jnp.concatenate([cos, cos, zeros], axis=1)
    sin_t = jnp.concatenate([-sin, sin, zeros], axis=1)
    return jnp.tile(cos_t, (reps, 1)), jnp.tile(sin_t, (reps, 1))


def _layer(rows, x, mod, l, W, P, lat_past, kr_past, C0, n0, m0, S0, shift0, tabs):
    B, T, M = rows.B, rows.T, rows.M
    D = D_MODEL
    m9 = [rows.expand(mod[:, i * D:(i + 1) * D]) for i in range(N_ADA)]
    sh1, sc1, g1, sh2, sc2, g2, sh3, sc3, g3 = m9
    act = mm_swiglu(rows, x, P['norm_g'][l, 0], sh1, sc1, (W['ffn_in'], 2 * l))
    x = mm_resid(rows, act, (W['ffn_out'], 2 * l), x, g1, 0.5)
    z_all = mm_norm(rows, x, P['norm_g'][l, 1], sh2, sc2, (W['w_in'], l), name="mm_in")
    bias_row = jnp.pad(jnp.concatenate([P['mlstm_i_bias'][l], P['mlstm_f_bias'][l]]), (0, LANES - 2 * ML_HEADS))[None]
    om, C, n, m = mlstm(B, T, z_all, bias_row, P['mlstm_head_g'][l], C0, n0, m0)
    past = lat_past.shape[1]
    qr, lat, kr = mla_pre(rows, z_all, tabs[0], tabs[1], P['mla_kv_norm_g'][l])
    Sk = past + T
    if past:
        lat_all = jnp.concatenate([lat_past, lat.reshape(B, T, -1)], axis=1).reshape(B * Sk, -1)
        kr_all = jnp.concatenate([jnp.pad(kr_past, ((0, 0), (0, 0), (0, LANES - MLA_ROPE))), kr.reshape(B, T, -1)],
                                 axis=1).reshape(B * Sk, LANES)
    else:
        lat_all, kr_all = lat, kr
    if T <= LATENT_ATTN_MAX_T:
        oa = mla_attention_latent(B, T, Sk, past, z_all, qr, lat_all.astype(BF16), kr_all.astype(BF16), (W['kv_b'], l))
    else:
        kv = matmul(lat_all, (W['kv_b'], l), _row_tile(B * Sk, 1536), out_dtype=BF16, name="mm_kv")
        oa = mla_attention(B, T, Sk, past, z_all, qr, kv, kr_all.astype(BF16))
    r_, w_, k_, v_, a_, b_, g_, bonus = rwkv_pre(rows, z_all, shift0, P['rwkv_mu'][l], P['rwkv_w0'][l], P['rwkv_w2'][l],
                                                 P['rwkv_a0'][l], P['rwkv_a2'][l], P['rwkv_g2'][l], P['rwkv_k_k'][l],
                                                 P['rwkv_k_a'][l], P['rwkv_r_k'][l].reshape(-1))
    y, S = rwkv_scan(B, T, r_, w_, k_, v_, a_, b_, S0)
    orw = rwkv_post(rows, y, g_, bonus, P['rwkv_lnx_g'][l], P['rwkv_lnx_b'][l])
    z_last = z_all.reshape(B, T, NZ)[:, -1]
    shift = jnp.concatenate([z_last[:, OFF_RW_R:OFF_RW_R + 3 * RW_WIDTH], z_last[:, OFF_RW_LORA:OFF_RW_LORA + RW_LORA]],
                            axis=1)
    mixed = mm_branch(rows, om, oa, orw, (W['branch'], 3 * l), z_all, P['b_merge'][l][None])
    x = mm_resid(rows, mixed, (W['w_out'], l), x, g2, 1.0)
    act = mm_swiglu(rows, x, P['norm_g'][l, 2], sh3, sc3, (W['ffn_in'], 2 * l + 1))
    x = mm_resid(rows, act, (W['ffn_out'], 2 * l + 1), x, g3, 0.5)
    new = (lat.reshape(B, T, MLA_LATENT), kr.reshape(B, T, LANES)[:, :, :MLA_ROPE], C, n, m, S, shift)
    return x, new


def _trunk(rows, x, mods, W, P, lat_cache, kr_cache, C_st, n_st, m_st, S_st, sh_st):
    depth = len(mods)
    past = lat_cache.shape[2]
    tabs = _rope_tables(rows.T, past, rows.B)
    x = x.reshape(rows.M, D_MODEL)
    new = []
    for l in range(depth):
        x, st = _layer(rows, x, mods[l], l, W, P, lat_cache[l], kr_cache[l], C_st[l], n_st[l], m_st[l], S_st[l],
                       sh_st[l], tabs)
        new.append(st)
    y = final_norm(rows, x, P['final_norm_g']).reshape(rows.B, rows.T, D_MODEL)
    return y, tuple(jnp.stack(s, axis=0) for s in zip(*new))


def kernel(x_prompt, x_sample, c_prompt, c_sample, cache_mla_latent, cache_mla_krope, state_mlstm_C, state_mlstm_n,
           state_mlstm_m, state_rwkv_S, state_rwkv_shift, ada_w, ada_b, norm_g, ffn_w_in, ffn_w_out, w_in,
           mlstm_i_bias, mlstm_f_bias, mlstm_head_g, mla_kv_norm_g, mla_w_kv_b, rwkv_mu, rwkv_w0, rwkv_w2, rwkv_a0,
           rwkv_a2, rwkv_g2, rwkv_k_k, rwkv_k_a, rwkv_r_k, rwkv_lnx_g, rwkv_lnx_b, w_branch, b_merge, w_out,
           final_norm_g):
    P = dict(norm_g=norm_g, mlstm_i_bias=mlstm_i_bias, mlstm_f_bias=mlstm_f_bias, mlstm_head_g=mlstm_head_g,
             mla_kv_norm_g=mla_kv_norm_g, rwkv_mu=rwkv_mu, rwkv_w0=rwkv_w0, rwkv_w2=rwkv_w2, rwkv_a0=rwkv_a0,
             rwkv_a2=rwkv_a2, rwkv_g2=rwkv_g2, rwkv_k_k=rwkv_k_k, rwkv_k_a=rwkv_k_a, rwkv_r_k=rwkv_r_k,
             rwkv_lnx_g=rwkv_lnx_g, rwkv_lnx_b=rwkv_lnx_b, b_merge=b_merge, final_norm_g=final_norm_g)
    depth = w_in.shape[0]
    W = dict(ffn_in=cast_blocks(ffn_w_in), ffn_out=cast_blocks(ffn_w_out), w_in=prep_w_in(w_in),
             kv_b=cast_blocks(mla_w_kv_b), branch=cast_blocks(w_branch), w_out=cast_blocks(w_out))
    Bp, Tp, _ = x_prompt.shape
    Bs, Ts, _ = x_sample.shape
    n_c = Bp + Bs
    c_all = jnp.pad(jnp.concatenate([c_prompt, c_sample], axis=0), ((0, (-n_c) % 8), (0, 0)))
    mods = [ada_mod(c_all, ada_w, ada_b, l) for l in range(depth)]
    rows_p = Rows(Bp, Tp, TM_MAX)
    rows_s = Rows(Bs, Ts, TM_MAX)
    zeros = lambda *s: jnp.zeros((depth, Bp) + s, F32)
    y_p, st_p = _trunk(rows_p, x_prompt, [m[:Bp] for m in mods], W, P,
                       zeros(0, MLA_LATENT), zeros(0, MLA_ROPE), zeros(ML_HEADS, ML_DV, ML_DQK), zeros(ML_HEADS, ML_DQK),
                       zeros(ML_HEADS), zeros(RW_HEADS, RW_HEAD, RW_HEAD), zeros(RW_COLS))
    y_s, st_s = _trunk(rows_s, x_sample, [m[Bp:n_c] for m in mods], W, P,
                       cache_mla_latent, cache_mla_krope, state_mlstm_C, state_mlstm_n, state_mlstm_m, state_rwkv_S,
                       state_rwkv_shift)
    return (y_p, y_s) + st_p + st_s
```

```python
import functools

import jax
import jax.numpy as jnp
import numpy as np
from jax import lax
from jax.experimental import pallas as pl
from jax.experimental.pallas import tpu as pltpu

F32 = jnp.float32
BF16 = jnp.bfloat16
HIGHEST = lax.Precision.HIGHEST

D_MODEL = 2048
CHUNK = 64
NORM_EPS = 1e-6
D_FF = 5632
N_ADA = 9
ML_HEADS, ML_DQK, ML_DV, ML_GATE_CAP = 4, 128, 256, 15.0
ML_CHUNK = 256
MLA_HEADS, MLA_NOPE, MLA_ROPE, MLA_V, MLA_LATENT = 8, 128, 64, 128, 512
MLA_SCALE = (MLA_NOPE + MLA_ROPE) ** -0.5
LATENT_ATTN_MAX_T = 64
ROPE_BASE = 10000.0
RW_HEADS, RW_HEAD = 16, 64
RW_WIDTH = RW_HEADS * RW_HEAD
RW_LN_EPS = 64e-5
RW_LORA = 256
RW_COLS = 3 * RW_WIDTH + RW_LORA
BRANCH_WIDTH = 1024

LANES = 128
VMEM_LIMIT = 56 * 1024 * 1024
TM_MAX = 1024
TN = 512
TN_IN = 1024

OFF_ML_Q, OFF_ML_K, OFF_ML_V, OFF_ML_O = 0, 512, 1024, 2048
OFF_QN, OFF_QR = 3072, 4096
OFF_RW_R, OFF_RW_K, OFF_RW_V = 5120, 6144, 7168
OFF_GATE = 8192
OFF_LAT = 14336
OFF_RW_LORA = 14848
OFF_KR = 15104
OFF_IF = 15232
NZ = 15360
IN_COLS = 14664


def _cparams(sem):
    return pltpu.CompilerParams(dimension_semantics=sem, vmem_limit_bytes=VMEM_LIMIT)


def _sigmoid(x):
    return 1.0 / (1.0 + jnp.exp(-x))


def _silu(x):
    return x * _sigmoid(x)


def _dot(a, b):
    return jnp.dot(a, b, preferred_element_type=F32)


def _dot_nt(a, b, precision=None):
    return lax.dot_general(a, b, (((1,), (1,)), ((), ())), precision=precision, preferred_element_type=F32)


def _dot_tn(a, b, precision=None):
    return lax.dot_general(a, b, (((0,), (0,)), ((), ())), precision=precision, preferred_element_type=F32)


class Rows:
    def __init__(self, B, T, tm_max):
        self.B, self.T, self.M = B, T, B * T
        if T >= 256:
            self.tm = min(tm_max, T)
            assert T % self.tm == 0
            self.tpg = T // self.tm
            self.R = 1
        else:
            self.tm = self.M
            self.tpg = 1
            self.R = self.M
        self.nm = self.M // self.tm

    def expand(self, v):
        if self.R == 1:
            return v[:, None, :]
        return jnp.repeat(v, self.T, axis=0)[None]

    def vec_spec(self, tn, col=lambda j: j):
        return pl.BlockSpec((None, self.R, tn), lambda i, j: (i // self.tpg, 0, col(j)))


def _ada_kernel(c_ref, w_ref, b_ref, o_ref):
    c = c_ref[...]
    o_ref[...] = _dot(_silu(c).astype(BF16), w_ref[...].astype(BF16)) + b_ref[...]


def ada_mod(c, w, b, l):
    R, D = c.shape
    N = w.shape[2]
    tn = 1024
    return pl.pallas_call(
        _ada_kernel,
        grid=(N // tn,),
        in_specs=[pl.BlockSpec((R, D), lambda j: (0, 0)),
                  pl.BlockSpec((None, D, tn), lambda j: (l, 0, j)),
                  pl.BlockSpec((1, tn), lambda j: (0, j))],
        out_specs=pl.BlockSpec((R, tn), lambda j: (0, j)),
        out_shape=jax.ShapeDtypeStruct((R, N), F32),
        compiler_params=_cparams(("arbitrary",)),
        name="ada_mod",
    )(c, w, b[l].reshape(1, N))


def _final_norm_kernel(x_ref, g_ref, o_ref):
    x = x_ref[...]
    o_ref[...] = x * lax.rsqrt(jnp.mean(x * x, axis=-1, keepdims=True) + NORM_EPS) * g_ref[...]


def final_norm(rows, x, g):
    D = x.shape[1]
    tm = min(rows.tm, 512)
    return pl.pallas_call(
        _final_norm_kernel,
        grid=(rows.M // tm,),
        in_specs=[pl.BlockSpec((tm, D), lambda i: (i, 0)), pl.BlockSpec((1, D), lambda i: (0, 0))],
        out_specs=pl.BlockSpec((tm, D), lambda i: (i, 0)),
        out_shape=jax.ShapeDtypeStruct((rows.M, D), F32),
        compiler_params=_cparams(("parallel",)),
        name="final_norm",
    )(x, g.reshape(1, D))


def _mm_kernel(a_ref, w_ref, o_ref):
    o_ref[...] = _dot(a_ref[...].astype(BF16), w_ref[...]).astype(o_ref.dtype)


def matmul(a, w, tm, out_dtype=F32, name="matmul"):
    M, K = a.shape
    w, g = w
    _, nj, _, tn = w.shape
    N = nj * tn
    return pl.pallas_call(
        _mm_kernel,
        grid=(M // tm, nj),
        in_specs=[pl.BlockSpec((tm, K), lambda i, j: (i, 0)),
                  pl.BlockSpec((None, None, K, tn), lambda i, j: (g, j, 0, 0))],
        out_specs=pl.BlockSpec((tm, tn), lambda i, j: (i, j)),
        out_shape=jax.ShapeDtypeStruct((M, N), out_dtype),
        compiler_params=_cparams(("parallel", "arbitrary")),
        name=name,
    )(a, w)


def _normed_rows(x_ref, g_ref, sh_ref, sc_ref, h_sc):
    @pl.when(pl.program_id(1) == 0)
    def _():
        x = x_ref[...]
        y = x * lax.rsqrt(jnp.mean(x * x, axis=-1, keepdims=True) + NORM_EPS) * g_ref[...]
        h_sc[...] = (y * (1.0 + sc_ref[...]) + sh_ref[...]).astype(h_sc.dtype)


def _norm_specs(rows, D):
    vec = (pl.BlockSpec((None, 1, D), lambda i, j: (i // rows.tpg, 0, 0)) if rows.R == 1
           else pl.BlockSpec((None, rows.tm, D), lambda i, j: (0, i, 0)))
    return [pl.BlockSpec((rows.tm, D), lambda i, j: (i, 0)), pl.BlockSpec((1, D), lambda i, j: (0, 0)), vec, vec]


def _mm_norm_kernel(x_ref, g_ref, sh_ref, sc_ref, w_ref, o_ref, h_sc):
    _normed_rows(x_ref, g_ref, sh_ref, sc_ref, h_sc)
    o_ref[...] = _dot(h_sc[...], w_ref[...]).astype(o_ref.dtype)


def mm_norm(rows, x, g, shift, scale, w, name):
    D = x.shape[1]
    w, gi = w
    _, nj, _, tn = w.shape
    return pl.pallas_call(
        _mm_norm_kernel,
        grid=(rows.nm, nj),
        in_specs=_norm_specs(rows, D) + [pl.BlockSpec((None, None, D, tn), lambda i, j: (gi, j, 0, 0))],
        out_specs=pl.BlockSpec((rows.tm, tn), lambda i, j: (i, j)),
        out_shape=jax.ShapeDtypeStruct((rows.M, nj * tn), F32),
        scratch_shapes=[pltpu.VMEM((rows.tm, D), BF16)],
        compiler_params=_cparams(("parallel", "arbitrary")),
        name=name,
    )(x, g.reshape(1, D), shift, scale, w)


def _swiglu_kernel(x_ref, g_ref, sh_ref, sc_ref, wg_ref, wu_ref, o_ref, h_sc):
    _normed_rows(x_ref, g_ref, sh_ref, sc_ref, h_sc)
    h = h_sc[...]
    g = _dot(h, wg_ref[...])
    u = _dot(h, wu_ref[...])
    o_ref[...] = (_silu(g) * u).astype(o_ref.dtype)


def mm_swiglu(rows, x, g, shift, scale, w_in):
    D = x.shape[1]
    w_in, gi = w_in
    tn = w_in.shape[3]
    nj = D_FF // tn
    return pl.pallas_call(
        _swiglu_kernel,
        grid=(rows.nm, nj),
        in_specs=_norm_specs(rows, D) + [pl.BlockSpec((None, None, D, tn), lambda i, j: (gi, j, 0, 0)),
                                         pl.BlockSpec((None, None, D, tn), lambda i, j: (gi, j + nj, 0, 0))],
        out_specs=pl.BlockSpec((rows.tm, tn), lambda i, j: (i, j)),
        out_shape=jax.ShapeDtypeStruct((rows.M, D_FF), BF16),
        scratch_shapes=[pltpu.VMEM((rows.tm, D), BF16)],
        compiler_params=_cparams(("parallel", "arbitrary")),
        name="mm_swiglu",
    )(x, g.reshape(1, D), shift, scale, w_in, w_in)


def _resid_kernel(a_ref, w_ref, x_ref, g_ref, o_ref, *, coef):
    o_ref[...] = x_ref[...] + (coef * g_ref[...]) * _dot(a_ref[...], w_ref[...])


def mm_resid(rows, a, w, x, gate, coef):
    K = a.shape[1]
    w, g = w
    _, nj, _, tn = w.shape
    N = nj * tn
    return pl.pallas_call(
        functools.partial(_resid_kernel, coef=coef),
        grid=(rows.nm, nj),
        in_specs=[pl.BlockSpec((rows.tm, K), lambda i, j: (i, 0)),
                  pl.BlockSpec((None, None, K, tn), lambda i, j: (g, j, 0, 0)),
                  pl.BlockSpec((rows.tm, tn), lambda i, j: (i, j)),
                  rows.vec_spec(tn)],
        out_specs=pl.BlockSpec((rows.tm, tn), lambda i, j: (i, j)),
        out_shape=jax.ShapeDtypeStruct((rows.M, N), F32),
        compiler_params=_cparams(("parallel", "arbitrary")),
        name="mm_resid",
    )(a, w, x, gate)


def _branch_kernel(om_ref, oa_ref, or_ref, w0_ref, w1_ref, w2_ref, z0_ref, z1_ref, z2_ref, b0_ref, b1_ref, b2_ref,
                   o_ref):
    acc = _sigmoid(z0_ref[...] + b0_ref[...]) * _dot(om_ref[...], w0_ref[...])
    acc += _sigmoid(z1_ref[...] + b1_ref[...]) * _dot(oa_ref[...], w1_ref[...])
    acc += _sigmoid(z2_ref[...] + b2_ref[...]) * _dot(or_ref[...], w2_ref[...])
    o_ref[...] = acc.astype(o_ref.dtype)


def mm_branch(rows, om, oa, orw, w_branch, z_all, b_merge):
    w_branch, g = w_branch
    tn = w_branch.shape[3]
    D = D_MODEL
    nj = D // tn
    a_spec = pl.BlockSpec((rows.tm, BRANCH_WIDTH), lambda i, j: (i, 0))
    w_specs = [pl.BlockSpec((None, None, BRANCH_WIDTH, tn), lambda i, j, n=n: (g + n, j, 0, 0)) for n in range(3)]
    z_specs = [pl.BlockSpec((rows.tm, tn), lambda i, j, n=n: (i, (OFF_GATE + n * D) // tn + j)) for n in range(3)]
    b_specs = [pl.BlockSpec((1, tn), lambda i, j, n=n: (0, n * D // tn + j)) for n in range(3)]
    return pl.pallas_call(
        _branch_kernel,
        grid=(rows.nm, nj),
        in_specs=[a_spec, a_spec, a_spec] + w_specs + z_specs + b_specs,
        out_specs=pl.BlockSpec((rows.tm, tn), lambda i, j: (i, j)),
        out_shape=jax.ShapeDtypeStruct((rows.M, D), BF16),
        compiler_params=_cparams(("parallel", "arbitrary")),
        name="mm_branch",
    )(om, oa, orw, w_branch, w_branch, w_branch, z_all, z_all, z_all, b_merge, b_merge, b_merge)


def _mlstm_kernel(q_ref, k_ref, v_ref, o_ref, if_ref, bias_ref, hg_ref, C0_ref, n0_ref, m0_ref,
                  om_ref, C_ref, n_ref, m_ref, *, L):
    c = pl.program_id(1)

    @pl.when(c == 0)
    def _():
        C_ref[...] = C0_ref[...]
        n_ref[...] = n0_ref[...]
        m_ref[...] = m0_ref[...]

    H = ML_HEADS
    sc = ML_GATE_CAP * jnp.tanh((if_ref[...] + bias_ref[...]) * (1.0 / ML_GATE_CAP))
    lf = jnp.minimum(sc, 0.0) - jnp.log(1.0 + jnp.exp(-jnp.abs(sc)))
    row = lax.broadcasted_iota(jnp.int32, (L, L), 0)
    col = lax.broadcasted_iota(jnp.int32, (L, L), 1)
    causal = row >= col
    bcum = jnp.dot(causal.astype(F32), lf, precision=HIGHEST, preferred_element_type=F32)
    sel = (lax.broadcasted_iota(jnp.int32, (8, LANES), 0) == lax.broadcasted_iota(jnp.int32, (8, LANES), 1)).astype(F32)
    scT = _dot_nt(sel, sc, HIGHEST)
    bT = _dot_nt(sel, bcum, HIGHEST)
    hs = range(H)
    each = lambda f, *lists: [f(*xs) for xs in zip(*lists)]
    qsl = [slice(h * ML_DQK, (h + 1) * ML_DQK) for h in hs]
    vsl = [slice(h * ML_DV, (h + 1) * ML_DV) for h in hs]
    b_col = [bcum[:, H + h:H + h + 1] for h in hs]
    b_row = [bT[H + h:H + h + 1, :] for h in hs]
    ig_row = [scT[h:h + 1, :] for h in hs]
    ig_col = [sc[:, h:h + 1] for h in hs]
    m_prev = [m_ref[0, h, :, 0:1] for h in hs]
    C = [C_ref[0, h] for h in hs]
    n = [n_ref[0, h] for h in hs]
    qf = [q_ref[:, s] for s in qsl]
    q = each(lambda x: x.astype(BF16), qf)
    kb = [(k_ref[:, s] * (ML_DQK ** -0.5)).astype(BF16) for s in qsl]
    v = [v_ref[:, s] for s in vsl]
    logD = each(lambda bc, br, ir: jnp.where(causal, bc - br + ir, -jnp.inf), b_col, b_row, ig_row)
    g = each(lambda bc, mp: bc + mp, b_col, m_prev)
    m_t = each(lambda gg, ld: jnp.maximum(gg, jnp.max(ld, axis=1, keepdims=True)), g, logD)
    Dm = each(lambda ld, mt: jnp.exp(ld - mt), logD, m_t)
    inter = each(lambda gg, mt: jnp.exp(gg - mt), g, m_t)
    qk = each(lambda qq, kk, d: _dot_nt(qq, kk) * d, q, kb, Dm)
    qC = each(lambda qq, cc: _dot_nt(qq, cc.astype(BF16)), q, C)
    num = each(lambda s, vv, it, x: _dot(s.astype(BF16), vv.astype(BF16)) + it * x, qk, v, inter, qC)
    qn = each(lambda x, nn: jnp.sum(x * nn, axis=1, keepdims=True), qf, n)
    den = each(lambda s, it, x: jnp.sum(s, axis=1, keepdims=True) + it * x, qk, inter, qn)
    hh = each(lambda nu, de, mt: nu / jnp.maximum(jnp.abs(de), jnp.exp(-mt)), num, den, m_t)
    m_new = each(lambda mt: mt[L - 1:L, :], m_t)
    b_last = each(lambda bc: bc[L - 1:L, :], b_col)
    carry_w = each(lambda bl, mp, mn: jnp.exp(bl + mp - mn), b_last, m_prev, m_new)
    w_s = each(lambda bl, bc, ic, mn: jnp.exp(bl - bc + ic - mn), b_last, b_col, ig_col, m_new)
    dC = each(lambda w, vv, kk: _dot_tn((w * vv).astype(BF16), kk), w_s, v, kb)
    dn = each(lambda w, kk: jnp.sum(w * kk.astype(F32), axis=0, keepdims=True), w_s, kb)
    for h in hs:
        y = hh[h] * lax.rsqrt(jnp.mean(hh[h] * hh[h], axis=-1, keepdims=True) + NORM_EPS) * hg_ref[:, vsl[h]]
        om_ref[:, vsl[h]] = (_sigmoid(o_ref[:, vsl[h]]) * y).astype(om_ref.dtype)
        C_ref[0, h] = carry_w[h] * C[h] + dC[h]
        n_ref[0, h] = carry_w[h] * n[h] + dn[h]
        m_ref[0, h] = jnp.broadcast_to(m_new[h], (1, LANES))


def mlstm(B, T, z_all, bias_row, head_g, C0, n0, m0):
    L = next((c for c in (ML_CHUNK, CHUNK) if T % c == 0), T)
    nc = T // L
    H = ML_HEADS
    zrow = lambda w, off: pl.BlockSpec((L, w), lambda b, c: (b * nc + c, off // w))
    st = lambda *shape: pl.BlockSpec((1,) + shape, lambda b, c: (b,) + (0,) * len(shape))
    om, C, n, m = pl.pallas_call(
        functools.partial(_mlstm_kernel, L=L),
        grid=(B, nc),
        in_specs=[zrow(H * ML_DQK, OFF_ML_Q), zrow(H * ML_DQK, OFF_ML_K), zrow(H * ML_DV, OFF_ML_V),
                  zrow(H * ML_DV, OFF_ML_O), zrow(LANES, OFF_IF),
                  pl.BlockSpec((1, LANES), lambda b, c: (0, 0)),
                  pl.BlockSpec((1, H * ML_DV), lambda b, c: (0, 0)),
                  st(H, ML_DV, ML_DQK), st(H, 1, ML_DQK), st(H, 1, LANES)],
        out_specs=[pl.BlockSpec((L, H * ML_DV), lambda b, c: (b * nc + c, 0)),
                   st(H, ML_DV, ML_DQK), st(H, 1, ML_DQK), st(H, 1, LANES)],
        out_shape=[jax.ShapeDtypeStruct((B * T, H * ML_DV), BF16),
                   jax.ShapeDtypeStruct((B, H, ML_DV, ML_DQK), F32),
                   jax.ShapeDtypeStruct((B, H, 1, ML_DQK), F32),
                   jax.ShapeDtypeStruct((B, H, 1, LANES), F32)],
        compiler_params=_cparams(("parallel", "arbitrary")),
        name="mlstm",
    )(z_all, z_all, z_all, z_all, z_all, bias_row, head_g.reshape(1, -1), C0, n0.reshape(B, H, 1, ML_DQK),
      jnp.broadcast_to(m0[:, :, None, None], (B, H, 1, LANES)))
    return om, C, n.reshape(B, H, ML_DQK), m[:, :, 0, 0]


def _rope_lanes(x, cos, sin):
    swapped = pltpu.roll(x, 32, 1) + pltpu.roll(x, 96, 1)
    return x * cos + swapped * sin


def _mla_pre_kernel(qr_ref, lat_ref, kr_ref, cos_ref, sin_ref, g_ref, qro_ref, lato_ref, kro_ref):
    cos = cos_ref[...]
    sin = sin_ref[...]
    for h in range(MLA_HEADS):
        sl = slice(h * LANES, (h + 1) * LANES)
        qro_ref[:, sl] = (_rope_lanes(qr_ref[:, sl], cos, sin) * MLA_SCALE).astype(qro_ref.dtype)
    kro_ref[...] = _rope_lanes(kr_ref[...], cos, sin)
    lat = lat_ref[...]
    lato_ref[...] = lat * lax.rsqrt(jnp.mean(lat * lat, axis=-1, keepdims=True) + NORM_EPS) * g_ref[...]


def mla_pre(rows, z_all, cos_tab, sin_tab, kv_norm_g):
    tm = min(rows.tm, 256)
    M = rows.M
    zspec = lambda w, off: pl.BlockSpec((tm, w), lambda i: (i, off // w))
    W = MLA_HEADS * LANES
    return pl.pallas_call(
        _mla_pre_kernel,
        grid=(M // tm,),
        in_specs=[zspec(W, OFF_QR), zspec(MLA_LATENT, OFF_LAT), zspec(LANES, OFF_KR),
                  pl.BlockSpec((tm, LANES), lambda i: (i, 0)), pl.BlockSpec((tm, LANES), lambda i: (i, 0)),
                  pl.BlockSpec((1, MLA_LATENT), lambda i: (0, 0))],
        out_specs=[pl.BlockSpec((tm, W), lambda i: (i, 0)), pl.BlockSpec((tm, MLA_LATENT), lambda i: (i, 0)),
                   pl.BlockSpec((tm, LANES), lambda i: (i, 0))],
        out_shape=[jax.ShapeDtypeStruct((M, W), BF16), jax.ShapeDtypeStruct((M, MLA_LATENT), F32),
                   jax.ShapeDtypeStruct((M, LANES), F32)],
        compiler_params=_cparams(("parallel",)),
        name="mla_pre",
    )(z_all, z_all, z_all, cos_tab, sin_tab, kv_norm_g.reshape(1, -1))


def _attn_kernel(qn_ref, qr_ref, kn_ref, kr_ref, v_ref, o_ref, m_sc, l_sc, acc_sc, *, tq, tk, past, nk):
    qi = pl.program_id(2)
    q = jnp.concatenate([(qn_ref[...] * MLA_SCALE).astype(BF16), qr_ref[...]], axis=1)
    m_sc[...] = jnp.full_like(m_sc, -jnp.inf)
    l_sc[...] = jnp.zeros_like(l_sc)
    acc_sc[...] = jnp.zeros_like(acc_sc)
    q0 = past + qi * tq
    n_full = jnp.minimum(((q0 // CHUNK + 1) * CHUNK) // tk, nk)
    n_vis = jnp.minimum((((q0 + tq - 1) // CHUNK + 1) * CHUNK + tk - 1) // tk, nk)

    def tile(kj, masked):
        k0 = pl.multiple_of(kj * tk, tk)
        k = jnp.concatenate([kn_ref[pl.ds(k0, tk), :], kr_ref[pl.ds(k0, tk), :]], axis=1)
        s = _dot_nt(q, k)
        if masked:
            q_chunk = (q0 + lax.broadcasted_iota(jnp.int32, (tq, 1), 0)) // CHUNK
            k_chunk = (k0 + lax.broadcasted_iota(jnp.int32, (1, tk), 1)) // CHUNK
            s = jnp.where(k_chunk <= q_chunk, s, -jnp.inf)
        m_prev = m_sc[...]
        m_new = jnp.maximum(m_prev, jnp.max(s, axis=1, keepdims=True))
        alpha = jnp.exp(m_prev - m_new)
        p = jnp.exp(s - m_new)
        l_sc[...] = alpha * l_sc[...] + jnp.sum(p, axis=1, keepdims=True)
        acc_sc[...] = alpha * acc_sc[...] + _dot(p.astype(BF16), v_ref[pl.ds(k0, tk), :])
        m_sc[...] = m_new

    def full_tile(kj, c):
        tile(kj, False)
        return c

    def edge_tile(kj, c):
        tile(kj, True)
        return c

    lax.fori_loop(0, n_full, full_tile, 0)
    lax.fori_loop(n_full, n_vis, edge_tile, 0)
    o_ref[...] = (acc_sc[...] / l_sc[...]).astype(o_ref.dtype)


def mla_attention(B, T, Sk, past, z_all, qr, kv, kr_all):
    H = MLA_HEADS
    tq = min(T, 512)
    tk = next((t for t in (2048, 1024, 512) if Sk % t == 0), Sk)
    nq, nk = T // tq, Sk // tk
    z3 = z_all.reshape(B, T, NZ)
    qr3 = qr.reshape(B, T, H * LANES)
    kv3 = kv.reshape(B, Sk, H * 2 * LANES)
    kr3 = kr_all.reshape(B, Sk, LANES)
    out = pl.pallas_call(
        functools.partial(_attn_kernel, tq=tq, tk=tk, past=past, nk=nk),
        grid=(B, H, nq),
        in_specs=[pl.BlockSpec((None, tq, LANES), lambda b, h, qi: (b, qi, OFF_QN // LANES + h)),
                  pl.BlockSpec((None, tq, LANES), lambda b, h, qi: (b, qi, h)),
                  pl.BlockSpec((None, Sk, LANES), lambda b, h, qi: (b, 0, 2 * h)),
                  pl.BlockSpec((None, Sk, LANES), lambda b, h, qi: (b, 0, 0)),
                  pl.BlockSpec((None, Sk, LANES), lambda b, h, qi: (b, 0, 2 * h + 1))],
        out_specs=pl.BlockSpec((None, tq, LANES), lambda b, h, qi: (b, qi, h)),
        out_shape=jax.ShapeDtypeStruct((B, T, H * MLA_V), BF16),
        scratch_shapes=[pltpu.VMEM((tq, 1), F32), pltpu.VMEM((tq, 1), F32), pltpu.VMEM((tq, MLA_V), F32)],
        compiler_params=_cparams(("parallel", "parallel", "arbitrary")),
        name="mla_attention",
    )(z3, qr3, kv3, kr3, kv3)
    return out.reshape(B * T, H * MLA_V)


def _attn_latent_kernel(qn_ref, qr_ref, lat_ref, kr_ref, w_ref, o_ref, *, T, Sk, past):
    lat = lat_ref[...]
    kr = kr_ref[...]
    q_chunk = (past + lax.broadcasted_iota(jnp.int32, (T, 1), 0)) // CHUNK
    k_chunk = lax.broadcasted_iota(jnp.int32, (1, Sk), 1) // CHUNK
    visible = k_chunk <= q_chunk
    per_block = w_ref.shape[2] // (MLA_NOPE + MLA_V)
    hs = range(MLA_HEADS)
    sls = [slice(h * LANES, (h + 1) * LANES) for h in hs]
    c0 = [(h % per_block) * (MLA_NOPE + MLA_V) for h in hs]
    wk = [w_ref[h // per_block, :, c0[h]:c0[h] + MLA_NOPE] for h in hs]
    wv = [w_ref[h // per_block, :, c0[h] + MLA_NOPE:c0[h] + MLA_NOPE + MLA_V] for h in hs]
    qa = [_dot_nt((qn_ref[:, sls[h]] * MLA_SCALE).astype(BF16), wk[h]).astype(BF16) for h in hs]
    s = [jnp.where(visible, _dot_nt(qa[h], lat) + _dot_nt(qr_ref[:, sls[h]], kr), -jnp.inf) for h in hs]
    p = [jnp.exp(x - jnp.max(x, axis=1, keepdims=True)) for x in s]
    ctx = [_dot(x.astype(BF16), lat) / jnp.sum(x, axis=1, keepdims=True) for x in p]
    for h in hs:
        o_ref[:, sls[h]] = _dot(ctx[h].astype(BF16), wv[h]).astype(o_ref.dtype)


def mla_attention_latent(B, T, Sk, past, z_all, qr, lat_all, kr_all, w_kv):
    H = MLA_HEADS
    w, g = w_kv
    _, nj, K, tn = w.shape
    W = H * LANES
    out = pl.pallas_call(
        functools.partial(_attn_latent_kernel, T=T, Sk=Sk, past=past),
        grid=(B,),
        in_specs=[pl.BlockSpec((None, T, W), lambda b: (b, 0, OFF_QN // W)),
                  pl.BlockSpec((None, T, W), lambda b: (b, 0, 0)),
                  pl.BlockSpec((None, Sk, MLA_LATENT), lambda b: (b, 0, 0)),
                  pl.BlockSpec((None, Sk, LANES), lambda b: (b, 0, 0)),
                  pl.BlockSpec((None, nj, K, tn), lambda b: (g, 0, 0, 0))],
        out_specs=pl.BlockSpec((None, T, W), lambda b: (b, 0, 0)),
        out_shape=jax.ShapeDtypeStruct((B, T, H * MLA_V), BF16),
        compiler_params=_cparams(("parallel",)),
        name="mla_attention_latent",
    )(z_all.reshape(B, T, NZ), qr.reshape(B, T, W), lat_all.reshape(B, Sk, MLA_LATENT), kr_all.reshape(B, Sk, LANES), w)
    return out.reshape(B * T, H * MLA_V)


def _pair_sum_matrix():
    r = lax.broadcasted_iota(jnp.int32, (LANES, LANES), 0) // RW_HEAD
    c = lax.broadcasted_iota(jnp.int32, (LANES, LANES), 1) // RW_HEAD
    return (r == c).astype(BF16)


def _head_sum(x, bd):
    m = x.shape[0]
    parts = []
    for p in range(RW_WIDTH // LANES):
        hi, lo = _split_bf16(x[:, p * LANES:(p + 1) * LANES])
        both = _dot(jnp.concatenate([hi, lo], axis=0), bd)
        parts.append(both[:m] + both[m:])
    return jnp.concatenate(parts, axis=1)


def _rwkv_pre_kernel(r_ref, k_ref, v_ref, lo_ref, tr_ref, tk_ref, tv_ref, tlo_ref, fr_ref, fk_ref, fv_ref, flo_ref,
                     mu_ref, mulo_ref, w0_ref, w2_ref, a0_ref, a2_ref, g2_ref, kk_ref, ka_ref, rk_ref,
                     ro_ref, wo_ref, ko_ref, vo_ref, ao_ref, bo_ref, go_ref, bonus_ref, *, T):
    W = RW_WIDTH
    tm = r_ref.shape[0]
    row = lax.broadcasted_iota(jnp.int32, (tm, 1), 0)
    stream_start = (pl.program_id(0) * tm + row) % T == 0

    def mix(z_ref, tail_ref, first_ref, mu):
        z = z_ref[...]
        prev = jnp.where(row == 0, tail_ref[7:8, :], pltpu.roll(z, 1, 0))
        prev = jnp.where(stream_start, first_ref[...], prev)
        return z + (prev - z) * mu

    r = mix(r_ref, tr_ref, fr_ref, mu_ref[:, 0:W])
    k = mix(k_ref, tk_ref, fk_ref, mu_ref[:, W:2 * W])
    v = mix(v_ref, tv_ref, fv_ref, mu_ref[:, 2 * W:3 * W])
    lo = mix(lo_ref, tlo_ref, flo_ref, mulo_ref[...])
    wd, ad, gd = lo[:, 0:64], lo[:, 64:128], lo[:, 128:256]
    hdot = lambda x, w: _mm(x, w, "nn", 3)
    wpre = w0_ref[...] + hdot(jnp.tanh(wd), w2_ref[...])
    w_raw = jnp.minimum(wpre, 0.0) - jnp.log(1.0 + jnp.exp(-jnp.abs(wpre))) - 0.5
    log_decay = -jnp.exp(w_raw)
    a = _sigmoid(a0_ref[...] + hdot(ad, a2_ref[...]))
    g = hdot(_sigmoid(gd), g2_ref[...])
    bd = _pair_sum_matrix()
    kk = k * kk_ref[...]
    kk = kk / jnp.maximum(jnp.sqrt(_head_sum(kk * kk, bd)), 1e-12)
    k = k * (1.0 + (a - 1.0) * ka_ref[...])
    ro_ref[...] = r
    wo_ref[...] = log_decay
    ko_ref[...] = k
    vo_ref[...] = v
    ao_ref[...] = -kk
    bo_ref[...] = kk * a
    go_ref[...] = g
    bonus_ref[...] = _head_sum(r * k * rk_ref[...], bd) * v


def rwkv_pre(rows, z_all, shift0, mu, w0, w2, a0, a2, g2, k_k, k_a, r_k):
    M = rows.M
    tm = min(rows.tm, 256)
    W = RW_WIDTH
    zs = lambda w, off: pl.BlockSpec((tm, w), lambda i: (i, off // w))
    tail = lambda w, off: pl.BlockSpec((8, w), lambda i: (jnp.maximum(i * (tm // 8) - 1, 0), off // w))
    if rows.R == 1:
        first = lambda w: pl.BlockSpec((None, 1, w), lambda i: (i // (rows.T // tm), 0, 0))
    else:
        first = lambda w: pl.BlockSpec((None, tm, w), lambda i: (0, i, 0))
    full = lambda a: pl.BlockSpec(a.shape, lambda i: (0,) * a.ndim)
    row = lambda a: a.reshape(1, -1)
    firsts = [rows.expand(shift0[:, o:o + w]) for o, w in ((0, W), (W, W), (2 * W, W), (3 * W, RW_LORA))]
    params = [row(mu[:3 * W]), row(mu[3 * W:]), row(w0), w2, row(a0), a2, g2, row(k_k), row(k_a), row(r_k)]
    cols = ((W, OFF_RW_R), (W, OFF_RW_K), (W, OFF_RW_V), (RW_LORA, OFF_RW_LORA))
    outs = pl.pallas_call(
        functools.partial(_rwkv_pre_kernel, T=rows.T),
        grid=(M // tm,),
        in_specs=[zs(*c) for c in cols] + [tail(*c) for c in cols] + [first(c[0]) for c in cols]
        + [full(p) for p in params],
        out_specs=[pl.BlockSpec((tm, W), lambda i: (i, 0))] * 8,
        out_shape=[jax.ShapeDtypeStruct((M, W), F32)] * 8,
        compiler_params=_cparams(("parallel",)),
        name="rwkv_pre",
    )(*([z_all] * 8), *firsts, *params)
    return outs


_DIMS = {"nn": (((1,), (0,)), ((), ())), "nt": (((1,), (1,)), ((), ())), "tn": (((0,), (0,)), ((), ()))}


def _split_bf16(x):
    hi = x.astype(BF16)
    return hi, (x - hi.astype(F32)).astype(BF16)


def _mm(a, b, kind, passes):
    dims = _DIMS[kind]
    if passes == 6:
        return lax.dot_general(a, b, dims, precision=HIGHEST, preferred_element_type=F32)
    dg = lambda x, y: lax.dot_general(x, y, dims, preferred_element_type=F32)
    if passes == 1:
        return dg(a.astype(BF16), b.astype(BF16))
    ah, al = _split_bf16(a)
    bh, bl = _split_bf16(b)
    m = a.shape[0]
    if kind == "tn" or m < RW_HEAD:
        return dg(ah, bh) + (dg(al, bh) + dg(ah, bl))
    both = dg(jnp.concatenate([ah, al], axis=0), bh)
    return both[:m] + (both[m:] + dg(ah, bl))


def _rwkv_chunk_kernel(r_ref, lw_ref, k_ref, v_ref, a_ref, b_ref, S0_ref, y_ref, S_ref, *, L, P_G, P_T, P_X, P_S):
    c = pl.program_id(1)

    @pl.when(c == 0)
    def _():
        S_ref[...] = S0_ref[...]

    N = RW_HEAD
    row = lax.broadcasted_iota(jnp.int32, (L, L), 0)
    col = lax.broadcasted_iota(jnp.int32, (L, L), 1)
    incl = row >= col
    strict = row > col
    eye = (row == col).astype(F32)
    tri = incl.astype(F32)
    cum_all = jnp.dot(tri, lw_ref[...], precision=HIGHEST, preferred_element_type=F32)
    sls = [slice(h * N, (h + 1) * N) for h in range(RW_HEADS)]
    each = lambda f, *lists: [f(*xs) for xs in zip(*lists)]
    S0 = [S_ref[0, h] for h in range(RW_HEADS)]
    lw = [lw_ref[:, s] for s in sls]
    cum = [cum_all[:, s] for s in sls]
    V = [v_ref[:, s] for s in sls]
    g_in = each(jnp.exp, cum)
    g_inv = each(lambda c: jnp.exp(-c), cum)
    At = each(lambda s, c, w: a_ref[:, s] * jnp.exp(c - w), sls, cum, lw)
    Bt = each(lambda s, g: b_ref[:, s] * g, sls, g_inv)
    Kt = each(lambda s, g: k_ref[:, s] * g, sls, g_inv)
    Rt = each(lambda s, g: r_ref[:, s] * g, sls, g_in)
    right = each(lambda x, y: jnp.concatenate([x, y], axis=0), Bt, Kt)
    G = each(lambda x, y, rt: _mm(jnp.concatenate([x, y], axis=0), rt, "nt", P_G), At, Rt, right)
    Nab = each(lambda g: jnp.where(strict, g[:L, :L], 0.0), G)
    Nak = each(lambda g: jnp.where(strict, g[:L, L:], 0.0), G)
    Mrb = each(lambda g: jnp.where(incl, g[L:, :L], 0.0), G)
    Mrk = each(lambda g: jnp.where(incl, g[L:, L:], 0.0), G)
    T = each(lambda n: eye + jnp.where(row // 2 == col // 2, n, 0.0), Nab)
    s = 2
    while s < L:
        off = (row // (2 * s) == col // (2 * s)) & (row // s != col // s)
        T = each(lambda t, n: t + _mm(t, _mm(jnp.where(off, n, 0.0), t, "nn", P_T), "nn", P_T), T, Nab)
        s *= 2
    NV = each(lambda n, v: _mm(n, v, "nn", P_X), Nak, V)
    TX = each(lambda t, x, nv: _mm(t, jnp.concatenate([x, nv], axis=1), "nn", P_X), T, At, NV)
    MX = each(lambda m, tx: _mm(m, tx, "nn", P_X), Mrb, TX)
    MV = each(lambda m, v: _mm(m, v, "nn", P_X), Mrk, V)
    Y = each(lambda rt, mx, mv, s0: _mm(rt + mx[:, :N], s0, "nt", P_S) + (mx[:, N:] + mv), Rt, MX, MV, S0)
    U = each(lambda tx, s0: _mm(tx[:, :N], s0, "nt", P_S) + tx[:, N:], TX, S0)
    upd = each(lambda u, v, bt, kt: _mm(u, bt, "tn", P_S) + _mm(v, kt, "tn", P_S), U, V, Bt, Kt)
    for h, s in enumerate(sls):
        y_ref[:, s] = Y[h]
        S_ref[0, h] = (S0[h] + upd[h]) * g_in[h][L - 1:L, :]


def rwkv_scan(B, T, r, lw, k, v, a, b, S0):
    L = CHUNK if T % CHUNK == 0 else T
    assert L & (L - 1) == 0
    nc = T // L
    xs = pl.BlockSpec((L, RW_WIDTH), lambda bi, c: (bi * nc + c, 0))
    ss = pl.BlockSpec((1, RW_HEADS, RW_HEAD, RW_HEAD), lambda bi, c: (bi, 0, 0, 0))
    return pl.pallas_call(
        functools.partial(_rwkv_chunk_kernel, L=L, P_G=3, P_T=3, P_X=3, P_S=3),
        grid=(B, nc),
        in_specs=[xs] * 6 + [ss],
        out_specs=[xs, ss],
        out_shape=[jax.ShapeDtypeStruct((B * T, RW_WIDTH), F32),
                   jax.ShapeDtypeStruct((B, RW_HEADS, RW_HEAD, RW_HEAD), F32)],
        compiler_params=_cparams(("parallel", "arbitrary")),
        name="rwkv_scan",
    )(r, lw, k, v, a, b, S0)


def _rwkv_post_kernel(y_ref, g_ref, bonus_ref, lg_ref, lb_ref, o_ref):
    bd = _pair_sum_matrix()
    y = y_ref[...]
    mean = _head_sum(y, bd) * (1.0 / RW_HEAD)
    d = y - mean
    var = _head_sum(d * d, bd) * (1.0 / RW_HEAD)
    yn = d * lax.rsqrt(var + RW_LN_EPS) * lg_ref[...] + lb_ref[...]
    o_ref[...] = ((yn + bonus_ref[...]) * g_ref[...]).astype(o_ref.dtype)


def rwkv_post(rows, y, g, bonus, lnx_g, lnx_b):
    M = rows.M
    tm = min(rows.tm, 256)
    W = RW_WIDTH
    xs = pl.BlockSpec((tm, W), lambda i: (i, 0))
    ps = pl.BlockSpec((1, W), lambda i: (0, 0))
    return pl.pallas_call(
        _rwkv_post_kernel,
        grid=(M // tm,),
        in_specs=[xs, xs, xs, ps, ps],
        out_specs=xs,
        out_shape=jax.ShapeDtypeStruct((M, W), BF16),
        compiler_params=_cparams(("parallel",)),
        name="rwkv_post",
    )(y, g, bonus, lnx_g.reshape(1, W), lnx_b.reshape(1, W))


def _cast_blocks_kernel(x_ref, o_ref):
    o_ref[...] = x_ref[...].astype(o_ref.dtype)


def cast_blocks(w):
    lead = w.shape[:-2]
    K, N = w.shape[-2:]
    G = int(np.prod(lead))
    tk = _row_tile(K, 1024)

    def src(g, j, k):
        idx = []
        for d in reversed(lead):
            idx.append(g % d)
            g = g // d
        return tuple(reversed(idx)) + (k, j)

    return pl.pallas_call(
        _cast_blocks_kernel,
        grid=(G, N // TN, K // tk),
        in_specs=[pl.BlockSpec((None,) * len(lead) + (tk, TN), src)],
        out_specs=pl.BlockSpec((None, None, tk, TN), lambda g, j, k: (g, j, k, 0)),
        out_shape=jax.ShapeDtypeStruct((G, N // TN, K, TN), BF16),
        compiler_params=_cparams(("parallel", "parallel", "parallel")),
        name="cast_blocks",
    )(w)


SRC_IF = 3072
SRC_Q = SRC_IF + 2 * ML_HEADS
SRC_LAT = SRC_Q + MLA_HEADS * (MLA_NOPE + MLA_ROPE)
SRC_KR = SRC_LAT + MLA_LATENT
SRC_RW = SRC_KR + MLA_ROPE
SRC_LO = SRC_RW + 3 * RW_WIDTH
SRC_GATE = SRC_LO + RW_LORA
J_QN, J_QR, J_RW, J_GATE, J_TAIL = OFF_QN // TN_IN, OFF_QR // TN_IN, OFF_RW_R // TN_IN, OFF_GATE // TN_IN, OFF_LAT // TN_IN


def _prep_w_in_t_kernel(main_ref, q_ref, lo_ref, if_ref, o_ref):
    j = pl.program_id(2)
    kw = main_ref.shape[1]
    head = MLA_NOPE + MLA_ROPE

    def put(rows):
        o_ref[...] = rows.T.astype(o_ref.dtype)

    @pl.when((j != J_QN) & (j != J_QR) & (j != J_TAIL))
    def _():
        put(main_ref[...])

    @pl.when(j == J_QN)
    def _():
        put(jnp.concatenate([q_ref[h * head:h * head + MLA_NOPE, :] for h in range(MLA_HEADS)], axis=0))

    @pl.when(j == J_QR)
    def _():
        zeros = jnp.zeros((LANES - MLA_ROPE, kw), F32)
        put(jnp.concatenate([x for h in range(MLA_HEADS)
                             for x in (q_ref[h * head + MLA_NOPE:(h + 1) * head, :], zeros)], axis=0))

    @pl.when(j == J_TAIL)
    def _():
        put(jnp.concatenate([main_ref[0:MLA_LATENT, :], lo_ref[...],
                             main_ref[MLA_LATENT:MLA_LATENT + MLA_ROPE, :], jnp.zeros((LANES - MLA_ROPE, kw), F32),
                             if_ref[...], jnp.zeros((LANES - 2 * ML_HEADS, kw), F32)], axis=0))


def prep_w_in_t(w):
    G, K, N = w.shape
    assert (OFF_ML_Q, OFF_QN, OFF_QR, OFF_RW_R, OFF_GATE, OFF_LAT) == (0, 3072, 4096, 5120, 8192, 14336) and TN_IN == 1024
    assert (OFF_RW_LORA, OFF_KR, OFF_IF) == (OFF_LAT + 512, OFF_LAT + 768, OFF_LAT + 896) and N == SRC_GATE + 3 * D_MODEL
    wt = jnp.swapaxes(w, 1, 2)
    kw = K // 2

    def main_row(j):
        row = jnp.where(j < J_QN, j * TN_IN,
                        jnp.where(j < J_RW, SRC_Q,
                                  jnp.where(j < J_GATE, SRC_RW + (j - J_RW) * TN_IN,
                                            jnp.where(j < J_TAIL, SRC_GATE + (j - J_GATE) * TN_IN, SRC_LAT))))
        return pl.multiple_of(row, 8)

    el = lambda n: pl.Element(n)
    return pl.pallas_call(
        _prep_w_in_t_kernel,
        grid=(G, 2, NZ // TN_IN),
        in_specs=[pl.BlockSpec((None, el(TN_IN), el(kw)), lambda g, kh, j: (g, main_row(j), kh * kw)),
                  pl.BlockSpec((None, el(MLA_HEADS * (MLA_NOPE + MLA_ROPE)), el(kw)), lambda g, kh, j: (g, SRC_Q, kh * kw)),
                  pl.BlockSpec((None, el(RW_LORA), el(kw)), lambda g, kh, j: (g, SRC_LO, kh * kw)),
                  pl.BlockSpec((None, el(2 * ML_HEADS), el(kw)), lambda g, kh, j: (g, SRC_IF, kh * kw))],
        out_specs=pl.BlockSpec((None, None, kw, TN_IN), lambda g, kh, j: (g, j, kh, 0)),
        out_shape=jax.ShapeDtypeStruct((G, NZ // TN_IN, K, TN_IN), BF16),
        compiler_params=_cparams(("parallel", "parallel", "arbitrary")),
        name="prep_w_in",
    )(wt, wt, wt, wt)


def _row_tile(M, cap):
    return max(t for t in range(8, cap + 1, 8) if M % t == 0)


def _rope_tables(T, past, reps):
    half = MLA_ROPE // 2
    inv = ROPE_BASE ** (-jnp.arange(half, dtype=F32) / half)
    ang = (past + jnp.arange(T)).astype(F32)[:, None] * inv[None, :]
    cos, sin = jnp.cos(ang), jnp.sin(ang)
    zeros = jnp.zeros((T, LANES - MLA_ROPE), F32)
    cos_t = jnp.concatenate([cos, cos, zeros], axis=1)
    sin_t = jnp.concatenate([-sin, sin, zeros], axis=1)
    return jnp.tile(cos_t, (reps, 1)), jnp.tile(sin_t, (reps, 1))


def _layer(rows, x, mod, l, W, P, lat_past, kr_past, C0, n0, m0, S0, shift0, tabs):
    B, T, M = rows.B, rows.T, rows.M
    D = D_MODEL
    m9 = [rows.expand(mod[:, i * D:(i + 1) * D]) for i in range(N_ADA)]
    sh1, sc1, g1, sh2, sc2, g2, sh3, sc3, g3 = m9
    act = mm_swiglu(rows, x, P['norm_g'][l, 0], sh1, sc1, (W['ffn_in'], 2 * l))
    x = mm_resid(rows, act, (W['ffn_out'], 2 * l), x, g1, 0.5)
    z_all = mm_norm(rows, x, P['norm_g'][l, 1], sh2, sc2, (W['w_in'], l), name="mm_in")
    bias_row = jnp.pad(jnp.concatenate([P['mlstm_i_bias'][l], P['mlstm_f_bias'][l]]), (0, LANES - 2 * ML_HEADS))[None]
    om, C, n, m = mlstm(B, T, z_all, bias_row, P['mlstm_head_g'][l], C0, n0, m0)
    past = lat_past.shape[1]
    qr, lat, kr = mla_pre(rows, z_all, tabs[0], tabs[1], P['mla_kv_norm_g'][l])
    Sk = past + T
    if past:
        lat_all = jnp.concatenate([lat_past, lat.reshape(B, T, -1)], axis=1).reshape(B * Sk, -1)
        kr_all = jnp.concatenate([jnp.pad(kr_past, ((0, 0), (0, 0), (0, LANES - MLA_ROPE))), kr.reshape(B, T, -1)],
                                 axis=1).reshape(B * Sk, LANES)
    else:
        lat_all, kr_all = lat, kr
    if T <= LATENT_ATTN_MAX_T:
        oa = mla_attention_latent(B, T, Sk, past, z_all, qr, lat_all.astype(BF16), kr_all.astype(BF16), (W['kv_b'], l))
    else:
        kv = matmul(lat_all, (W['kv_b'], l), _row_tile(B * Sk, 1536), out_dtype=BF16, name="mm_kv")
        oa = mla_attention(B, T, Sk, past, z_all, qr, kv, kr_all.astype(BF16))
    r_, w_, k_, v_, a_, b_, g_, bonus = rwkv_pre(rows, z_all, shift0, P['rwkv_mu'][l], P['rwkv_w0'][l], P['rwkv_w2'][l],
                                                 P['rwkv_a0'][l], P['rwkv_a2'][l], P['rwkv_g2'][l], P['rwkv_k_k'][l],
                                                 P['rwkv_k_a'][l], P['rwkv_r_k'][l].reshape(-1))
    y, S = rwkv_scan(B, T, r_, w_, k_, v_, a_, b_, S0)
    orw = rwkv_post(rows, y, g_, bonus, P['rwkv_lnx_g'][l], P['rwkv_lnx_b'][l])
    z_last = z_all.reshape(B, T, NZ)[:, -1]
    shift = jnp.concatenate([z_last[:, OFF_RW_R:OFF_RW_R + 3 * RW_WIDTH], z_last[:, OFF_RW_LORA:OFF_RW_LORA + RW_LORA]],
                            axis=1)
    mixed = mm_branch(rows, om, oa, orw, (W['branch'], 3 * l), z_all, P['b_merge'][l][None])
    x = mm_resid(rows, mixed, (W['w_out'], l), x, g2, 1.0)
    act = mm_swiglu(rows, x, P['norm_g'][l, 2], sh3, sc3, (W['ffn_in'], 2 * l + 1))
    x = mm_resid(rows, act, (W['ffn_out'], 2 * l + 1), x, g3, 0.5)
    new = (lat.reshape(B, T, MLA_LATENT), kr.reshape(B, T, LANES)[:, :, :MLA_ROPE], C, n, m, S, shift)
    return x, new


def _trunk(rows, x, mods, W, P, lat_cache, kr_cache, C_st, n_st, m_st, S_st, sh_st):
    depth = len(mods)
    past = lat_cache.shape[2]
    tabs = _rope_tables(rows.T, past, rows.B)
    x = x.reshape(rows.M, D_MODEL)
    new = []
    for l in range(depth):
        x, st = _layer(rows, x, mods[l], l, W, P, lat_cache[l], kr_cache[l], C_st[l], n_st[l], m_st[l], S_st[l],
                       sh_st[l], tabs)
        new.append(st)
    y = final_norm(rows, x, P['final_norm_g']).reshape(rows.B, rows.T, D_MODEL)
    return y, tuple(jnp.stack(s, axis=0) for s in zip(*new))


def kernel(x_prompt, x_sample, c_prompt, c_sample, cache_mla_latent, cache_mla_krope, state_mlstm_C, state_mlstm_n,
           state_mlstm_m, state_rwkv_S, state_rwkv_shift, ada_w, ada_b, norm_g, ffn_w_in, ffn_w_out, w_in,
           mlstm_i_bias, mlstm_f_bias, mlstm_head_g, mla_kv_norm_g, mla_w_kv_b, rwkv_mu, rwkv_w0, rwkv_w2, rwkv_a0,
           rwkv_a2, rwkv_g2, rwkv_k_k, rwkv_k_a, rwkv_r_k, rwkv_lnx_g, rwkv_lnx_b, w_branch, b_merge, w_out,
           final_norm_g):
    P = dict(norm_g=norm_g, mlstm_i_bias=mlstm_i_bias, mlstm_f_bias=mlstm_f_bias, mlstm_head_g=mlstm_head_g,
             mla_kv_norm_g=mla_kv_norm_g, rwkv_mu=rwkv_mu, rwkv_w0=rwkv_w0, rwkv_w2=rwkv_w2, rwkv_a0=rwkv_a0,
             rwkv_a2=rwkv_a2, rwkv_g2=rwkv_g2, rwkv_k_k=rwkv_k_k, rwkv_k_a=rwkv_k_a, rwkv_r_k=rwkv_r_k,
             rwkv_lnx_g=rwkv_lnx_g, rwkv_lnx_b=rwkv_lnx_b, b_merge=b_merge, final_norm_g=final_norm_g)
    depth = w_in.shape[0]
    W = dict(ffn_in=cast_blocks(ffn_w_in), ffn_out=cast_blocks(ffn_w_out), w_in=prep_w_in_t(w_in),
             kv_b=cast_blocks(mla_w_kv_b), branch=cast_blocks(w_branch), w_out=cast_blocks(w_out))
    Bp, Tp, _ = x_prompt.shape
    Bs, Ts, _ = x_sample.shape
    n_c = Bp + Bs
    c_all = jnp.pad(jnp.concatenate([c_prompt, c_sample], axis=0), ((0, (-n_c) % 8), (0, 0)))
    mods = [ada_mod(c_all, ada_w, ada_b, l) for l in range(depth)]
    rows_p = Rows(Bp, Tp, TM_MAX)
    rows_s = Rows(Bs, Ts, TM_MAX)
    zeros = lambda *s: jnp.zeros((depth, Bp) + s, F32)
    y_p, st_p = _trunk(rows_p, x_prompt, [m[:Bp] for m in mods], W, P,
                       zeros(0, MLA_LATENT), zeros(0, MLA_ROPE), zeros(ML_HEADS, ML_DV, ML_DQK), zeros(ML_HEADS, ML_DQK),
                       zeros(ML_HEADS), zeros(RW_HEADS, RW_HEAD, RW_HEAD), zeros(RW_COLS))
    y_s, st_s = _trunk(rows_s, x_sample, [m[Bp:n_c] for m in mods], W, P,
                       cache_mla_latent, cache_mla_krope, state_mlstm_C, state_mlstm_n, state_mlstm_m, state_rwkv_S,
                       state_rwkv_shift)
    return (y_p, y_s) + st_p + st_s
```

```python
import functools

import jax
import jax.numpy as jnp
import numpy as np
from jax import lax
from jax.experimental import pallas as pl
from jax.experimental.pallas import tpu as pltpu

F32 = jnp.float32
BF16 = jnp.bfloat16
HIGHEST = lax.Precision.HIGHEST

D_MODEL = 2048
CHUNK = 64
NORM_EPS = 1e-6
D_FF = 5632
N_ADA = 9
ML_HEADS, ML_DQK, ML_DV, ML_GATE_CAP = 4, 128, 256, 15.0
ML_CHUNK = 256
MLA_HEADS, MLA_NOPE, MLA_ROPE, MLA_V, MLA_LATENT = 8, 128, 64, 128, 512
MLA_SCALE = (MLA_NOPE + MLA_ROPE) ** -0.5
LATENT_ATTN_MAX_T = 64
ROPE_BASE = 10000.0
RW_HEADS, RW_HEAD = 16, 64
RW_WIDTH = RW_HEADS * RW_HEAD
RW_LN_EPS = 64e-5
RW_LORA = 256
RW_COLS = 3 * RW_WIDTH + RW_LORA
BRANCH_WIDTH = 1024

LANES = 128
VMEM_LIMIT = 56 * 1024 * 1024
TM_MAX = 1024
TN = 512
TN_IN = 1024

OFF_ML_Q, OFF_ML_K, OFF_ML_V, OFF_ML_O = 0, 512, 1024, 2048
OFF_QN, OFF_QR = 3072, 4096
OFF_RW_R, OFF_RW_K, OFF_RW_V = 5120, 6144, 7168
OFF_GATE = 8192
OFF_LAT = 14336
OFF_RW_LORA = 14848
OFF_KR = 15104
OFF_IF = 15232
NZ = 15360
IN_COLS = 14664


def _cparams(sem):
    return pltpu.CompilerParams(dimension_semantics=sem, vmem_limit_bytes=VMEM_LIMIT)


def _sigmoid(x):
    return 1.0 / (1.0 + jnp.exp(-x))


def _silu(x):
    return x * _sigmoid(x)


def _dot(a, b):
    return jnp.dot(a, b, preferred_element_type=F32)


def _dot_nt(a, b, precision=None):
    return lax.dot_general(a, b, (((1,), (1,)), ((), ())), precision=precision, preferred_element_type=F32)


def _dot_tn(a, b, precision=None):
    return lax.dot_general(a, b, (((0,), (0,)), ((), ())), precision=precision, preferred_element_type=F32)


class Rows:
    def __init__(self, B, T, tm_max):
        self.B, self.T, self.M = B, T, B * T
        if T >= 256:
            self.tm = min(tm_max, T)
            assert T % self.tm == 0
            self.tpg = T // self.tm
            self.R = 1
        else:
            self.tm = self.M
            self.tpg = 1
            self.R = self.M
        self.nm = self.M // self.tm

    def expand(self, v):
        if self.R == 1:
            return v[:, None, :]
        return jnp.repeat(v, self.T, axis=0)[None]

    def vec_spec(self, tn, col=lambda j: j):
        return pl.BlockSpec((None, self.R, tn), lambda i, j: (i // self.tpg, 0, col(j)))


def _ada_kernel(c_ref, w_ref, b_ref, o_ref):
    c = c_ref[...]
    o_ref[...] = _dot(_silu(c).astype(BF16), w_ref[...].astype(BF16)) + b_ref[...]


def ada_mod(c, w, b, l):
    R, D = c.shape
    N = w.shape[2]
    tn = 1024
    return pl.pallas_call(
        _ada_kernel,
        grid=(N // tn,),
        in_specs=[pl.BlockSpec((R, D), lambda j: (0, 0)),
                  pl.BlockSpec((None, D, tn), lambda j: (l, 0, j)),
                  pl.BlockSpec((1, tn), lambda j: (0, j))],
        out_specs=pl.BlockSpec((R, tn), lambda j: (0, j)),
        out_shape=jax.ShapeDtypeStruct((R, N), F32),
        compiler_params=_cparams(("arbitrary",)),
        name="ada_mod",
    )(c, w, b[l].reshape(1, N))


def _final_norm_kernel(x_ref, g_ref, o_ref):
    x = x_ref[...]
    o_ref[...] = x * lax.rsqrt(jnp.mean(x * x, axis=-1, keepdims=True) + NORM_EPS) * g_ref[...]


def final_norm(rows, x, g):
    D = x.shape[1]
    tm = min(rows.tm, 512)
    return pl.pallas_call(
        _final_norm_kernel,
        grid=(rows.M // tm,),
        in_specs=[pl.BlockSpec((tm, D), lambda i: (i, 0)), pl.BlockSpec((1, D), lambda i: (0, 0))],
        out_specs=pl.BlockSpec((tm, D), lambda i: (i, 0)),
        out_shape=jax.ShapeDtypeStruct((rows.M, D), F32),
        compiler_params=_cparams(("parallel",)),
        name="final_norm",
    )(x, g.reshape(1, D))


def _mm_kernel(a_ref, w_ref, o_ref):
    o_ref[...] = _dot(a_ref[...].astype(BF16), w_ref[...]).astype(o_ref.dtype)


def matmul(a, w, tm, out_dtype=F32, name="matmul"):
    M, K = a.shape
    w, g = w
    _, nj, _, tn = w.shape
    N = nj * tn
    return pl.pallas_call(
        _mm_kernel,
        grid=(M // tm, nj),
        in_specs=[pl.BlockSpec((tm, K), lambda i, j: (i, 0)),
                  pl.BlockSpec((None, None, K, tn), lambda i, j: (g, j, 0, 0))],
        out_specs=pl.BlockSpec((tm, tn), lambda i, j: (i, j)),
        out_shape=jax.ShapeDtypeStruct((M, N), out_dtype),
        compiler_params=_cparams(("parallel", "arbitrary")),
        name=name,
    )(a, w)


def _normed_rows(x_ref, g_ref, sh_ref, sc_ref, h_sc):
    @pl.when(pl.program_id(1) == 0)
    def _():
        x = x_ref[...]
        y = x * lax.rsqrt(jnp.mean(x * x, axis=-1, keepdims=True) + NORM_EPS) * g_ref[...]
        h_sc[...] = (y * (1.0 + sc_ref[...]) + sh_ref[...]).astype(h_sc.dtype)


def _norm_specs(rows, D):
    vec = (pl.BlockSpec((None, 1, D), lambda i, j: (i // rows.tpg, 0, 0)) if rows.R == 1
           else pl.BlockSpec((None, rows.tm, D), lambda i, j: (0, i, 0)))
    return [pl.BlockSpec((rows.tm, D), lambda i, j: (i, 0)), pl.BlockSpec((1, D), lambda i, j: (0, 0)), vec, vec]


def _mm_norm_kernel(x_ref, g_ref, sh_ref, sc_ref, w_ref, o_ref, h_sc):
    _normed_rows(x_ref, g_ref, sh_ref, sc_ref, h_sc)
    o_ref[...] = _dot(h_sc[...], w_ref[...]).astype(o_ref.dtype)


def mm_norm(rows, x, g, shift, scale, w, name):
    D = x.shape[1]
    w, gi = w
    _, nj, _, tn = w.shape
    return pl.pallas_call(
        _mm_norm_kernel,
        grid=(rows.nm, nj),
        in_specs=_norm_specs(rows, D) + [pl.BlockSpec((None, None, D, tn), lambda i, j: (gi, j, 0, 0))],
        out_specs=pl.BlockSpec((rows.tm, tn), lambda i, j: (i, j)),
        out_shape=jax.ShapeDtypeStruct((rows.M, nj * tn), F32),
        scratch_shapes=[pltpu.VMEM((rows.tm, D), BF16)],
        compiler_params=_cparams(("parallel", "arbitrary")),
        name=name,
    )(x, g.reshape(1, D), shift, scale, w)


def _swiglu_kernel(x_ref, g_ref, sh_ref, sc_ref, wg_ref, wu_ref, o_ref, h_sc):
    _normed_rows(x_ref, g_ref, sh_ref, sc_ref, h_sc)
    h = h_sc[...]
    g = _dot(h, wg_ref[...])
    u = _dot(h, wu_ref[...])
    o_ref[...] = (_silu(g) * u).astype(o_ref.dtype)


def mm_swiglu(rows, x, g, shift, scale, w_in):
    D = x.shape[1]
    w_in, gi = w_in
    tn = w_in.shape[3]
    nj = D_FF // tn
    return pl.pallas_call(
        _swiglu_kernel,
        grid=(rows.nm, nj),
        in_specs=_norm_specs(rows, D) + [pl.BlockSpec((None, None, D, tn), lambda i, j: (gi, j, 0, 0)),
                                         pl.BlockSpec((None, None, D, tn), lambda i, j: (gi, j + nj, 0, 0))],
        out_specs=pl.BlockSpec((rows.tm, tn), lambda i, j: (i, j)),
        out_shape=jax.ShapeDtypeStruct((rows.M, D_FF), BF16),
        scratch_shapes=[pltpu.VMEM((rows.tm, D), BF16)],
        compiler_params=_cparams(("parallel", "arbitrary")),
        name="mm_swiglu",
    )(x, g.reshape(1, D), shift, scale, w_in, w_in)


def _resid_kernel(a_ref, w_ref, x_ref, g_ref, o_ref, *, coef):
    o_ref[...] = x_ref[...] + (coef * g_ref[...]) * _dot(a_ref[...], w_ref[...])


def mm_resid(rows, a, w, x, gate, coef):
    K = a.shape[1]
    w, g = w
    _, nj, _, tn = w.shape
    N = nj * tn
    return pl.pallas_call(
        functools.partial(_resid_kernel, coef=coef),
        grid=(rows.nm, nj),
        in_specs=[pl.BlockSpec((rows.tm, K), lambda i, j: (i, 0)),
                  pl.BlockSpec((None, None, K, tn), lambda i, j: (g, j, 0, 0)),
                  pl.BlockSpec((rows.tm, tn), lambda i, j: (i, j)),
                  rows.vec_spec(tn)],
        out_specs=pl.BlockSpec((rows.tm, tn), lambda i, j: (i, j)),
        out_shape=jax.ShapeDtypeStruct((rows.M, N), F32),
        compiler_params=_cparams(("parallel", "arbitrary")),
        name="mm_resid",
    )(a, w, x, gate)


def _branch_kernel(om_ref, oa_ref, or_ref, w0_ref, w1_ref, w2_ref, z0_ref, z1_ref, z2_ref, b0_ref, b1_ref, b2_ref,
                   o_ref):
    acc = _sigmoid(z0_ref[...] + b0_ref[...]) * _dot(om_ref[...], w0_ref[...])
    acc += _sigmoid(z1_ref[...] + b1_ref[...]) * _dot(oa_ref[...], w1_ref[...])
    acc += _sigmoid(z2_ref[...] + b2_ref[...]) * _dot(or_ref[...], w2_ref[...])
    o_ref[...] = acc.astype(o_ref.dtype)


def mm_branch(rows, om, oa, orw, w_branch, z_all, b_merge):
    w_branch, g = w_branch
    tn = w_branch.shape[3]
    D = D_MODEL
    nj = D // tn
    a_spec = pl.BlockSpec((rows.tm, BRANCH_WIDTH), lambda i, j: (i, 0))
    w_specs = [pl.BlockSpec((None, None, BRANCH_WIDTH, tn), lambda i, j, n=n: (g + n, j, 0, 0)) for n in range(3)]
    z_specs = [pl.BlockSpec((rows.tm, tn), lambda i, j, n=n: (i, (OFF_GATE + n * D) // tn + j)) for n in range(3)]
    b_specs = [pl.BlockSpec((1, tn), lambda i, j, n=n: (0, n * D // tn + j)) for n in range(3)]
    return pl.pallas_call(
        _branch_kernel,
        grid=(rows.nm, nj),
        in_specs=[a_spec, a_spec, a_spec] + w_specs + z_specs + b_specs,
        out_specs=pl.BlockSpec((rows.tm, tn), lambda i, j: (i, j)),
        out_shape=jax.ShapeDtypeStruct((rows.M, D), BF16),
        compiler_params=_cparams(("parallel", "arbitrary")),
        name="mm_branch",
    )(om, oa, orw, w_branch, w_branch, w_branch, z_all, z_all, z_all, b_merge, b_merge, b_merge)


def _mlstm_kernel(q_ref, k_ref, v_ref, o_ref, if_ref, bias_ref, hg_ref, C0_ref, n0_ref, m0_ref,
                  om_ref, C_ref, n_ref, m_ref, *, L):
    c = pl.program_id(1)

    @pl.when(c == 0)
    def _():
        C_ref[...] = C0_ref[...]
        n_ref[...] = n0_ref[...]
        m_ref[...] = m0_ref[...]

    H = ML_HEADS
    sc = ML_GATE_CAP * jnp.tanh((if_ref[...] + bias_ref[...]) * (1.0 / ML_GATE_CAP))
    lf = jnp.minimum(sc, 0.0) - jnp.log(1.0 + jnp.exp(-jnp.abs(sc)))
    row = lax.broadcasted_iota(jnp.int32, (L, L), 0)
    col = lax.broadcasted_iota(jnp.int32, (L, L), 1)
    causal = row >= col
    bcum = jnp.dot(causal.astype(F32), lf, precision=HIGHEST, preferred_element_type=F32)
    sel = (lax.broadcasted_iota(jnp.int32, (8, LANES), 0) == lax.broadcasted_iota(jnp.int32, (8, LANES), 1)).astype(F32)
    scT = _dot_nt(sel, sc, HIGHEST)
    bT = _dot_nt(sel, bcum, HIGHEST)
    hs = range(H)
    each = lambda f, *lists: [f(*xs) for xs in zip(*lists)]
    qsl = [slice(h * ML_DQK, (h + 1) * ML_DQK) for h in hs]
    vsl = [slice(h * ML_DV, (h + 1) * ML_DV) for h in hs]
    b_col = [bcum[:, H + h:H + h + 1] for h in hs]
    b_row = [bT[H + h:H + h + 1, :] for h in hs]
    ig_row = [scT[h:h + 1, :] for h in hs]
    ig_col = [sc[:, h:h + 1] for h in hs]
    m_prev = [m_ref[0, h, :, 0:1] for h in hs]
    C = [C_ref[0, h] for h in hs]
    n = [n_ref[0, h] for h in hs]
    qf = [q_ref[:, s] for s in qsl]
    q = each(lambda x: x.astype(BF16), qf)
    kb = [(k_ref[:, s] * (ML_DQK ** -0.5)).astype(BF16) for s in qsl]
    v = [v_ref[:, s] for s in vsl]
    logD = each(lambda bc, br, ir: jnp.where(causal, bc - br + ir, -jnp.inf), b_col, b_row, ig_row)
    g = each(lambda bc, mp: bc + mp, b_col, m_prev)
    m_t = each(lambda gg, ld: jnp.maximum(gg, jnp.max(ld, axis=1, keepdims=True)), g, logD)
    Dm = each(lambda ld, mt: jnp.exp(ld - mt), logD, m_t)
    inter = each(lambda gg, mt: jnp.exp(gg - mt), g, m_t)
    qk = each(lambda qq, kk, d: _dot_nt(qq, kk) * d, q, kb, Dm)
    qC = each(lambda qq, cc: _dot_nt(qq, cc.astype(BF16)), q, C)
    num = each(lambda s, vv, it, x: _dot(s.astype(BF16), vv.astype(BF16)) + it * x, qk, v, inter, qC)
    qn = each(lambda x, nn: jnp.sum(x * nn, axis=1, keepdims=True), qf, n)
    den = each(lambda s, it, x: jnp.sum(s, axis=1, keepdims=True) + it * x, qk, inter, qn)
    hh = each(lambda nu, de, mt: nu / jnp.maximum(jnp.abs(de), jnp.exp(-mt)), num, den, m_t)
    m_new = each(lambda mt: mt[L - 1:L, :], m_t)
    b_last = each(lambda bc: bc[L - 1:L, :], b_col)
    carry_w = each(lambda bl, mp, mn: jnp.exp(bl + mp - mn), b_last, m_prev, m_new)
    w_s = each(lambda bl, bc, ic, mn: jnp.exp(bl - bc + ic - mn), b_last, b_col, ig_col, m_new)
    dC = each(lambda w, vv, kk: _dot_tn((w * vv).astype(BF16), kk), w_s, v, kb)
    dn = each(lambda w, kk: jnp.sum(w * kk.astype(F32), axis=0, keepdims=True), w_s, kb)
    for h in hs:
        y = hh[h] * lax.rsqrt(jnp.mean(hh[h] * hh[h], axis=-1, keepdims=True) + NORM_EPS) * hg_ref[:, vsl[h]]
        om_ref[:, vsl[h]] = (_sigmoid(o_ref[:, vsl[h]]) * y).astype(om_ref.dtype)
        C_ref[0, h] = carry_w[h] * C[h] + dC[h]
        n_ref[0, h] = carry_w[h] * n[h] + dn[h]
        m_ref[0, h] = jnp.broadcast_to(m_new[h], (1, LANES))


def mlstm(B, T, z_all, bias_row, head_g, C0, n0, m0):
    L = next((c for c in (ML_CHUNK, CHUNK) if T % c == 0), T)
    nc = T // L
    H = ML_HEADS
    zrow = lambda w, off: pl.BlockSpec((L, w), lambda b, c: (b * nc + c, off // w))
    st = lambda *shape: pl.BlockSpec((1,) + shape, lambda b, c: (b,) + (0,) * len(shape))
    om, C, n, m = pl.pallas_call(
        functools.partial(_mlstm_kernel, L=L),
        grid=(B, nc),
        in_specs=[zrow(H * ML_DQK, OFF_ML_Q), zrow(H * ML_DQK, OFF_ML_K), zrow(H * ML_DV, OFF_ML_V),
                  zrow(H * ML_DV, OFF_ML_O), zrow(LANES, OFF_IF),
                  pl.BlockSpec((1, LANES), lambda b, c: (0, 0)),
                  pl.BlockSpec((1, H * ML_DV), lambda b, c: (0, 0)),
                  st(H, ML_DV, ML_DQK), st(H, 1, ML_DQK), st(H, 1, LANES)],
        out_specs=[pl.BlockSpec((L, H * ML_DV), lambda b, c: (b * nc + c, 0)),
                   st(H, ML_DV, ML_DQK), st(H, 1, ML_DQK), st(H, 1, LANES)],
        out_shape=[jax.ShapeDtypeStruct((B * T, H * ML_DV), BF16),
                   jax.ShapeDtypeStruct((B, H, ML_DV, ML_DQK), F32),
                   jax.ShapeDtypeStruct((B, H, 1, ML_DQK), F32),
                   jax.ShapeDtypeStruct((B, H, 1, LANES), F32)],
        compiler_params=_cparams(("parallel", "arbitrary")),
        name="mlstm",
    )(z_all, z_all, z_all, z_all, z_all, bias_row, head_g.reshape(1, -1), C0, n0.reshape(B, H, 1, ML_DQK),
      jnp.broadcast_to(m0[:, :, None, None], (B, H, 1, LANES)))
    return om, C, n.reshape(B, H, ML_DQK), m[:, :, 0, 0]


def _rope_lanes(x, cos, sin):
    swapped = pltpu.roll(x, 32, 1) + pltpu.roll(x, 96, 1)
    return x * cos + swapped * sin


def _mla_pre_kernel(qr_ref, lat_ref, kr_ref, cos_ref, sin_ref, g_ref, qro_ref, lato_ref, kro_ref):
    cos = cos_ref[...]
    sin = sin_ref[...]
    for h in range(MLA_HEADS):
        sl = slice(h * LANES, (h + 1) * LANES)
        qro_ref[:, sl] = (_rope_lanes(qr_ref[:, sl], cos, sin) * MLA_SCALE).astype(qro_ref.dtype)
    kro_ref[...] = _rope_lanes(kr_ref[...], cos, sin)
    lat = lat_ref[...]
    lato_ref[...] = lat * lax.rsqrt(jnp.mean(lat * lat, axis=-1, keepdims=True) + NORM_EPS) * g_ref[...]


def mla_pre(rows, z_all, cos_tab, sin_tab, kv_norm_g):
    tm = min(rows.tm, 256)
    M = rows.M
    zspec = lambda w, off: pl.BlockSpec((tm, w), lambda i: (i, off // w))
    W = MLA_HEADS * LANES
    return pl.pallas_call(
        _mla_pre_kernel,
        grid=(M // tm,),
        in_specs=[zspec(W, OFF_QR), zspec(MLA_LATENT, OFF_LAT), zspec(LANES, OFF_KR),
                  pl.BlockSpec((tm, LANES), lambda i: (i, 0)), pl.BlockSpec((tm, LANES), lambda i: (i, 0)),
                  pl.BlockSpec((1, MLA_LATENT), lambda i: (0, 0))],
        out_specs=[pl.BlockSpec((tm, W), lambda i: (i, 0)), pl.BlockSpec((tm, MLA_LATENT), lambda i: (i, 0)),
                   pl.BlockSpec((tm, LANES), lambda i: (i, 0))],
        out_shape=[jax.ShapeDtypeStruct((M, W), BF16), jax.ShapeDtypeStruct((M, MLA_LATENT), F32),
                   jax.ShapeDtypeStruct((M, LANES), F32)],
        compiler_params=_cparams(("parallel",)),
        name="mla_pre",
    )(z_all, z_all, z_all, cos_tab, sin_tab, kv_norm_g.reshape(1, -1))


def _attn_kernel(qn_ref, qr_ref, kn_ref, kr_ref, v_ref, o_ref, m_sc, l_sc, acc_sc, *, tq, tk, past, nk):
    qi = pl.program_id(2)
    q = jnp.concatenate([(qn_ref[...] * MLA_SCALE).astype(BF16), qr_ref[...]], axis=1)
    m_sc[...] = jnp.full_like(m_sc, -jnp.inf)
    l_sc[...] = jnp.zeros_like(l_sc)
    acc_sc[...] = jnp.zeros_like(acc_sc)
    q0 = past + qi * tq
    n_full = jnp.minimum(((q0 // CHUNK + 1) * CHUNK) // tk, nk)
    n_vis = jnp.minimum((((q0 + tq - 1) // CHUNK + 1) * CHUNK + tk - 1) // tk, nk)

    def tile(kj, masked):
        k0 = pl.multiple_of(kj * tk, tk)
        k = jnp.concatenate([kn_ref[pl.ds(k0, tk), :], kr_ref[pl.ds(k0, tk), :]], axis=1)
        s = _dot_nt(q, k)
        if masked:
            q_chunk = (q0 + lax.broadcasted_iota(jnp.int32, (tq, 1), 0)) // CHUNK
            k_chunk = (k0 + lax.broadcasted_iota(jnp.int32, (1, tk), 1)) // CHUNK
            s = jnp.where(k_chunk <= q_chunk, s, -jnp.inf)
        m_prev = m_sc[...]
        m_new = jnp.maximum(m_prev, jnp.max(s, axis=1, keepdims=True))
        alpha = jnp.exp(m_prev - m_new)
        p = jnp.exp(s - m_new)
        l_sc[...] = alpha * l_sc[...] + jnp.sum(p, axis=1, keepdims=True)
        acc_sc[...] = alpha * acc_sc[...] + _dot(p.astype(BF16), v_ref[pl.ds(k0, tk), :])
        m_sc[...] = m_new

    def full_tile(kj, c):
        tile(kj, False)
        return c

    def edge_tile(kj, c):
        tile(kj, True)
        return c

    lax.fori_loop(0, n_full, full_tile, 0)
    lax.fori_loop(n_full, n_vis, edge_tile, 0)
    o_ref[...] = (acc_sc[...] / l_sc[...]).astype(o_ref.dtype)


def mla_attention(B, T, Sk, past, z_all, qr, kv, kr_all):
    H = MLA_HEADS
    tq = min(T, 512)
    tk = next((t for t in (2048, 1024, 512) if Sk % t == 0), Sk)
    nq, nk = T // tq, Sk // tk
    z3 = z_all.reshape(B, T, NZ)
    qr3 = qr.reshape(B, T, H * LANES)
    kv3 = kv.reshape(B, Sk, H * 2 * LANES)
    kr3 = kr_all.reshape(B, Sk, LANES)
    out = pl.pallas_call(
        functools.partial(_attn_kernel, tq=tq, tk=tk, past=past, nk=nk),
        grid=(B, H, nq),
        in_specs=[pl.BlockSpec((None, tq, LANES), lambda b, h, qi: (b, qi, OFF_QN // LANES + h)),
                  pl.BlockSpec((None, tq, LANES), lambda b, h, qi: (b, qi, h)),
                  pl.BlockSpec((None, Sk, LANES), lambda b, h, qi: (b, 0, 2 * h)),
                  pl.BlockSpec((None, Sk, LANES), lambda b, h, qi: (b, 0, 0)),
                  pl.BlockSpec((None, Sk, LANES), lambda b, h, qi: (b, 0, 2 * h + 1))],
        out_specs=pl.BlockSpec((None, tq, LANES), lambda b, h, qi: (b, qi, h)),
        out_shape=jax.ShapeDtypeStruct((B, T, H * MLA_V), BF16),
        scratch_shapes=[pltpu.VMEM((tq, 1), F32), pltpu.VMEM((tq, 1), F32), pltpu.VMEM((tq, MLA_V), F32)],
        compiler_params=_cparams(("parallel", "parallel", "arbitrary")),
        name="mla_attention",
    )(z3, qr3, kv3, kr3, kv3)
    return out.reshape(B * T, H * MLA_V)


def _attn_latent_kernel(qn_ref, qr_ref, latp_ref, krp_ref, latn_ref, krn_ref, w_ref, o_ref, *, T, past):
    q_chunk = (past + lax.broadcasted_iota(jnp.int32, (T, 1), 0)) // CHUNK
    segs = []
    for lat_ref, kr_ref, k0, n in ((latp_ref, krp_ref, 0, past), (latn_ref, krn_ref, past, T)):
        k_chunk = (k0 + lax.broadcasted_iota(jnp.int32, (1, n), 1)) // CHUNK
        segs.append((lat_ref[...].astype(BF16), kr_ref[...].astype(BF16), kr_ref.shape[1], k_chunk <= q_chunk))
    per_block = w_ref.shape[2] // (MLA_NOPE + MLA_V)
    hs = range(MLA_HEADS)
    sls = [slice(h * LANES, (h + 1) * LANES) for h in hs]
    c0 = [(h % per_block) * (MLA_NOPE + MLA_V) for h in hs]
    wk = [w_ref[h // per_block, :, c0[h]:c0[h] + MLA_NOPE] for h in hs]
    wv = [w_ref[h // per_block, :, c0[h] + MLA_NOPE:c0[h] + MLA_NOPE + MLA_V] for h in hs]
    qa = [_dot_nt((qn_ref[:, sls[h]] * MLA_SCALE).astype(BF16), wk[h]).astype(BF16) for h in hs]
    qr = [qr_ref[:, sls[h]] for h in hs]
    s = [[jnp.where(vis, _dot_nt(qa[h], lat) + _dot_nt(qr[h][:, :rw], kr), -jnp.inf) for lat, kr, rw, vis in segs]
         for h in hs]
    m = [functools.reduce(jnp.maximum, [jnp.max(x, axis=1, keepdims=True) for x in sh]) for sh in s]
    p = [[jnp.exp(x - m[h]) for x in s[h]] for h in hs]
    den = [sum(jnp.sum(x, axis=1, keepdims=True) for x in ph) for ph in p]
    ctx = [sum(_dot(x.astype(BF16), seg[0]) for x, seg in zip(p[h], segs)) / den[h] for h in hs]
    for h in hs:
        o_ref[:, sls[h]] = _dot(ctx[h].astype(BF16), wv[h]).astype(o_ref.dtype)


def mla_attention_latent(B, T, past, l, z_all, qr, lat_cache, kr_cache, lat_new, kr_new, w_kv):
    H = MLA_HEADS
    w, g = w_kv
    _, nj, K, tn = w.shape
    W = H * LANES
    out = pl.pallas_call(
        functools.partial(_attn_latent_kernel, T=T, past=past),
        grid=(B,),
        in_specs=[pl.BlockSpec((None, T, W), lambda b: (b, 0, OFF_QN // W)),
                  pl.BlockSpec((None, T, W), lambda b: (b, 0, 0)),
                  pl.BlockSpec((None, None, past, MLA_LATENT), lambda b: (l, b, 0, 0)),
                  pl.BlockSpec((None, None, past, MLA_ROPE), lambda b: (l, b, 0, 0)),
                  pl.BlockSpec((None, T, MLA_LATENT), lambda b: (b, 0, 0)),
                  pl.BlockSpec((None, T, LANES), lambda b: (b, 0, 0)),
                  pl.BlockSpec((None, nj, K, tn), lambda b: (g, 0, 0, 0))],
        out_specs=pl.BlockSpec((None, T, W), lambda b: (b, 0, 0)),
        out_shape=jax.ShapeDtypeStruct((B, T, H * MLA_V), BF16),
        compiler_params=_cparams(("parallel",)),
        name="mla_attention_latent",
    )(z_all.reshape(B, T, NZ), qr.reshape(B, T, W), lat_cache, kr_cache, lat_new.reshape(B, T, MLA_LATENT),
      kr_new.reshape(B, T, LANES), w)
    return out.reshape(B * T, H * MLA_V)


def _pair_sum_matrix():
    r = lax.broadcasted_iota(jnp.int32, (LANES, LANES), 0) // RW_HEAD
    c = lax.broadcasted_iota(jnp.int32, (LANES, LANES), 1) // RW_HEAD
    return (r == c).astype(BF16)


def _head_sum(x, bd):
    m = x.shape[0]
    parts = []
    for p in range(RW_WIDTH // LANES):
        hi, lo = _split_bf16(x[:, p * LANES:(p + 1) * LANES])
        both = _dot(jnp.concatenate([hi, lo], axis=0), bd)
        parts.append(both[:m] + both[m:])
    return jnp.concatenate(parts, axis=1)


def _rwkv_pre_kernel(r_ref, k_ref, v_ref, lo_ref, tr_ref, tk_ref, tv_ref, tlo_ref, fr_ref, fk_ref, fv_ref, flo_ref,
                     mu_ref, mulo_ref, w0_ref, w2_ref, a0_ref, a2_ref, g2_ref, kk_ref, ka_ref, rk_ref,
                     ro_ref, wo_ref, ko_ref, vo_ref, ao_ref, bo_ref, go_ref, bonus_ref, *, T):
    W = RW_WIDTH
    tm = r_ref.shape[0]
    row = lax.broadcasted_iota(jnp.int32, (tm, 1), 0)
    stream_start = (pl.program_id(0) * tm + row) % T == 0

    def mix(z_ref, tail_ref, first_ref, mu):
        z = z_ref[...]
        prev = jnp.where(row == 0, tail_ref[7:8, :], pltpu.roll(z, 1, 0))
        prev = jnp.where(stream_start, first_ref[...], prev)
        return z + (prev - z) * mu

    r = mix(r_ref, tr_ref, fr_ref, mu_ref[:, 0:W])
    k = mix(k_ref, tk_ref, fk_ref, mu_ref[:, W:2 * W])
    v = mix(v_ref, tv_ref, fv_ref, mu_ref[:, 2 * W:3 * W])
    lo = mix(lo_ref, tlo_ref, flo_ref, mulo_ref[...])
    wd, ad, gd = lo[:, 0:64], lo[:, 64:128], lo[:, 128:256]
    hdot = lambda x, w: _mm(x, w, "nn", 3)
    wpre = w0_ref[...] + hdot(jnp.tanh(wd), w2_ref[...])
    w_raw = jnp.minimum(wpre, 0.0) - jnp.log(1.0 + jnp.exp(-jnp.abs(wpre))) - 0.5
    log_decay = -jnp.exp(w_raw)
    a = _sigmoid(a0_ref[...] + hdot(ad, a2_ref[...]))
    g = hdot(_sigmoid(gd), g2_ref[...])
    bd = _pair_sum_matrix()
    kk = k * kk_ref[...]
    kk = kk / jnp.maximum(jnp.sqrt(_head_sum(kk * kk, bd)), 1e-12)
    k = k * (1.0 + (a - 1.0) * ka_ref[...])
    ro_ref[...] = r
    wo_ref[...] = log_decay
    ko_ref[...] = k
    vo_ref[...] = v
    ao_ref[...] = -kk
    bo_ref[...] = kk * a
    go_ref[...] = g
    bonus_ref[...] = _head_sum(r * k * rk_ref[...], bd) * v


def rwkv_pre(rows, z_all, shift0, mu, w0, w2, a0, a2, g2, k_k, k_a, r_k):
    M = rows.M
    tm = min(rows.tm, 256)
    W = RW_WIDTH
    zs = lambda w, off: pl.BlockSpec((tm, w), lambda i: (i, off // w))
    tail = lambda w, off: pl.BlockSpec((8, w), lambda i: (jnp.maximum(i * (tm // 8) - 1, 0), off // w))
    if rows.R == 1:
        first = lambda w: pl.BlockSpec((None, 1, w), lambda i: (i // (rows.T // tm), 0, 0))
    else:
        first = lambda w: pl.BlockSpec((None, tm, w), lambda i: (0, i, 0))
    full = lambda a: pl.BlockSpec(a.shape, lambda i: (0,) * a.ndim)
    row = lambda a: a.reshape(1, -1)
    firsts = [rows.expand(shift0[:, o:o + w]) for o, w in ((0, W), (W, W), (2 * W, W), (3 * W, RW_LORA))]
    params = [row(mu[:3 * W]), row(mu[3 * W:]), row(w0), w2, row(a0), a2, g2, row(k_k), row(k_a), row(r_k)]
    cols = ((W, OFF_RW_R), (W, OFF_RW_K), (W, OFF_RW_V), (RW_LORA, OFF_RW_LORA))
    outs = pl.pallas_call(
        functools.partial(_rwkv_pre_kernel, T=rows.T),
        grid=(M // tm,),
        in_specs=[zs(*c) for c in cols] + [tail(*c) for c in cols] + [first(c[0]) for c in cols]
        + [full(p) for p in params],
        out_specs=[pl.BlockSpec((tm, W), lambda i: (i, 0))] * 8,
        out_shape=[jax.ShapeDtypeStruct((M, W), F32)] * 8,
        compiler_params=_cparams(("parallel",)),
        name="rwkv_pre",
    )(*([z_all] * 8), *firsts, *params)
    return outs


_DIMS = {"nn": (((1,), (0,)), ((), ())), "nt": (((1,), (1,)), ((), ())), "tn": (((0,), (0,)), ((), ()))}


def _split_bf16(x):
    hi = x.astype(BF16)
    return hi, (x - hi.astype(F32)).astype(BF16)


def _mm(a, b, kind, passes):
    dims = _DIMS[kind]
    if passes == 6:
        return lax.dot_general(a, b, dims, precision=HIGHEST, preferred_element_type=F32)
    dg = lambda x, y: lax.dot_general(x, y, dims, preferred_element_type=F32)
    if passes == 1:
        return dg(a.astype(BF16), b.astype(BF16))
    ah, al = _split_bf16(a)
    bh, bl = _split_bf16(b)
    m = a.shape[0]
    if kind == "tn" or m < RW_HEAD:
        return dg(ah, bh) + (dg(al, bh) + dg(ah, bl))
    both = dg(jnp.concatenate([ah, al], axis=0), bh)
    return both[:m] + (both[m:] + dg(ah, bl))


def _rwkv_chunk_kernel(r_ref, lw_ref, k_ref, v_ref, a_ref, b_ref, S0_ref, y_ref, S_ref, *, L, P_G, P_T, P_X, P_S):
    c = pl.program_id(1)

    @pl.when(c == 0)
    def _():
        S_ref[...] = S0_ref[...]

    N = RW_HEAD
    row = lax.broadcasted_iota(jnp.int32, (L, L), 0)
    col = lax.broadcasted_iota(jnp.int32, (L, L), 1)
    incl = row >= col
    strict = row > col
    eye = (row == col).astype(F32)
    tri = incl.astype(F32)
    cum_all = jnp.dot(tri, lw_ref[...], precision=HIGHEST, preferred_element_type=F32)
    sls = [slice(h * N, (h + 1) * N) for h in range(RW_HEADS)]
    each = lambda f, *lists: [f(*xs) for xs in zip(*lists)]
    S0 = [S_ref[0, h] for h in range(RW_HEADS)]
    lw = [lw_ref[:, s] for s in sls]
    cum = [cum_all[:, s] for s in sls]
    V = [v_ref[:, s] for s in sls]
    g_in = each(jnp.exp, cum)
    g_inv = each(lambda c: jnp.exp(-c), cum)
    At = each(lambda s, c, w: a_ref[:, s] * jnp.exp(c - w), sls, cum, lw)
    Bt = each(lambda s, g: b_ref[:, s] * g, sls, g_inv)
    Kt = each(lambda s, g: k_ref[:, s] * g, sls, g_inv)
    Rt = each(lambda s, g: r_ref[:, s] * g, sls, g_in)
    right = each(lambda x, y: jnp.concatenate([x, y], axis=0), Bt, Kt)
    G = each(lambda x, y, rt: _mm(jnp.concatenate([x, y], axis=0), rt, "nt", P_G), At, Rt, right)
    Nab = each(lambda g: jnp.where(strict, g[:L, :L], 0.0), G)
    Nak = each(lambda g: jnp.where(strict, g[:L, L:], 0.0), G)
    Mrb = each(lambda g: jnp.where(incl, g[L:, :L], 0.0), G)
    Mrk = each(lambda g: jnp.where(incl, g[L:, L:], 0.0), G)
    T = each(lambda n: eye + jnp.where(row // 2 == col // 2, n, 0.0), Nab)
    s = 2
    while s < L:
        off = (row // (2 * s) == col // (2 * s)) & (row // s != col // s)
        T = each(lambda t, n: t + _mm(t, _mm(jnp.where(off, n, 0.0), t, "nn", P_T), "nn", P_T), T, Nab)
        s *= 2
    NV = each(lambda n, v: _mm(n, v, "nn", P_X), Nak, V)
    TX = each(lambda t, x, nv: _mm(t, jnp.concatenate([x, nv], axis=1), "nn", P_X), T, At, NV)
    MX = each(lambda m, tx: _mm(m, tx, "nn", P_X), Mrb, TX)
    MV = each(lambda m, v: _mm(m, v, "nn", P_X), Mrk, V)
    Y = each(lambda rt, mx, mv, s0: _mm(rt + mx[:, :N], s0, "nt", P_S) + (mx[:, N:] + mv), Rt, MX, MV, S0)
    U = each(lambda tx, s0: _mm(tx[:, :N], s0, "nt", P_S) + tx[:, N:], TX, S0)
    upd = each(lambda u, v, bt, kt: _mm(u, bt, "tn", P_S) + _mm(v, kt, "tn", P_S), U, V, Bt, Kt)
    for h, s in enumerate(sls):
        y_ref[:, s] = Y[h]
        S_ref[0, h] = (S0[h] + upd[h]) * g_in[h][L - 1:L, :]


def rwkv_scan(B, T, r, lw, k, v, a, b, S0):
    L = CHUNK if T % CHUNK == 0 else T
    assert L & (L - 1) == 0
    nc = T // L
    xs = pl.BlockSpec((L, RW_WIDTH), lambda bi, c: (bi * nc + c, 0))
    ss = pl.BlockSpec((1, RW_HEADS, RW_HEAD, RW_HEAD), lambda bi, c: (bi, 0, 0, 0))
    return pl.pallas_call(
        functools.partial(_rwkv_chunk_kernel, L=L, P_G=3, P_T=3, P_X=3, P_S=3),
        grid=(B, nc),
        in_specs=[xs] * 6 + [ss],
        out_specs=[xs, ss],
        out_shape=[jax.ShapeDtypeStruct((B * T, RW_WIDTH), F32),
                   jax.ShapeDtypeStruct((B, RW_HEADS, RW_HEAD, RW_HEAD), F32)],
        compiler_params=_cparams(("parallel", "arbitrary")),
        name="rwkv_scan",
    )(r, lw, k, v, a, b, S0)


def _rwkv_post_kernel(y_ref, g_ref, bonus_ref, lg_ref, lb_ref, o_ref):
    bd = _pair_sum_matrix()
    y = y_ref[...]
    mean = _head_sum(y, bd) * (1.0 / RW_HEAD)
    d = y - mean
    var = _head_sum(d * d, bd) * (1.0 / RW_HEAD)
    yn = d * lax.rsqrt(var + RW_LN_EPS) * lg_ref[...] + lb_ref[...]
    o_ref[...] = ((yn + bonus_ref[...]) * g_ref[...]).astype(o_ref.dtype)


def rwkv_post(rows, y, g, bonus, lnx_g, lnx_b):
    M = rows.M
    tm = min(rows.tm, 256)
    W = RW_WIDTH
    xs = pl.BlockSpec((tm, W), lambda i: (i, 0))
    ps = pl.BlockSpec((1, W), lambda i: (0, 0))
    return pl.pallas_call(
        _rwkv_post_kernel,
        grid=(M // tm,),
        in_specs=[xs, xs, xs, ps, ps],
        out_specs=xs,
        out_shape=jax.ShapeDtypeStruct((M, W), BF16),
        compiler_params=_cparams(("parallel",)),
        name="rwkv_post",
    )(y, g, bonus, lnx_g.reshape(1, W), lnx_b.reshape(1, W))


def _cast_blocks_kernel(x_ref, o_ref):
    o_ref[...] = x_ref[...].astype(o_ref.dtype)


def cast_blocks(w):
    lead = w.shape[:-2]
    K, N = w.shape[-2:]
    G = int(np.prod(lead))
    tk = _row_tile(K, 1024)

    def src(g, j, k):
        idx = []
        for d in reversed(lead):
            idx.append(g % d)
            g = g // d
        return tuple(reversed(idx)) + (k, j)

    return pl.pallas_call(
        _cast_blocks_kernel,
        grid=(G, N // TN, K // tk),
        in_specs=[pl.BlockSpec((None,) * len(lead) + (tk, TN), src)],
        out_specs=pl.BlockSpec((None, None, tk, TN), lambda g, j, k: (g, j, k, 0)),
        out_shape=jax.ShapeDtypeStruct((G, N // TN, K, TN), BF16),
        compiler_params=_cparams(("parallel", "parallel", "parallel")),
        name="cast_blocks",
    )(w)


SRC_IF = 3072
SRC_Q = SRC_IF + 2 * ML_HEADS
SRC_LAT = SRC_Q + MLA_HEADS * (MLA_NOPE + MLA_ROPE)
SRC_KR = SRC_LAT + MLA_LATENT
SRC_RW = SRC_KR + MLA_ROPE
SRC_LO = SRC_RW + 3 * RW_WIDTH
SRC_GATE = SRC_LO + RW_LORA
J_QN, J_QR, J_RW, J_GATE, J_TAIL = OFF_QN // TN_IN, OFF_QR // TN_IN, OFF_RW_R // TN_IN, OFF_GATE // TN_IN, OFF_LAT // TN_IN


def _prep_w_in_t_kernel(main_ref, q_ref, lo_ref, if_ref, o_ref):
    j = pl.program_id(2)
    kw = main_ref.shape[1]
    head = MLA_NOPE + MLA_ROPE

    def put(rows):
        o_ref[...] = rows.T.astype(o_ref.dtype)

    @pl.when((j != J_QN) & (j != J_QR) & (j != J_TAIL))
    def _():
        put(main_ref[...])

    @pl.when(j == J_QN)
    def _():
        put(jnp.concatenate([q_ref[h * head:h * head + MLA_NOPE, :] for h in range(MLA_HEADS)], axis=0))

    @pl.when(j == J_QR)
    def _():
        zeros = jnp.zeros((LANES - MLA_ROPE, kw), F32)
        put(jnp.concatenate([x for h in range(MLA_HEADS)
                             for x in (q_ref[h * head + MLA_NOPE:(h + 1) * head, :], zeros)], axis=0))

    @pl.when(j == J_TAIL)
    def _():
        put(jnp.concatenate([main_ref[0:MLA_LATENT, :], lo_ref[...],
                             main_ref[MLA_LATENT:MLA_LATENT + MLA_ROPE, :], jnp.zeros((LANES - MLA_ROPE, kw), F32),
                             if_ref[...], jnp.zeros((LANES - 2 * ML_HEADS, kw), F32)], axis=0))


def prep_w_in_t(w):
    G, K, N = w.shape
    assert (OFF_ML_Q, OFF_QN, OFF_QR, OFF_RW_R, OFF_GATE, OFF_LAT) == (0, 3072, 4096, 5120, 8192, 14336) and TN_IN == 1024
    assert (OFF_RW_LORA, OFF_KR, OFF_IF) == (OFF_LAT + 512, OFF_LAT + 768, OFF_LAT + 896) and N == SRC_GATE + 3 * D_MODEL
    wt = jnp.swapaxes(w, 1, 2)
    kw = K // 2

    def main_row(j):
        row = jnp.where(j < J_QN, j * TN_IN,
                        jnp.where(j < J_RW, SRC_Q,
                                  jnp.where(j < J_GATE, SRC_RW + (j - J_RW) * TN_IN,
                                            jnp.where(j < J_TAIL, SRC_GATE + (j - J_GATE) * TN_IN, SRC_LAT))))
        return pl.multiple_of(row, 8)

    el = lambda n: pl.Element(n)
    return pl.pallas_call(
        _prep_w_in_t_kernel,
        grid=(G, 2, NZ // TN_IN),
        in_specs=[pl.BlockSpec((None, el(TN_IN), el(kw)), lambda g, kh, j: (g, main_row(j), kh * kw)),
                  pl.BlockSpec((None, el(MLA_HEADS * (MLA_NOPE + MLA_ROPE)), el(kw)), lambda g, kh, j: (g, SRC_Q, kh * kw)),
                  pl.BlockSpec((None, el(RW_LORA), el(kw)), lambda g, kh, j: (g, SRC_LO, kh * kw)),
                  pl.BlockSpec((None, el(2 * ML_HEADS), el(kw)), lambda g, kh, j: (g, SRC_IF, kh * kw))],
        out_specs=pl.BlockSpec((None, None, kw, TN_IN), lambda g, kh, j: (g, j, kh, 0)),
        out_shape=jax.ShapeDtypeStruct((G, NZ // TN_IN, K, TN_IN), BF16),
        compiler_params=_cparams(("parallel", "parallel", "arbitrary")),
        name="prep_w_in",
    )(wt, wt, wt, wt)


def _row_tile(M, cap):
    return max(t for t in range(8, cap + 1, 8) if M % t == 0)


def _rope_tables(T, past, reps):
    half = MLA_ROPE // 2
    inv = ROPE_BASE ** (-jnp.arange(half, dtype=F32) / half)
    ang = (past + jnp.arange(T)).astype(F32)[:, None] * inv[None, :]
    cos, sin = jnp.cos(ang), jnp.sin(ang)
    zeros = jnp.zeros((T, LANES - MLA_ROPE), F32)
    cos_t = jnp.concatenate([cos, cos, zeros], axis=1)
    sin_t = jnp.concatenate([-sin, sin, zeros], axis=1)
    return jnp.tile(cos_t, (reps, 1)), jnp.tile(sin_t, (reps, 1))


def _layer(rows, x, mod, l, W, P, lat_cache, kr_cache, C0, n0, m0, S0, shift0, tabs):
    B, T, M = rows.B, rows.T, rows.M
    D = D_MODEL
    m9 = [rows.expand(mod[:, i * D:(i + 1) * D]) for i in range(N_ADA)]
    sh1, sc1, g1, sh2, sc2, g2, sh3, sc3, g3 = m9
    act = mm_swiglu(rows, x, P['norm_g'][l, 0], sh1, sc1, (W['ffn_in'], 2 * l))
    x = mm_resid(rows, act, (W['ffn_out'], 2 * l), x, g1, 0.5)
    z_all = mm_norm(rows, x, P['norm_g'][l, 1], sh2, sc2, (W['w_in'], l), name="mm_in")
    bias_row = jnp.pad(jnp.concatenate([P['mlstm_i_bias'][l], P['mlstm_f_bias'][l]]), (0, LANES - 2 * ML_HEADS))[None]
    om, C, n, m = mlstm(B, T, z_all, bias_row, P['mlstm_head_g'][l], C0, n0, m0)
    past = lat_cache.shape[2]
    qr, lat, kr = mla_pre(rows, z_all, tabs[0], tabs[1], P['mla_kv_norm_g'][l])
    Sk = past + T
    if past and T <= LATENT_ATTN_MAX_T:
        oa = mla_attention_latent(B, T, past, l, z_all, qr, lat_cache, kr_cache, lat, kr, (W['kv_b'], l))
    else:
        if past:
            lat_all = jnp.concatenate([lat_cache[l], lat.reshape(B, T, -1)], axis=1).reshape(B * Sk, -1)
            kr_all = jnp.concatenate([jnp.pad(kr_cache[l], ((0, 0), (0, 0), (0, LANES - MLA_ROPE))),
                                      kr.reshape(B, T, -1)], axis=1).reshape(B * Sk, LANES)
        else:
            lat_all, kr_all = lat, kr
        kv = matmul(lat_all, (W['kv_b'], l), _row_tile(B * Sk, 1536), out_dtype=BF16, name="mm_kv")
        oa = mla_attention(B, T, Sk, past, z_all, qr, kv, kr_all.astype(BF16))
    r_, w_, k_, v_, a_, b_, g_, bonus = rwkv_pre(rows, z_all, shift0, P['rwkv_mu'][l], P['rwkv_w0'][l], P['rwkv_w2'][l],
                                                 P['rwkv_a0'][l], P['rwkv_a2'][l], P['rwkv_g2'][l], P['rwkv_k_k'][l],
                                                 P['rwkv_k_a'][l], P['rwkv_r_k'][l].reshape(-1))
    y, S = rwkv_scan(B, T, r_, w_, k_, v_, a_, b_, S0)
    orw = rwkv_post(rows, y, g_, bonus, P['rwkv_lnx_g'][l], P['rwkv_lnx_b'][l])
    z_last = z_all.reshape(B, T, NZ)[:, -1]
    shift = jnp.concatenate([z_last[:, OFF_RW_R:OFF_RW_R + 3 * RW_WIDTH], z_last[:, OFF_RW_LORA:OFF_RW_LORA + RW_LORA]],
                            axis=1)
    mixed = mm_branch(rows, om, oa, orw, (W['branch'], 3 * l), z_all, P['b_merge'][l][None])
    x = mm_resid(rows, mixed, (W['w_out'], l), x, g2, 1.0)
    act = mm_swiglu(rows, x, P['norm_g'][l, 2], sh3, sc3, (W['ffn_in'], 2 * l + 1))
    x = mm_resid(rows, act, (W['ffn_out'], 2 * l + 1), x, g3, 0.5)
    new = (lat.reshape(B, T, MLA_LATENT), kr.reshape(B, T, LANES)[:, :, :MLA_ROPE], C, n, m, S, shift)
    return x, new


def _trunk(rows, x, mods, W, P, lat_cache, kr_cache, C_st, n_st, m_st, S_st, sh_st):
    depth = len(mods)
    past = lat_cache.shape[2]
    tabs = _rope_tables(rows.T, past, rows.B)
    x = x.reshape(rows.M, D_MODEL)
    new = []
    for l in range(depth):
        x, st = _layer(rows, x, mods[l], l, W, P, lat_cache, kr_cache, C_st[l], n_st[l], m_st[l], S_st[l],
                       sh_st[l], tabs)
        new.append(st)
    y = final_norm(rows, x, P['final_norm_g']).reshape(rows.B, rows.T, D_MODEL)
    return y, tuple(jnp.stack(s, axis=0) for s in zip(*new))


def kernel(x_prompt, x_sample, c_prompt, c_sample, cache_mla_latent, cache_mla_krope, state_mlstm_C, state_mlstm_n,
           state_mlstm_m, state_rwkv_S, state_rwkv_shift, ada_w, ada_b, norm_g, ffn_w_in, ffn_w_out, w_in,
           mlstm_i_bias, mlstm_f_bias, mlstm_head_g, mla_kv_norm_g, mla_w_kv_b, rwkv_mu, rwkv_w0, rwkv_w2, rwkv_a0,
           rwkv_a2, rwkv_g2, rwkv_k_k, rwkv_k_a, rwkv_r_k, rwkv_lnx_g, rwkv_lnx_b, w_branch, b_merge, w_out,
           final_norm_g):
    P = dict(norm_g=norm_g, mlstm_i_bias=mlstm_i_bias, mlstm_f_bias=mlstm_f_bias, mlstm_head_g=mlstm_head_g,
             mla_kv_norm_g=mla_kv_norm_g, rwkv_mu=rwkv_mu, rwkv_w0=rwkv_w0, rwkv_w2=rwkv_w2, rwkv_a0=rwkv_a0,
             rwkv_a2=rwkv_a2, rwkv_g2=rwkv_g2, rwkv_k_k=rwkv_k_k, rwkv_k_a=rwkv_k_a, rwkv_r_k=rwkv_r_k,
             rwkv_lnx_g=rwkv_lnx_g, rwkv_lnx_b=rwkv_lnx_b, b_merge=b_merge, final_norm_g=final_norm_g)
    depth = w_in.shape[0]
    W = dict(ffn_in=cast_blocks(ffn_w_in), ffn_out=cast_blocks(ffn_w_out), w_in=prep_w_in_t(w_in),
             kv_b=cast_blocks(mla_w_kv_b), branch=cast_blocks(w_branch), w_out=cast_blocks(w_out))
    Bp, Tp, _ = x_prompt.shape
    Bs, Ts, _ = x_sample.shape
    n_c = Bp + Bs
    c_all = jnp.pad(jnp.concatenate([c_prompt, c_sample], axis=0), ((0, (-n_c) % 8), (0, 0)))
    mods = [ada_mod(c_all, ada_w, ada_b, l) for l in range(depth)]
    rows_p = Rows(Bp, Tp, TM_MAX)
    rows_s = Rows(Bs, Ts, TM_MAX)
    zeros = lambda *s: jnp.zeros((depth, Bp) + s, F32)
    y_p, st_p = _trunk(rows_p, x_prompt, [m[:Bp] for m in mods], W, P,
                       zeros(0, MLA_LATENT), zeros(0, MLA_ROPE), zeros(ML_HEADS, ML_DV, ML_DQK), zeros(ML_HEADS, ML_DQK),
                       zeros(ML_HEADS), zeros(RW_HEADS, RW_HEAD, RW_HEAD), zeros(RW_COLS))
    y_s, st_s = _trunk(rows_s, x_sample, [m[Bp:n_c] for m in mods], W, P,
                       cache_mla_latent, cache_mla_krope, state_mlstm_C, state_mlstm_n, state_mlstm_m, state_rwkv_S,
                       state_rwkv_shift)
    return (y_p, y_s) + st_p + st_s
```
